```python
import math
import jax, jax.numpy as jnp
from jax import lax
import numpy as np

D_MODEL = 1024
BATCH = 16
SEQ = 2048
DEPTH = 2

N_EVEN = (DEPTH + 1) // 2
N_ODD = DEPTH // 2
EPS = 1e-6
D_FF = 2816
FFN_RES = 0.5
CONV_W = 4

SSD_D_INNER = D_MODEL
SSD_HEAD_DIM = 64
SSD_HEADS = SSD_D_INNER // SSD_HEAD_DIM
SSD_GROUPS = 2
SSD_D_STATE = 128
SSD_BC = SSD_GROUPS * SSD_D_STATE
SSD_CONV_CH = SSD_D_INNER + 2 * SSD_BC
SSD_CHUNK = 128

ML_D_INNER = D_MODEL
ML_HEADS = 4
ML_HEAD_DIM = ML_D_INNER // ML_HEADS
ML_QKV_BLOCK = 4
ML_QKV_BLOCKS = ML_D_INNER // ML_QKV_BLOCK
ML_CHUNK = 128

IN_COLS = SSD_D_INNER + SSD_CONV_CH + SSD_HEADS + 2 * ML_D_INNER
MIX_WIDTH = SSD_D_INNER + ML_D_INNER

S5_GROUP = 16
S5_GROUPS = D_MODEL // S5_GROUP
S5_STATE = 64

kernel_name = 'hybrid_ssd_mlstm_s5_macaron'


def _rmsnorm(x, w):
    xf = x.astype(jnp.float32)
    y = xf * lax.rsqrt(jnp.mean(xf * xf, axis=-1, keepdims=True) + EPS)
    return (y * w.astype(jnp.float32)).astype(x.dtype)


def _swiglu(x, w_gate, w_up, w_down):
    return (jax.nn.silu(x @ w_gate) * (x @ w_up)) @ w_down


def _causal_dwconv(x, w, b):
    k_w, s = w.shape[0], x.shape[1]
    xp = jnp.pad(x, ((0, 0), (k_w - 1, 0), (0, 0)))
    y = b
    for j in range(k_w):
        y = y + xp[:, j:j + s] * w[j]
    return y


def _segsum(x):
    t = x.shape[-1]
    cs = jnp.cumsum(x, axis=-1)
    d = cs[..., :, None] - cs[..., None, :]
    return jnp.where(jnp.tril(jnp.ones((t, t), dtype=bool)), d, -jnp.inf)


def _headwise(x, w):
    nb, o, i = w.shape
    y = jnp.einsum('bsni,noi->bsno', x.reshape(x.shape[:-1] + (nb, i)), w)
    return y.reshape(x.shape[:-1] + (nb * o,))


def _ssd_chunked(x, dt, a, bm, cm):
    bsz, s, h, p = x.shape
    g, n = bm.shape[2], bm.shape[3]
    r, l = h // g, SSD_CHUNK
    c = s // l
    xd = (x * dt[..., None]).reshape(bsz, c, l, g, r, p)
    adt = (dt * a).reshape(bsz, c, l, g, r).transpose(0, 1, 3, 4, 2)
    bc = bm.reshape(bsz, c, l, g, n)
    cc = cm.reshape(bsz, c, l, g, n)
    a_cs = jnp.cumsum(adt, axis=-1)
    lmat = jnp.exp(_segsum(adt))
    cb = jnp.einsum('bclgn,bcsgn->bcgls', cc, bc)
    y_diag = jnp.einsum('bcgls,bcgrls,bcsgrp->bclgrp', cb, lmat, xd)
    decay_states = jnp.exp(a_cs[..., -1:] - a_cs)
    states = jnp.einsum('bclgn,bcgrl,bclgrp->bcgrpn', bc, decay_states, xd)
    states = jnp.concatenate([jnp.zeros_like(states[:, :1]), states], axis=1)
    chunk_tot = jnp.pad(a_cs[..., -1].transpose(0, 2, 3, 1), ((0, 0), (0, 0), (0, 0), (1, 0)))
    decay_chunk = jnp.exp(_segsum(chunk_tot))
    states = jnp.einsum('bgrzc,bcgrpn->bzgrpn', decay_chunk, states)[:, :-1]
    y_off = jnp.einsum('bclgn,bcgrpn,bcgrl->bclgrp', cc, states, jnp.exp(a_cs))
    return (y_diag + y_off).reshape(bsz, s, h, p)


def _mlstm_chunkwise(q, k, v, i_pre, f_pre):
    bsz, s, h, d = q.shape
    l = ML_CHUNK
    c = s // l
    q = q.astype(jnp.float32)
    k = k.astype(jnp.float32) / math.sqrt(d)
    v = v.astype(jnp.float32)
    to_chunks = lambda t: t.reshape(bsz, c, l, h, -1).transpose(0, 1, 3, 2, 4)
    qc, kc, vc = to_chunks(q), to_chunks(k), to_chunks(v)
    logf = jax.nn.log_sigmoid(f_pre).reshape(bsz, c, l, h).transpose(0, 1, 3, 2)
    ig = i_pre.reshape(bsz, c, l, h).transpose(0, 1, 3, 2)
    bcum = jnp.cumsum(logf, axis=-1)
    dlog = bcum[..., :, None] - bcum[..., None, :] + ig[..., None, :]
    dlog = jnp.where(jnp.tril(jnp.ones((l, l), dtype=bool)), dlog, -jnp.inf)
    b_last = bcum[..., -1]
    w_state = b_last[..., None] - bcum + ig

    def step(carry, inp):
        c_prev, n_prev, m_prev = carry
        kk, vv, bl, ws = inp
        m_new = jnp.maximum(bl + m_prev, jnp.max(ws, axis=-1))
        decay = jnp.exp(bl + m_prev - m_new)
        wts = jnp.exp(ws - m_new[..., None])
        c_new = decay[..., None, None] * c_prev + jnp.einsum('bhl,bhlv,bhlk->bhvk', wts, vv, kk)
        n_new = decay[..., None] * n_prev + jnp.einsum('bhl,bhlk->bhk', wts, kk)
        return (c_new, n_new, m_new), (c_prev, n_prev, m_prev)

    init = (jnp.zeros((bsz, h, d, d), jnp.float32), jnp.zeros((bsz, h, d), jnp.float32),
            jnp.full((bsz, h), -1e30, jnp.float32))
    xs = (jnp.moveaxis(kc, 1, 0), jnp.moveaxis(vc, 1, 0), jnp.moveaxis(b_last, 1, 0), jnp.moveaxis(w_state, 1, 0))
    _, (c_all, n_all, m_all) = lax.scan(step, init, xs)
    c_all, n_all, m_all = jnp.moveaxis(c_all, 0, 1), jnp.moveaxis(n_all, 0, 1), jnp.moveaxis(m_all, 0, 1)

    m_inter = bcum + m_all[..., None]
    m_t = jnp.maximum(jnp.max(dlog, axis=-1), m_inter)
    scores = jnp.einsum('bchld,bchsd->bchls', qc, kc) * jnp.exp(dlog - m_t[..., None])
    inter_w = jnp.exp(m_inter - m_t)
    num = jnp.einsum('bchls,bchsv->bchlv', scores, vc) + inter_w[..., None] * jnp.einsum('bchlk,bchvk->bchlv', qc, c_all)
    den = jnp.sum(scores, axis=-1) + inter_w * jnp.einsum('bchlk,bchk->bchl', qc, n_all)
    hout = num / jnp.maximum(jnp.abs(den), jnp.exp(-m_t))[..., None]
    return hout.transpose(0, 1, 3, 2, 4).reshape(bsz, s, h, d)


def _hybrid_mixer(u, w_in, ssd_conv_w, ssd_conv_b, ssd_dt_bias, ssd_a_log, ssd_d, ssd_norm_w,
                  ml_conv_w, ml_conv_b, ml_w_q, ml_w_k, ml_w_v, ml_w_if, ml_b_if, ml_norm_w, ml_skip, w_out):
    bsz, s, _ = u.shape
    f32 = jnp.float32
    proj = u @ w_in
    o1 = SSD_D_INNER
    o2 = o1 + SSD_CONV_CH
    o3 = o2 + SSD_HEADS
    o4 = o3 + ML_D_INNER
    z_s, xbc, dt_raw, m_x, m_z = proj[..., :o1], proj[..., o1:o2], proj[..., o2:o3], proj[..., o3:o4], proj[..., o4:]
    xbc = jax.nn.silu(_causal_dwconv(xbc, ssd_conv_w, ssd_conv_b))
    xs = xbc[..., :SSD_D_INNER].reshape(bsz, s, SSD_HEADS, SSD_HEAD_DIM).astype(f32)
    bm = xbc[..., SSD_D_INNER:SSD_D_INNER + SSD_BC].reshape(bsz, s, SSD_GROUPS, SSD_D_STATE).astype(f32)
    cm = xbc[..., SSD_D_INNER + SSD_BC:].reshape(bsz, s, SSD_GROUPS, SSD_D_STATE).astype(f32)
    dt = jax.nn.softplus(dt_raw.astype(f32) + ssd_dt_bias.astype(f32))
    a = -jnp.exp(ssd_a_log.astype(f32))
    y = _ssd_chunked(xs, dt, a, bm, cm) + ssd_d.astype(f32)[:, None] * xs
    yg = (y.reshape(bsz, s, SSD_D_INNER) * jax.nn.silu(z_s.astype(f32))).reshape(bsz, s, SSD_GROUPS, -1)
    yg = yg * lax.rsqrt(jnp.mean(yg * yg, axis=-1, keepdims=True) + EPS)
    y_ssd = yg.reshape(bsz, s, SSD_D_INNER) * ssd_norm_w.astype(f32)
    xc = jax.nn.silu(_causal_dwconv(m_x, ml_conv_w, ml_conv_b))
    q = _headwise(xc, ml_w_q)
    k = _headwise(xc, ml_w_k)
    v = _headwise(m_x, ml_w_v)
    gates = (jnp.concatenate([q, k, v], axis=-1) @ ml_w_if + ml_b_if).astype(f32)
    hs = (bsz, s, ML_HEADS, ML_HEAD_DIM)
    hm = _mlstm_chunkwise(q.reshape(hs), k.reshape(hs), v.reshape(hs), gates[..., :ML_HEADS], gates[..., ML_HEADS:])
    mu = jnp.mean(hm, axis=-1, keepdims=True)
    var = jnp.mean(jnp.square(hm - mu), axis=-1, keepdims=True)
    hm = ((hm - mu) * lax.rsqrt(var + EPS)).reshape(bsz, s, ML_D_INNER) * ml_norm_w.astype(f32)
    y_ml = (hm + ml_skip.astype(f32) * xc.astype(f32)) * jax.nn.silu(m_z.astype(f32))
    y = jnp.concatenate([y_ssd, y_ml], axis=-1).astype(u.dtype)
    return y @ w_out


def _s5_mixer(u, a_re, a_im, log_step, b_re, b_im, c_re, c_im, d_skip, w_a, b_a, w_b, b_b):
    bsz, s, _ = u.shape
    f32 = jnp.float32
    a_re, a_im = a_re.astype(f32), a_im.astype(f32)
    b_re, b_im, c_re, c_im = b_re.astype(f32), b_im.astype(f32), c_re.astype(f32), c_im.astype(f32)
    step = jnp.exp(log_step.astype(f32))[:, None]
    mag = jnp.exp(a_re * step)
    lam_re, lam_im = mag * jnp.cos(a_im * step), mag * jnp.sin(a_im * step)
    den = a_re * a_re + a_im * a_im
    coef_re = ((lam_re - 1.0) * a_re + lam_im * a_im) / den
    coef_im = (lam_im * a_re - (lam_re - 1.0) * a_im) / den
    bb_re = coef_re[..., None] * b_re - coef_im[..., None] * b_im
    bb_im = coef_re[..., None] * b_im + coef_im[..., None] * b_re
    ug = u.astype(f32).reshape(bsz, s, S5_GROUPS, S5_GROUP)
    bu_re = jnp.einsum('bsgc,gnc->bsgn', ug, bb_re)
    bu_im = jnp.einsum('bsgc,gnc->bsgn', ug, bb_im)
    a_seq_re = jnp.broadcast_to(lam_re, (s,) + lam_re.shape)
    a_seq_im = jnp.broadcast_to(lam_im, (s,) + lam_im.shape)

    def combine(e1, e2):
        a1r, a1i, b1r, b1i = e1
        a2r, a2i, b2r, b2i = e2
        return (a2r * a1r - a2i * a1i, a2r * a1i + a2i * a1r,
                a2r * b1r - a2i * b1i + b2r, a2r * b1i + a2i * b1r + b2i)

    def scan_one(br, bi):
        _, _, xr, xi = lax.associative_scan(combine, (a_seq_re, a_seq_im, br, bi), axis=0)
        return xr, xi

    x_re, x_im = jax.vmap(scan_one)(bu_re, bu_im)
    y = jnp.einsum('bsgn,gcn->bsgc', x_re, c_re) - jnp.einsum('bsgn,gcn->bsgc', x_im, c_im)
    y = y.reshape(bsz, s, D_MODEL) + d_skip.astype(f32) * u.astype(f32)
    g = jax.nn.gelu(y).astype(u.dtype)
    return (g @ w_a + b_a) * jax.nn.sigmoid(g @ w_b + b_b)


def setup_inputs(seed: int = 0) -> dict:
    key = jax.random.key(seed)
    keys = iter(jax.random.split(key, 64))
    nk = lambda: next(keys)
    nrm = lambda shape, scale: jax.random.normal(nk(), shape, jnp.float32) * scale
    gain = lambda shape: 1.0 + nrm(shape, 0.02)
    uni = lambda shape, lo, hi: jax.random.uniform(nk(), shape, jnp.float32, lo, hi)
    x = nrm((BATCH, SEQ, D_MODEL), 1.0)
    ssd_dt = jnp.exp(uni((N_EVEN, SSD_HEADS), math.log(1e-3), math.log(1e-1)))
    f_bias = jnp.broadcast_to(jnp.linspace(3.0, 6.0, ML_HEADS, dtype=jnp.float32), (N_EVEN, ML_HEADS)) + nrm((N_EVEN, ML_HEADS), 0.01)
    i_bias = nrm((N_EVEN, ML_HEADS), 0.1)
    s5_a_im = jnp.broadcast_to(math.pi * jnp.arange(S5_STATE, dtype=jnp.float32), (N_ODD, S5_GROUPS, S5_STATE))
    return {
        'x': x,
        'ffn1_norm': gain((DEPTH, D_MODEL)),
        'ffn1_w_gate': nrm((DEPTH, D_MODEL, D_FF), D_MODEL ** -0.5),
        'ffn1_w_up': nrm((DEPTH, D_MODEL, D_FF), D_MODEL ** -0.5),
        'ffn1_w_down': nrm((DEPTH, D_FF, D_MODEL), D_FF ** -0.5),
        'mix_norm': gain((DEPTH, D_MODEL)),
        'ffn2_norm': gain((DEPTH, D_MODEL)),
        'ffn2_w_gate': nrm((DEPTH, D_MODEL, D_FF), D_MODEL ** -0.5),
        'ffn2_w_up': nrm((DEPTH, D_MODEL, D_FF), D_MODEL ** -0.5),
        'ffn2_w_down': nrm((DEPTH, D_FF, D_MODEL), D_FF ** -0.5),
        'hy_w_in': nrm((N_EVEN, D_MODEL, IN_COLS), D_MODEL ** -0.5),
        'ssd_conv_w': nrm((N_EVEN, CONV_W, SSD_CONV_CH), CONV_W ** -0.5),
        'ssd_conv_b': nrm((N_EVEN, SSD_CONV_CH), 0.02),
        'ssd_dt_bias': ssd_dt + jnp.log(-jnp.expm1(-ssd_dt)),
        'ssd_a_log': jnp.log(uni((N_EVEN, SSD_HEADS), 1.0, 16.0)),
        'ssd_d': 1.0 + nrm((N_EVEN, SSD_HEADS), 0.1),
        'ssd_norm_w': gain((N_EVEN, SSD_D_INNER)),
        'ml_conv_w': nrm((N_EVEN, CONV_W, ML_D_INNER), CONV_W ** -0.5),
        'ml_conv_b': nrm((N_EVEN, ML_D_INNER), 0.02),
        'ml_w_q': nrm((N_EVEN, ML_QKV_BLOCKS, ML_QKV_BLOCK, ML_QKV_BLOCK), ML_QKV_BLOCK ** -0.5),
        'ml_w_k': nrm((N_EVEN, ML_QKV_BLOCKS, ML_QKV_BLOCK, ML_QKV_BLOCK), ML_QKV_BLOCK ** -0.5),
        'ml_w_v': nrm((N_EVEN, ML_QKV_BLOCKS, ML_QKV_BLOCK, ML_QKV_BLOCK), ML_QKV_BLOCK ** -0.5),
        'ml_w_if': nrm((N_EVEN, 3 * ML_D_INNER, 2 * ML_HEADS), 0.5 * (3 * ML_D_INNER) ** -0.5),
        'ml_b_if': jnp.concatenate([i_bias, f_bias], axis=-1),
        'ml_norm_w': gain((N_EVEN, ML_D_INNER)),
        'ml_skip': gain((N_EVEN, ML_D_INNER)),
        'hy_w_out': nrm((N_EVEN, MIX_WIDTH, D_MODEL), MIX_WIDTH ** -0.5),
        's5_a_re': -0.5 + nrm((N_ODD, S5_GROUPS, S5_STATE), 0.01),
        's5_a_im': s5_a_im + nrm((N_ODD, S5_GROUPS, S5_STATE), 0.01),
        's5_log_step': uni((N_ODD, S5_GROUPS), math.log(1e-3), math.log(1e-1)),
        's5_b_re': nrm((N_ODD, S5_GROUPS, S5_STATE, S5_GROUP), (2.0 * S5_GROUP) ** -0.5),
        's5_b_im': nrm((N_ODD, S5_GROUPS, S5_STATE, S5_GROUP), (2.0 * S5_GROUP) ** -0.5),
        's5_c_re': nrm((N_ODD, S5_GROUPS, S5_GROUP, S5_STATE), (2.0 * S5_STATE) ** -0.5),
        's5_c_im': nrm((N_ODD, S5_GROUPS, S5_GROUP, S5_STATE), (2.0 * S5_STATE) ** -0.5),
        's5_d': nrm((N_ODD, D_MODEL), 1.0),
        's5_w_a': nrm((N_ODD, D_MODEL, D_MODEL), D_MODEL ** -0.5),
        's5_b_a': nrm((N_ODD, D_MODEL), 0.02),
        's5_w_b': nrm((N_ODD, D_MODEL, D_MODEL), D_MODEL ** -0.5),
        's5_b_b': nrm((N_ODD, D_MODEL), 0.02),
        'final_norm': gain((D_MODEL,)),
    }


def reference(x, ffn1_norm, ffn1_w_gate, ffn1_w_up, ffn1_w_down, mix_norm, ffn2_norm, ffn2_w_gate, ffn2_w_up, ffn2_w_down,
              hy_w_in, ssd_conv_w, ssd_conv_b, ssd_dt_bias, ssd_a_log, ssd_d, ssd_norm_w,
              ml_conv_w, ml_conv_b, ml_w_q, ml_w_k, ml_w_v, ml_w_if, ml_b_if, ml_norm_w, ml_skip, hy_w_out,
              s5_a_re, s5_a_im, s5_log_step, s5_b_re, s5_b_im, s5_c_re, s5_c_im, s5_d, s5_w_a, s5_b_a, s5_w_b, s5_b_b,
              final_norm):
    for layer in range(DEPTH):
        h = _rmsnorm(x, ffn1_norm[layer])
        x = x + FFN_RES * _swiglu(h, ffn1_w_gate[layer], ffn1_w_up[layer], ffn1_w_down[layer])
        u = _rmsnorm(x, mix_norm[layer])
        if layer % 2 == 0:
            e = layer // 2
            y = _hybrid_mixer(u, hy_w_in[e], ssd_conv_w[e], ssd_conv_b[e], ssd_dt_bias[e], ssd_a_log[e], ssd_d[e], ssd_norm_w[e],
                              ml_conv_w[e], ml_conv_b[e], ml_w_q[e], ml_w_k[e], ml_w_v[e], ml_w_if[e], ml_b_if[e],
                              ml_norm_w[e], ml_skip[e], hy_w_out[e])
        else:
            o = layer // 2
            y = _s5_mixer(u, s5_a_re[o], s5_a_im[o], s5_log_step[o], s5_b_re[o], s5_b_im[o], s5_c_re[o], s5_c_im[o],
                          s5_d[o], s5_w_a[o], s5_b_a[o], s5_w_b[o], s5_b_b[o])
        x = x + y.astype(x.dtype)
        h = _rmsnorm(x, ffn2_norm[layer])
        x = x + FFN_RES * _swiglu(h, ffn2_w_gate[layer], ffn2_w_up[layer], ffn2_w_down[layer])
    return _rmsnorm(x, final_norm)
```

```python
import functools
import math

import jax
import jax.numpy as jnp
from jax import lax
from jax.experimental import pallas as pl
from jax.experimental.pallas import tpu as pltpu

F32 = jnp.float32
BF16 = jnp.bfloat16
HIGHEST = lax.Precision.HIGHEST

D_MODEL = 1024
EPS = 1e-6
D_FF = 2816
FFN_RES = 0.5
CONV_W = 4
CONV_HALO = 8

SSD_HEADS = 16
SSD_HEAD_DIM = 64
SSD_GROUPS = 2
SSD_D_STATE = 128
SSD_D_INNER = SSD_HEADS * SSD_HEAD_DIM
SSD_BC = SSD_GROUPS * SSD_D_STATE
SSD_CONV_CH = SSD_D_INNER + 2 * SSD_BC
SSD_GROUP_W = SSD_D_INNER // SSD_GROUPS

ML_HEADS = 4
ML_HEAD_DIM = 256
ML_D_INNER = ML_HEADS * ML_HEAD_DIM
ML_QKV_BLOCK = 4

CHUNK = 128
LANES = 128

S5_GROUP = 16
S5_GROUPS = D_MODEL // S5_GROUP
S5_STATE = 64
S5_SLAB_GROUPS = 16
S5_SLABS = S5_GROUPS // S5_SLAB_GROUPS
S5_SLAB_IN = S5_SLAB_GROUPS * S5_GROUP
S5_SLAB_ST = S5_SLAB_GROUPS * S5_STATE
S5_SCAN_W = 512

VMEM_LIMIT_BYTES = 56 * 1024 * 1024


def _cparams(sem):
    return pltpu.CompilerParams(dimension_semantics=sem, vmem_limit_bytes=VMEM_LIMIT_BYTES)


def _dot(a, b):
    return jnp.dot(a, b, preferred_element_type=F32)


def _dot_f32(a, b):
    return jnp.dot(a, b, preferred_element_type=F32, precision=HIGHEST)


def _dot_nt(a, b):
    return lax.dot_general(a, b, (((1,), (1,)), ((), ())), preferred_element_type=F32)


def _rms(x, w):
    return x * lax.rsqrt(jnp.mean(x * x, axis=-1, keepdims=True) + EPS) * w


def _sigmoid(x):
    return 1.0 / (1.0 + jnp.exp(-x))


def _silu(x):
    return x * _sigmoid(x)


def _softplus(x):
    return jnp.maximum(x, 0.0) + jnp.log1p(jnp.exp(-jnp.abs(x)))


def _rep(x, n):
    return jnp.concatenate([x] * n, axis=-1)


def _tri(n, lower):
    r = lax.broadcasted_iota(jnp.int32, (n, n), 0)
    c = lax.broadcasted_iota(jnp.int32, (n, n), 1)
    return r >= c if lower else r <= c


def _const_spec(shape):
    nd = len(shape)
    return pl.BlockSpec(shape, lambda *_: (0,) * nd, pipeline_mode=pl.Buffered(1))


def _ffn_kernel(*refs, has_pre, has_final):
    it = iter(refs)
    x_ref = next(it)
    if has_pre:
        ys_ref, ym_ref, wos_ref, wom_ref = next(it), next(it), next(it), next(it)
    nw_ref, wg_ref, wu_ref, wd_ref = next(it), next(it), next(it), next(it)
    if has_final:
        fw_ref = next(it)
    o_ref = next(it)

    x = x_ref[...]
    if has_pre:
        x = x + _dot(ys_ref[...].astype(BF16), wos_ref[...]) + _dot(ym_ref[...].astype(BF16), wom_ref[...])
    h = _rms(x, nw_ref[...]).astype(BF16)
    g = _dot(h, wg_ref[...])
    u = _dot(h, wu_ref[...])
    a = (_silu(g) * u).astype(BF16)
    x = x + FFN_RES * _dot(a, wd_ref[...])
    if has_final:
        x = _rms(x, fw_ref[...])
    o_ref[...] = x


def _row_spec(layout, tm, n_inner, width=D_MODEL):
    if layout == "bs":
        return pl.BlockSpec((tm, width), lambda b, i: (b * n_inner + i, 0))
    return pl.BlockSpec((tm, width), lambda b, i: (i, b))


def _ffn(x, nw, wg, wu, wd, *, batch, seq, in_layout, out_layout, tm=256, pre=None, final_w=None):
    n_inner = seq // tm
    args = [x]
    specs = [_row_spec(in_layout, tm, n_inner)]
    if pre is not None:
        ys, ym, wos, wom = pre
        args += [ys, ym, wos, wom]
        specs += [_row_spec("bs", tm, n_inner), _row_spec("bs", tm, n_inner),
                  _const_spec(wos.shape), _const_spec(wom.shape)]
    args += [nw, wg, wu, wd]
    specs += [_const_spec(nw.shape), _const_spec(wg.shape), _const_spec(wu.shape), _const_spec(wd.shape)]
    if final_w is not None:
        args.append(final_w)
        specs.append(_const_spec(final_w.shape))
    out_shape = (batch * seq, D_MODEL) if out_layout == "bs" else (seq, batch * D_MODEL)
    return pl.pallas_call(
        functools.partial(_ffn_kernel, has_pre=pre is not None, has_final=final_w is not None),
        grid=(batch, n_inner),
        in_specs=specs,
        out_specs=_row_spec(out_layout, tm, n_inner),
        out_shape=jax.ShapeDtypeStruct(out_shape, F32),
        compiler_params=_cparams(("parallel", "parallel")),
        name="ffn",
    )(*args)


def _inproj_kernel(x_ref, nw_ref, wm_ref, wdt_ref, wdtT_ref, z_ref, xbc_ref, mx_ref, mz_ref, dt_ref, dtT_ref):
    u = _rms(x_ref[...], nw_ref[...]).astype(BF16)
    proj = _dot(u, wm_ref[...])
    o1 = SSD_D_INNER
    o2 = o1 + SSD_CONV_CH
    o3 = o2 + ML_D_INNER
    z_ref[...] = proj[:, :o1]
    xbc_ref[...] = proj[:, o1:o2]
    mx_ref[...] = proj[:, o2:o3]
    mz_ref[...] = proj[:, o3:]
    dt_ref[...] = _dot(u, wdt_ref[...])
    dtT_ref[...] = _dot_nt(wdtT_ref[...], u)


def _inproj(x, nw, w_main, w_dt, w_dtT, *, tm=512):
    rows = x.shape[0]
    n = rows // tm
    row = lambda w: pl.BlockSpec((tm, w), lambda i: (i, 0))
    return pl.pallas_call(
        _inproj_kernel,
        grid=(n,),
        in_specs=[row(D_MODEL), _const_spec(nw.shape), _const_spec(w_main.shape),
                  _const_spec(w_dt.shape), _const_spec(w_dtT.shape)],
        out_specs=[row(SSD_D_INNER), row(SSD_CONV_CH), row(ML_D_INNER), row(ML_D_INNER), row(LANES),
                   pl.BlockSpec((SSD_HEADS, tm), lambda i: (0, i))],
        out_shape=[jax.ShapeDtypeStruct((rows, SSD_D_INNER), F32),
                   jax.ShapeDtypeStruct((rows, SSD_CONV_CH), F32),
                   jax.ShapeDtypeStruct((rows, ML_D_INNER), F32),
                   jax.ShapeDtypeStruct((rows, ML_D_INNER), F32),
                   jax.ShapeDtypeStruct((rows, LANES), F32),
                   jax.ShapeDtypeStruct((SSD_HEADS, rows), F32)],
        compiler_params=_cparams(("parallel",)),
        name="inproj",
    )(x, nw, w_main, w_dt, w_dtT)


def _causal_conv_silu(cbuf, x_ref, w_ref, b_ref, first):
    L = x_ref.shape[0]

    @pl.when(first)
    def _():
        cbuf[0:CONV_HALO, :] = jnp.zeros((CONV_HALO, cbuf.shape[1]), F32)

    cbuf[CONV_HALO:CONV_HALO + L, :] = x_ref[...]
    acc = b_ref[...]
    for j in range(CONV_W):
        start = CONV_HALO - (CONV_W - 1) + j
        acc = acc + cbuf[start:start + L, :] * w_ref[j:j + 1, :]
    cbuf[0:CONV_HALO, :] = cbuf[L:L + CONV_HALO, :]
    return _silu(acc)


def _ssd_kernel(xbc_ref, dt_ref, dtT_ref, z_ref, cw_ref, cb_ref, dtb_ref, dtbT_ref, alog_ref, alogT_ref,
                dexp_ref, nw_ref, o_ref, cbuf, st_ref):
    L = CHUNK
    first = pl.program_id(1) == 0

    @pl.when(first)
    def _():
        st_ref[...] = jnp.zeros(st_ref.shape, F32)

    xbc = _causal_conv_silu(cbuf, xbc_ref, cw_ref, cb_ref, first)
    xs = xbc[:, :SSD_D_INNER]
    bm = xbc[:, SSD_D_INNER:SSD_D_INNER + SSD_BC]
    cm = xbc[:, SSD_D_INNER + SSD_BC:]

    dt = _softplus(dt_ref[...] + dtb_ref[...])
    adt = dt * (-jnp.exp(alog_ref[...]))
    adtT = _softplus(dtT_ref[...] + dtbT_ref[...]) * (-jnp.exp(alogT_ref[...]))
    tril = _tri(L, True)
    a_cs = _dot_f32(tril.astype(F32), adt)
    a_csT = _dot_f32(adtT, _tri(L, False).astype(F32))

    r64 = lax.broadcasted_iota(jnp.int32, (LANES, SSD_D_INNER), 0)
    c64 = lax.broadcasted_iota(jnp.int32, (LANES, SSD_D_INNER), 1)
    e64 = (c64 // SSD_HEAD_DIM == r64).astype(F32)
    r128 = lax.broadcasted_iota(jnp.int32, (LANES, SSD_HEADS * LANES), 0)
    c128 = lax.broadcasted_iota(jnp.int32, (LANES, SSD_HEADS * LANES), 1)
    e128 = (c128 // LANES == r128).astype(F32)
    dt_e = _dot_f32(dt, e64)
    acs_e = _dot_f32(a_cs, e64)
    acs_c = _dot_f32(a_cs, e128)

    xd = xs * dt_e
    alast_e = acs_e[L - 1:L, :]
    xd_dec = xd * jnp.exp(alast_e - acs_e)
    lane = lax.broadcasted_iota(jnp.int32, (L, LANES), 1)
    left = lane < SSD_HEAD_DIM

    y_parts = []
    for g in range(SSD_GROUPS):
        gs = slice(g * SSD_GROUP_W, (g + 1) * SSD_GROUP_W)
        bm_g = bm[:, g * SSD_D_STATE:(g + 1) * SSD_D_STATE]
        cm_g = cm[:, g * SSD_D_STATE:(g + 1) * SSD_D_STATE].astype(BF16)
        cb = _dot_nt(cm_g, bm_g.astype(BF16))
        st_prev = st_ref[:, gs]
        y_off = _dot(cm_g, st_prev.astype(BF16))
        heads_per_group = SSD_HEADS // SSD_GROUPS
        diag = []
        for hp in range(heads_per_group // 2):
            h0 = g * heads_per_group + 2 * hp
            ms = []
            for h in (h0, h0 + 1):
                seg = acs_c[:, h * LANES:(h + 1) * LANES] - a_csT[h:h + 1, :]
                lm = jnp.exp(jnp.where(tril, seg, -jnp.inf))
                ms.append((cb * lm).astype(BF16))
            slab = xd[:, h0 * SSD_HEAD_DIM:(h0 + 2) * SSD_HEAD_DIM]
            rhs = jnp.concatenate([jnp.where(left, slab, 0.0), jnp.where(left, 0.0, slab)], axis=0)
            diag.append(_dot(jnp.concatenate(ms, axis=1), rhs.astype(BF16)))
        y_parts.append(jnp.concatenate(diag, axis=1) + y_off * jnp.exp(acs_e[:, gs]))
        st_ref[:, gs] = st_prev * jnp.exp(alast_e[:, gs]) + _dot(bm_g.T.astype(BF16), xd_dec[:, gs].astype(BF16))

    y = jnp.concatenate(y_parts, axis=1) + dexp_ref[...] * xs
    yg = y * _silu(z_ref[...])
    outs = []
    for g in range(SSD_GROUPS):
        v = yg[:, g * SSD_GROUP_W:(g + 1) * SSD_GROUP_W]
        outs.append(v * lax.rsqrt(jnp.mean(v * v, axis=-1, keepdims=True) + EPS))
    o_ref[...] = jnp.concatenate(outs, axis=1) * nw_ref[...]


def _ssd(xbc, dt, dtT, z, cw, cb, dtb, dtbT, alog, alogT, dexp, nw, *, batch, seq):
    L = CHUNK
    nc = seq // L
    row = lambda w: pl.BlockSpec((L, w), lambda b, c: (b * nc + c, 0))
    consts = [cw, cb, dtb, dtbT, alog, alogT, dexp, nw]
    return pl.pallas_call(
        _ssd_kernel,
        grid=(batch, nc),
        in_specs=[row(SSD_CONV_CH), row(LANES), pl.BlockSpec((SSD_HEADS, L), lambda b, c: (0, b * nc + c)),
                  row(SSD_D_INNER)] + [_const_spec(a.shape) for a in consts],
        out_specs=row(SSD_D_INNER),
        out_shape=jax.ShapeDtypeStruct((batch * seq, SSD_D_INNER), F32),
        scratch_shapes=[pltpu.VMEM((CONV_HALO + L, SSD_CONV_CH), F32),
                        pltpu.VMEM((SSD_D_STATE, SSD_D_INNER), F32)],
        compiler_params=_cparams(("parallel", "arbitrary")),
        name="ssd",
    )(xbc, dt, dtT, z, *consts)


def _mlstm_kernel(mx_ref, mz_ref, cw_ref, cb_ref, wq_ref, wk_ref, wv_ref, wif_ref, wifT_ref, bif_ref, bifT_ref,
                  nw_ref, skip_ref, o_ref, cbuf, ct_ref, m_ref):
    L = CHUNK
    first = pl.program_id(1) == 0

    @pl.when(first)
    def _():
        ct_ref[...] = jnp.zeros(ct_ref.shape, F32)
        m_ref[...] = jnp.full(m_ref.shape, -1e30, F32)

    mx = mx_ref[...]
    xc = _causal_conv_silu(cbuf, mx_ref, cw_ref, cb_ref, first)
    xc_b = xc.astype(BF16)
    mx_b = mx.astype(BF16)
    tile = 2 * LANES
    q = jnp.concatenate([_dot(xc_b[:, t * tile:(t + 1) * tile], wq_ref[t]) for t in range(ML_HEADS)], axis=1)
    k = jnp.concatenate([_dot(xc_b[:, t * tile:(t + 1) * tile], wk_ref[t]) for t in range(ML_HEADS)], axis=1)
    v = jnp.concatenate([_dot(mx_b[:, t * tile:(t + 1) * tile], wv_ref[t]) for t in range(ML_HEADS)], axis=1)
    qkv = jnp.concatenate([q, k, v], axis=1).astype(BF16)
    gates = _dot(qkv, wif_ref[...]) + bif_ref[...]
    gatesT = _dot_nt(wifT_ref[...], qkv) + bifT_ref[...]

    tril = _tri(L, True)
    logf = -_softplus(-gates)
    logfT = -_softplus(-gatesT)
    bcum = _dot_f32(tril.astype(F32), logf)
    bcumT = _dot_f32(logfT, _tri(L, False).astype(F32))
    r = lax.broadcasted_iota(jnp.int32, (LANES, ML_HEADS * LANES), 0)
    c = lax.broadcasted_iota(jnp.int32, (LANES, ML_HEADS * LANES), 1)
    bc_all = _dot_f32(bcum, (c // LANES + ML_HEADS == r).astype(F32))
    ig_all = _dot_f32(gates, (c // LANES == r).astype(F32))

    ones = jnp.ones((L, LANES), BF16)
    outs = []
    for h in range(ML_HEADS):
        hs = slice(h * ML_HEAD_DIM, (h + 1) * ML_HEAD_DIM)
        qh, kh, vh = q[:, hs], k[:, hs], v[:, hs]
        v_aug = jnp.concatenate([vh.astype(BF16), ones], axis=1)
        bc = bc_all[:, h * LANES:(h + 1) * LANES]
        ig = ig_all[:, h * LANES:(h + 1) * LANES]
        m_prev = m_ref[h, 0:1, :]
        ct_prev = ct_ref[h]

        dlog = jnp.where(tril, bc - bcumT[ML_HEADS + h:ML_HEADS + h + 1, :] + gatesT[h:h + 1, :], -jnp.inf)
        m_inter = bc + m_prev
        m_t = jnp.maximum(jnp.max(dlog, axis=-1, keepdims=True), m_inter)
        scores = _dot_nt(qh.astype(BF16), kh.astype(BF16)) * jnp.exp(dlog - m_t)
        inter_w = jnp.exp(m_inter - m_t)
        num = _dot(scores.astype(BF16), v_aug) + _rep(inter_w, 3) * _dot(qh.astype(BF16), ct_prev.astype(BF16))
        den = jnp.maximum(jnp.abs(num[:, ML_HEAD_DIM:]), jnp.exp(-m_t))
        hout = num[:, :ML_HEAD_DIM] / _rep(den, 2)

        b_last = bc[L - 1:L, :]
        w_state = b_last - bc + ig
        m_new = jnp.maximum(b_last + m_prev, jnp.max(w_state, axis=0, keepdims=True))
        decay = jnp.exp(b_last + m_prev - m_new)
        kw = (kh * _rep(jnp.exp(w_state - m_new), 2)).T.astype(BF16)
        ct_ref[h] = _rep(decay, 3) * ct_prev + _dot(kw, v_aug)
        m_ref[h] = jnp.broadcast_to(m_new, m_ref.shape[1:])

        mu = jnp.mean(hout, axis=-1, keepdims=True)
        d = hout - mu
        var = jnp.mean(d * d, axis=-1, keepdims=True)
        outs.append(d * lax.rsqrt(var + EPS))
    hm = jnp.concatenate(outs, axis=1) * nw_ref[...]
    o_ref[...] = (hm + skip_ref[...] * xc) * _silu(mz_ref[...])


def _mlstm(mx, mz, cw, cb, wq, wk, wv, wif, wifT, bif, bifT, nw, skip, *, batch, seq):
    L = CHUNK
    nc = seq // L
    row = lambda w: pl.BlockSpec((L, w), lambda b, c: (b * nc + c, 0))
    consts = [cw, cb, wq, wk, wv, wif, wifT, bif, bifT, nw, skip]
    return pl.pallas_call(
        _mlstm_kernel,
        grid=(batch, nc),
        in_specs=[row(ML_D_INNER), row(ML_D_INNER)] + [_const_spec(a.shape) for a in consts],
        out_specs=row(ML_D_INNER),
        out_shape=jax.ShapeDtypeStruct((batch * seq, ML_D_INNER), F32),
        scratch_shapes=[pltpu.VMEM((CONV_HALO + L, ML_D_INNER), F32),
                        pltpu.VMEM((ML_HEADS, ML_HEAD_DIM, ML_HEAD_DIM + LANES), F32),
                        pltpu.VMEM((ML_HEADS, 8, LANES), F32)],
        compiler_params=_cparams(("parallel", "arbitrary")),
        name="mlstm",
    )(mx, mz, *consts)


def _s5_param_kernel(are_ref, aim_ref, lstep_ref, bre_ref, bim_ref, lre_ref, lim_ref, bbre_ref, bbim_ref):
    a_re, a_im = are_ref[...], aim_ref[...]
    step = jnp.exp(lstep_ref[...])
    mag = jnp.exp(a_re * step)
    lam_re = mag * jnp.cos(a_im * step)
    lam_im = mag * jnp.sin(a_im * step)
    den = a_re * a_re + a_im * a_im
    coef_re = ((lam_re - 1.0) * a_re + lam_im * a_im) / den
    coef_im = (lam_im * a_re - (lam_re - 1.0) * a_im) / den
    b_re, b_im = bre_ref[...], bim_ref[...]
    lre_ref[...] = lam_re
    lim_ref[...] = lam_im
    bbre_ref[...] = coef_re * b_re - coef_im * b_im
    bbim_ref[...] = coef_re * b_im + coef_im * b_re


def _s5_params(a_re, a_im, log_step, b_re, b_im):
    rep = lambda t: jnp.repeat(t, S5_GROUP, axis=0)
    are, aim = rep(a_re), rep(a_im)
    lstep = jnp.broadcast_to(rep(log_step[:, None]), are.shape)
    breT = jnp.swapaxes(b_re, 1, 2).reshape(D_MODEL, S5_STATE)
    bimT = jnp.swapaxes(b_im, 1, 2).reshape(D_MODEL, S5_STATE)
    shp = jax.ShapeDtypeStruct((D_MODEL, S5_STATE), F32)
    return pl.pallas_call(_s5_param_kernel, out_shape=[shp, shp, shp, shp], name="s5_params")(
        are, aim, lstep, breT, bimT)


def _s5_kernel(x_ref, nw_ref, bs_ref, lam_ref, cs_ref, dsk_ref, wa_ref, ba_ref, wb_ref, bb_ref, o_ref,
               bu_ref, st_ref, *, batch):
    rows = x_ref.shape[0]
    steps = rows // batch

    @pl.when(pl.program_id(0) == 0)
    def _():
        st_ref[...] = jnp.zeros(st_ref.shape, F32)

    x = x_ref[...]
    u = _rms(x, nw_ref[...])
    u_b = u.astype(BF16)
    ys = []
    for k in range(S5_SLABS):
        bu_ref[...] = _dot(u_b[:, k * S5_SLAB_IN:(k + 1) * S5_SLAB_IN], bs_ref[k])
        for w in range(S5_SLAB_ST // S5_SCAN_W):
            re_l = slice(w * S5_SCAN_W, (w + 1) * S5_SCAN_W)
            im_l = slice(S5_SLAB_ST + w * S5_SCAN_W, S5_SLAB_ST + (w + 1) * S5_SCAN_W)
            lr = jnp.broadcast_to(lam_ref[k, 0:1, re_l], (batch, S5_SCAN_W))
            li = jnp.broadcast_to(lam_ref[k, 0:1, im_l], (batch, S5_SCAN_W))

            def step(t, carry, re_l=re_l, im_l=im_l, lr=lr, li=li):
                xr, xi = carry
                r0 = pl.multiple_of(t * batch, batch)
                br = bu_ref[pl.ds(r0, batch), re_l]
                bi = bu_ref[pl.ds(r0, batch), im_l]
                nr = lr * xr - li * xi + br
                ni = lr * xi + li * xr + bi
                bu_ref[pl.ds(r0, batch), re_l] = nr
                bu_ref[pl.ds(r0, batch), im_l] = ni
                return nr, ni

            xr, xi = lax.fori_loop(0, steps, step, (st_ref[k, :, re_l], st_ref[k, :, im_l]), unroll=4)
            st_ref[k, :, re_l] = xr
            st_ref[k, :, im_l] = xi
        ys.append(_dot(bu_ref[...].astype(BF16), cs_ref[k]))
    y = jnp.concatenate(ys, axis=1) + dsk_ref[...] * u
    g = jax.nn.gelu(y).astype(BF16)
    out = (_dot(g, wa_ref[...]) + ba_ref[...]) * _sigmoid(_dot(g, wb_ref[...]) + bb_ref[...])
    o_ref[...] = x + out


def _s5(x_sb, nw, b_slabs, lam_slabs, c_slabs, dsk, wa, ba, wb, bb, *, batch, seq, ts=32):
    rows = ts * batch
    consts = [nw, b_slabs, lam_slabs, c_slabs, dsk, wa, ba, wb, bb]
    return pl.pallas_call(
        functools.partial(_s5_kernel, batch=batch),
        grid=(seq // ts,),
        in_specs=[pl.BlockSpec((rows, D_MODEL), lambda i: (i, 0))] + [_const_spec(a.shape) for a in consts],
        out_specs=pl.BlockSpec((rows, D_MODEL), lambda i: (i, 0)),
        out_shape=jax.ShapeDtypeStruct((seq * batch, D_MODEL), F32),
        scratch_shapes=[pltpu.VMEM((rows, 2 * S5_SLAB_ST), F32),
                        pltpu.VMEM((S5_SLABS, batch, 2 * S5_SLAB_ST), F32)],
        compiler_params=_cparams(("arbitrary",)),
        name="s5",
    )(x_sb, *consts)


def _s5_slab_weights(lam_re, lam_im, bb_re, bb_im, c_re, c_im):
    eye = jnp.eye(S5_SLAB_GROUPS, dtype=F32)

    def in_slab(bb):
        t = bb.reshape(S5_SLABS, S5_SLAB_GROUPS, S5_GROUP, S5_STATE)
        t = t[:, :, :, None, :] * eye[None, :, None, :, None]
        return t.reshape(S5_SLABS, S5_SLAB_IN, S5_SLAB_ST)

    def out_slab(cc):
        t = jnp.swapaxes(cc, 1, 2).reshape(S5_SLABS, S5_SLAB_GROUPS, S5_STATE, S5_GROUP)
        t = t[:, :, :, None, :] * eye[None, :, None, :, None]
        return t.reshape(S5_SLABS, S5_SLAB_ST, S5_SLAB_IN)

    b_slabs = jnp.concatenate([in_slab(bb_re), in_slab(bb_im)], axis=2).astype(BF16)
    c_slabs = jnp.concatenate([out_slab(c_re), out_slab(-c_im)], axis=1).astype(BF16)
    lam = lambda t: t[::S5_GROUP].reshape(S5_SLABS, 1, S5_SLAB_ST)
    lam_slabs = jnp.concatenate([lam(lam_re), lam(lam_im)], axis=2)
    return b_slabs, jnp.broadcast_to(lam_slabs, (S5_SLABS, 8, 2 * S5_SLAB_ST)), c_slabs


def _blockdiag_tiles(w):
    nb = w.shape[0]
    per = nb // ML_HEADS
    wt = jnp.swapaxes(w, 1, 2).reshape(ML_HEADS, per, ML_QKV_BLOCK, ML_QKV_BLOCK)
    eye = jnp.eye(per, dtype=w.dtype)
    t = wt[:, :, :, None, :] * eye[None, :, None, :, None]
    return t.reshape(ML_HEADS, per * ML_QKV_BLOCK, per * ML_QKV_BLOCK)


def _pad_lanes(t, n=LANES):
    return jnp.pad(t, ((0, 0), (0, n - t.shape[1])))


def kernel(x, ffn1_norm, ffn1_w_gate, ffn1_w_up, ffn1_w_down, mix_norm, ffn2_norm, ffn2_w_gate, ffn2_w_up, ffn2_w_down, hy_w_in, ssd_conv_w, ssd_conv_b, ssd_dt_bias, ssd_a_log, ssd_d, ssd_norm_w, ml_conv_w, ml_conv_b, ml_w_q, ml_w_k, ml_w_v, ml_w_if, ml_b_if, ml_norm_w, ml_skip, hy_w_out, s5_a_re, s5_a_im, s5_log_step, s5_b_re, s5_b_im, s5_c_re, s5_c_im, s5_d, s5_w_a, s5_b_a, s5_w_b, s5_b_b, final_norm):
    batch, seq, _ = x.shape
    assert seq % 512 == 0 and batch % 8 == 0
    row = lambda t: t.reshape(1, -1)
    bf = lambda t: t.astype(BF16)
    xf = x.reshape(batch * seq, D_MODEL)

    x1 = _ffn(xf, row(ffn1_norm[0]), bf(ffn1_w_gate[0]), bf(ffn1_w_up[0]), bf(ffn1_w_down[0]),
              batch=batch, seq=seq, in_layout="bs", out_layout="bs")

    w_in = hy_w_in[0]
    o1 = SSD_D_INNER
    o2 = o1 + SSD_CONV_CH
    o3 = o2 + SSD_HEADS
    w_main = bf(jnp.concatenate([w_in[:, :o2], w_in[:, o3:]], axis=1))
    w_dt = w_in[:, o2:o3]
    z_s, xbc, m_x, m_z, dt_raw, dt_rawT = _inproj(x1, row(mix_norm[0]), w_main, bf(_pad_lanes(w_dt)),
                                                   bf(w_dt.T))

    y_ssd = _ssd(xbc, dt_raw, dt_rawT, z_s, ssd_conv_w[0], row(ssd_conv_b[0]),
                 _pad_lanes(row(ssd_dt_bias[0])), ssd_dt_bias[0].reshape(-1, 1),
                 _pad_lanes(row(ssd_a_log[0])), ssd_a_log[0].reshape(-1, 1),
                 row(jnp.repeat(ssd_d[0], SSD_HEAD_DIM)), row(ssd_norm_w[0]), batch=batch, seq=seq)

    k_scale = 1.0 / math.sqrt(ML_HEAD_DIM)
    w_if = ml_w_if[0]
    w_if = jnp.concatenate([w_if[:ML_D_INNER], w_if[ML_D_INNER:2 * ML_D_INNER] / k_scale, w_if[2 * ML_D_INNER:]],
                           axis=0)
    b_if = ml_b_if[0]
    y_ml = _mlstm(m_x, m_z, ml_conv_w[0], row(ml_conv_b[0]),
                  bf(_blockdiag_tiles(ml_w_q[0])), bf(_blockdiag_tiles(ml_w_k[0]) * k_scale),
                  bf(_blockdiag_tiles(ml_w_v[0])), bf(_pad_lanes(w_if)), bf(w_if.T),
                  _pad_lanes(row(b_if)), b_if.reshape(-1, 1), row(ml_norm_w[0]), row(ml_skip[0]),
                  batch=batch, seq=seq)

    w_out = bf(hy_w_out[0])
    x3 = _ffn(x1, row(ffn2_norm[0]), bf(ffn2_w_gate[0]), bf(ffn2_w_up[0]), bf(ffn2_w_down[0]),
              batch=batch, seq=seq, in_layout="bs", out_layout="bs",
              pre=(y_ssd, y_ml, w_out[:SSD_D_INNER], w_out[SSD_D_INNER:]))

    x4 = _ffn(x3, row(ffn1_norm[1]), bf(ffn1_w_gate[1]), bf(ffn1_w_up[1]), bf(ffn1_w_down[1]),
              batch=batch, seq=seq, in_layout="bs", out_layout="sb")
    lam_re, lam_im, bb_re, bb_im = _s5_params(s5_a_re[0], s5_a_im[0], s5_log_step[0], s5_b_re[0], s5_b_im[0])
    b_slabs, lam_slabs, c_slabs = _s5_slab_weights(lam_re, lam_im, bb_re, bb_im, s5_c_re[0], s5_c_im[0])
    x5 = _s5(x4.reshape(seq * batch, D_MODEL), row(mix_norm[1]), b_slabs, lam_slabs, c_slabs, row(s5_d[0]),
             bf(s5_w_a[0]), row(s5_b_a[0]), bf(s5_w_b[0]), row(s5_b_b[0]), batch=batch, seq=seq)
    out = _ffn(x5.reshape(seq, batch * D_MODEL), row(ffn2_norm[1]), bf(ffn2_w_gate[1]), bf(ffn2_w_up[1]),
               bf(ffn2_w_down[1]), batch=batch, seq=seq, in_layout="sb", out_layout="bs",
               final_w=row(final_norm))
    return out.reshape(batch, seq, D_MODEL)
```

```python
import functools
import math

import jax
import jax.numpy as jnp
from jax import lax
from jax.experimental import pallas as pl
from jax.experimental.pallas import tpu as pltpu

F32 = jnp.float32
BF16 = jnp.bfloat16
HIGHEST = lax.Precision.HIGHEST

D_MODEL = 1024
EPS = 1e-6
D_FF = 2816
FFN_RES = 0.5
CONV_W = 4
CONV_HALO = 8

SSD_HEADS = 16
SSD_HEAD_DIM = 64
SSD_GROUPS = 2
SSD_D_STATE = 128
SSD_D_INNER = SSD_HEADS * SSD_HEAD_DIM
SSD_BC = SSD_GROUPS * SSD_D_STATE
SSD_CONV_CH = SSD_D_INNER + 2 * SSD_BC
SSD_GROUP_W = SSD_D_INNER // SSD_GROUPS

ML_HEADS = 4
ML_HEAD_DIM = 256
ML_D_INNER = ML_HEADS * ML_HEAD_DIM
ML_QKV_BLOCK = 4

CHUNK = 128
LANES = 128
SUBLANES = 8

S5_GROUP = 16
S5_GROUPS = D_MODEL // S5_GROUP
S5_STATE = 64
S5_SLAB_GROUPS = 16
S5_SLABS = S5_GROUPS // S5_SLAB_GROUPS
S5_SLAB_IN = S5_SLAB_GROUPS * S5_GROUP
S5_SLAB_ST = S5_SLAB_GROUPS * S5_STATE
S5_SCAN_W = 512

VMEM_LIMIT_BYTES = 56 * 1024 * 1024


def _cparams(sem):
    return pltpu.CompilerParams(dimension_semantics=sem, vmem_limit_bytes=VMEM_LIMIT_BYTES)


def _dot(a, b):
    return jnp.dot(a, b, preferred_element_type=F32)


def _dot_f32(a, b):
    return jnp.dot(a, b, preferred_element_type=F32, precision=HIGHEST)


def _dot_nt(a, b):
    return lax.dot_general(a, b, (((1,), (1,)), ((), ())), preferred_element_type=F32)


def _rms(x, w):
    return x * lax.rsqrt(jnp.mean(x * x, axis=-1, keepdims=True) + EPS) * w


def _sigmoid(x):
    return 1.0 / (1.0 + jnp.exp(-x))


def _silu(x):
    return x * _sigmoid(x)


def _softplus(x):
    return jnp.maximum(x, 0.0) + jnp.log1p(jnp.exp(-jnp.abs(x)))


def _rep(x, n):
    return jnp.concatenate([x] * n, axis=-1)


def _tri(n, lower):
    r = lax.broadcasted_iota(jnp.int32, (n, n), 0)
    c = lax.broadcasted_iota(jnp.int32, (n, n), 1)
    return r >= c if lower else r <= c


def _const_spec(shape):
    nd = len(shape)
    return pl.BlockSpec(shape, lambda *_: (0,) * nd, pipeline_mode=pl.Buffered(1))


def _ffn_kernel(*refs, has_pre, has_final, in_layout, out_layout):
    it = iter(refs)
    x_ref = next(it)
    if has_pre:
        ys_ref, ym_ref, wos_ref, wom_ref = next(it), next(it), next(it), next(it)
    nw_ref, wg_ref, wu_ref, wd_ref = next(it), next(it), next(it), next(it)
    if has_final:
        fw_ref = next(it)
    o_ref = next(it)
    scr = next(it, None)

    if in_layout == "tb":
        ts = x_ref.shape[0]
        for t in range(ts):
            for j in range(scr.shape[0]):
                scr[j, pl.ds(t, SUBLANES, stride=ts), :] = x_ref[t, :, j * LANES:(j + 1) * LANES]
        x = jnp.concatenate([scr[j] for j in range(scr.shape[0])], axis=1)
    elif in_layout == "bt":
        x = x_ref[...].reshape(-1, D_MODEL)
    else:
        x = x_ref[...]
    if has_pre:
        x = x + _dot(ys_ref[...].astype(BF16), wos_ref[...]) + _dot(ym_ref[...].astype(BF16), wom_ref[...])
    h = _rms(x, nw_ref[...]).astype(BF16)
    g = _dot(h, wg_ref[...])
    u = _dot(h, wu_ref[...])
    a = (_silu(g) * u).astype(BF16)
    x = x + FFN_RES * _dot(a, wd_ref[...])
    if has_final:
        x = _rms(x, fw_ref[...])
    if out_layout == "tb":
        ts = o_ref.shape[0]
        for j in range(scr.shape[0]):
            scr[j] = x[:, j * LANES:(j + 1) * LANES]
        for t in range(ts):
            for j in range(scr.shape[0]):
                o_ref[t, :, j * LANES:(j + 1) * LANES] = scr[j, pl.ds(t, SUBLANES, stride=ts), :]
    elif out_layout == "bt":
        o_ref[...] = x.reshape(o_ref.shape)
    else:
        o_ref[...] = x


def _row_spec(layout, tm, n_inner):
    if layout == "bs":
        return pl.BlockSpec((tm, D_MODEL), lambda b, i: (b * n_inner + i, 0))
    ts = tm // SUBLANES
    if layout == "bt":
        return pl.BlockSpec((None, SUBLANES, ts, D_MODEL), lambda b, i: (b, 0, i, 0))
    return pl.BlockSpec((ts, None, SUBLANES, D_MODEL), lambda b, i: (i, b, 0, 0))


def _ffn(x, nw, wg, wu, wd, *, batch, seq, in_layout="bs", out_layout="bs", tm=512, pre=None, final_w=None):
    if in_layout == "bs":
        grid = (batch, seq // tm)
        n_inner = seq // tm
    else:
        grid = (batch // SUBLANES, seq * SUBLANES // tm)
        n_inner = None
    args = [x]
    specs = [_row_spec(in_layout, tm, n_inner)]
    if pre is not None:
        ys, ym, wos, wom = pre
        args += [ys, ym, wos, wom]
        specs += [_row_spec("bs", tm, n_inner), _row_spec("bs", tm, n_inner),
                  _const_spec(wos.shape), _const_spec(wom.shape)]
    args += [nw, wg, wu, wd]
    specs += [_const_spec(nw.shape), _const_spec(wg.shape), _const_spec(wu.shape), _const_spec(wd.shape)]
    if final_w is not None:
        args.append(final_w)
        specs.append(_const_spec(final_w.shape))
    out_shape = {"bs": (batch * seq, D_MODEL),
                 "bt": (batch // SUBLANES, SUBLANES, seq, D_MODEL),
                 "tb": (seq, batch // SUBLANES, SUBLANES, D_MODEL)}[out_layout]
    scratch = [pltpu.VMEM((D_MODEL // LANES, tm, LANES), F32)] if "tb" in (in_layout, out_layout) else []
    return pl.pallas_call(
        functools.partial(_ffn_kernel, has_pre=pre is not None, has_final=final_w is not None,
                          in_layout=in_layout, out_layout=out_layout),
        grid=grid,
        in_specs=specs,
        out_specs=_row_spec(out_layout, tm, n_inner),
        out_shape=jax.ShapeDtypeStruct(out_shape, F32),
        scratch_shapes=scratch,
        compiler_params=_cparams(("parallel", "parallel")),
        name="ffn",
    )(*args)


def _inproj_kernel(x_ref, nw_ref, wm_ref, wdt_ref, wdtT_ref, z_ref, xbc_ref, mx_ref, mz_ref, dt_ref, dtT_ref):
    u = _rms(x_ref[...], nw_ref[...]).astype(BF16)
    proj = _dot(u, wm_ref[...])
    o1 = SSD_D_INNER
    o2 = o1 + SSD_CONV_CH
    o3 = o2 + ML_D_INNER
    z_ref[...] = proj[:, :o1]
    xbc_ref[...] = proj[:, o1:o2]
    mx_ref[...] = proj[:, o2:o3]
    mz_ref[...] = proj[:, o3:]
    dt_ref[...] = _dot(u, wdt_ref[...])
    dtT_ref[...] = _dot_nt(wdtT_ref[...], u)


def _inproj(x, nw, w_main, w_dt, w_dtT, *, tm=512):
    rows = x.shape[0]
    n = rows // tm
    row = lambda w: pl.BlockSpec((tm, w), lambda i: (i, 0))
    return pl.pallas_call(
        _inproj_kernel,
        grid=(n,),
        in_specs=[row(D_MODEL), _const_spec(nw.shape), _const_spec(w_main.shape),
                  _const_spec(w_dt.shape), _const_spec(w_dtT.shape)],
        out_specs=[row(SSD_D_INNER), row(SSD_CONV_CH), row(ML_D_INNER), row(ML_D_INNER), row(LANES),
                   pl.BlockSpec((SSD_HEADS, tm), lambda i: (0, i))],
        out_shape=[jax.ShapeDtypeStruct((rows, SSD_D_INNER), F32),
                   jax.ShapeDtypeStruct((rows, SSD_CONV_CH), F32),
                   jax.ShapeDtypeStruct((rows, ML_D_INNER), F32),
                   jax.ShapeDtypeStruct((rows, ML_D_INNER), F32),
                   jax.ShapeDtypeStruct((rows, LANES), F32),
                   jax.ShapeDtypeStruct((SSD_HEADS, rows), F32)],
        compiler_params=_cparams(("parallel",)),
        name="inproj",
    )(x, nw, w_main, w_dt, w_dtT)


def _causal_conv_silu(cbuf, x_ref, w_ref, b_ref, first):
    L = x_ref.shape[0]

    @pl.when(first)
    def _():
        cbuf[0:CONV_HALO, :] = jnp.zeros((CONV_HALO, cbuf.shape[1]), F32)

    cbuf[CONV_HALO:CONV_HALO + L, :] = x_ref[...]
    acc = b_ref[...]
    for j in range(CONV_W):
        start = CONV_HALO - (CONV_W - 1) + j
        acc = acc + cbuf[start:start + L, :] * w_ref[j:j + 1, :]
    cbuf[0:CONV_HALO, :] = cbuf[L:L + CONV_HALO, :]
    return _silu(acc)


def _ssd_kernel(xbc_ref, dt_ref, dtT_ref, z_ref, cw_ref, cb_ref, dtb_ref, dtbT_ref, alog_ref, alogT_ref,
                dexp_ref, nw_ref, o_ref, cbuf, st_ref):
    L = CHUNK
    first = pl.program_id(1) == 0

    @pl.when(first)
    def _():
        st_ref[...] = jnp.zeros(st_ref.shape, F32)

    xbc = _causal_conv_silu(cbuf, xbc_ref, cw_ref, cb_ref, first)
    xs = xbc[:, :SSD_D_INNER]
    bm = xbc[:, SSD_D_INNER:SSD_D_INNER + SSD_BC]
    cm = xbc[:, SSD_D_INNER + SSD_BC:]

    dt = _softplus(dt_ref[...] + dtb_ref[...])
    adt = dt * (-jnp.exp(alog_ref[...]))
    dtT = _softplus(dtT_ref[...] + dtbT_ref[...])
    adtT = dtT * (-jnp.exp(alogT_ref[...]))
    tril = _tri(L, True)
    a_cs = _dot_f32(tril.astype(F32), adt)
    a_csT = _dot_f32(adtT, _tri(L, False).astype(F32))
    w_st = dt * jnp.exp(a_cs[L - 1:L, :] - a_cs)
    left = lax.broadcasted_iota(jnp.int32, (L, LANES), 1) < SSD_HEAD_DIM

    def col(a, h):
        return jnp.broadcast_to(a[:, h:h + 1], (L, LANES))

    y_parts = []
    heads_per_group = SSD_HEADS // SSD_GROUPS
    for g in range(SSD_GROUPS):
        gs = slice(g * SSD_GROUP_W, (g + 1) * SSD_GROUP_W)
        bm_g = bm[:, g * SSD_D_STATE:(g + 1) * SSD_D_STATE]
        cm_g = cm[:, g * SSD_D_STATE:(g + 1) * SSD_D_STATE].astype(BF16)
        cb = _dot_nt(cm_g, bm_g.astype(BF16))
        st_prev = st_ref[:, gs]
        y_off = _dot(cm_g, st_prev.astype(BF16))
        diag, acs_pairs, xdec_pairs = [], [], []
        for hp in range(heads_per_group // 2):
            h0 = g * heads_per_group + 2 * hp
            cols = [col(a_cs, h0), col(a_cs, h0 + 1)]
            ms = []
            for h, acs_col in zip((h0, h0 + 1), cols):
                lm = jnp.exp(jnp.where(tril, acs_col - a_csT[h:h + 1, :], -jnp.inf))
                ms.append((cb * lm * dtT[h:h + 1, :]).astype(BF16))
            slab = xs[:, h0 * SSD_HEAD_DIM:(h0 + 2) * SSD_HEAD_DIM]
            rhs = jnp.concatenate([jnp.where(left, slab, 0.0), jnp.where(left, 0.0, slab)], axis=0)
            diag.append(_dot(jnp.concatenate(ms, axis=1), rhs.astype(BF16)))
            acs_pairs.append(jnp.where(left, cols[0], cols[1]))
            xdec_pairs.append(slab * jnp.where(left, col(w_st, h0), col(w_st, h0 + 1)))
        acs_e = jnp.concatenate(acs_pairs, axis=1)
        y_parts.append(jnp.concatenate(diag, axis=1) + y_off * jnp.exp(acs_e))
        xd_dec = jnp.concatenate(xdec_pairs, axis=1).astype(BF16)
        st_ref[:, gs] = st_prev * jnp.exp(acs_e[L - 1:L, :]) + _dot(bm_g.T.astype(BF16), xd_dec)

    y = jnp.concatenate(y_parts, axis=1) + dexp_ref[...] * xs
    yg = y * _silu(z_ref[...])
    outs = []
    for g in range(SSD_GROUPS):
        v = yg[:, g * SSD_GROUP_W:(g + 1) * SSD_GROUP_W]
        outs.append(v * lax.rsqrt(jnp.mean(v * v, axis=-1, keepdims=True) + EPS))
    o_ref[...] = jnp.concatenate(outs, axis=1) * nw_ref[...]


def _ssd(xbc, dt, dtT, z, cw, cb, dtb, dtbT, alog, alogT, dexp, nw, *, batch, seq):
    L = CHUNK
    nc = seq // L
    row = lambda w: pl.BlockSpec((L, w), lambda b, c: (b * nc + c, 0))
    consts = [cw, cb, dtb, dtbT, alog, alogT, dexp, nw]
    return pl.pallas_call(
        _ssd_kernel,
        grid=(batch, nc),
        in_specs=[row(SSD_CONV_CH), row(LANES), pl.BlockSpec((SSD_HEADS, L), lambda b, c: (0, b * nc + c)),
                  row(SSD_D_INNER)] + [_const_spec(a.shape) for a in consts],
        out_specs=row(SSD_D_INNER),
        out_shape=jax.ShapeDtypeStruct((batch * seq, SSD_D_INNER), F32),
        scratch_shapes=[pltpu.VMEM((CONV_HALO + L, SSD_CONV_CH), F32),
                        pltpu.VMEM((SSD_D_STATE, SSD_D_INNER), F32)],
        compiler_params=_cparams(("parallel", "arbitrary")),
        name="ssd",
    )(xbc, dt, dtT, z, *consts)


def _mlstm_kernel(mx_ref, mz_ref, cw_ref, cb_ref, wq_ref, wk_ref, wv_ref, wif_ref, wifT_ref, bif_ref, bifT_ref,
                  nw_ref, skip_ref, o_ref, cbuf, ct_ref, m_ref):
    L = CHUNK
    first = pl.program_id(1) == 0

    @pl.when(first)
    def _():
        ct_ref[...] = jnp.zeros(ct_ref.shape, F32)
        m_ref[...] = jnp.full(m_ref.shape, -1e30, F32)

    mx = mx_ref[...]
    xc = _causal_conv_silu(cbuf, mx_ref, cw_ref, cb_ref, first)
    xc_b = xc.astype(BF16)
    mx_b = mx.astype(BF16)
    tile = 2 * LANES
    q = jnp.concatenate([_dot(xc_b[:, t * tile:(t + 1) * tile], wq_ref[t]) for t in range(ML_HEADS)], axis=1)
    k = jnp.concatenate([_dot(xc_b[:, t * tile:(t + 1) * tile], wk_ref[t]) for t in range(ML_HEADS)], axis=1)
    v = jnp.concatenate([_dot(mx_b[:, t * tile:(t + 1) * tile], wv_ref[t]) for t in range(ML_HEADS)], axis=1)
    qkv = jnp.concatenate([q, k, v], axis=1).astype(BF16)
    gates = _dot(qkv, wif_ref[...]) + bif_ref[...]
    gatesT = _dot_nt(wifT_ref[...], qkv) + bifT_ref[...]

    tril = _tri(L, True)
    logf = -_softplus(-gates)
    logfT = -_softplus(-gatesT)
    bcum = _dot_f32(tril.astype(F32), logf)
    bcumT = _dot_f32(logfT, _tri(L, False).astype(F32))

    ones = jnp.ones((L, LANES), BF16)
    outs = []
    for h in range(ML_HEADS):
        hs = slice(h * ML_HEAD_DIM, (h + 1) * ML_HEAD_DIM)
        qh, kh, vh = q[:, hs], k[:, hs], v[:, hs]
        v_aug = jnp.concatenate([vh.astype(BF16), ones], axis=1)
        bc = jnp.broadcast_to(bcum[:, ML_HEADS + h:ML_HEADS + h + 1], (L, LANES))
        ig = jnp.broadcast_to(gates[:, h:h + 1], (L, LANES))
        m_prev = m_ref[h, 0:1, :]
        ct_prev = ct_ref[h]

        dlog = jnp.where(tril, bc - bcumT[ML_HEADS + h:ML_HEADS + h + 1, :] + gatesT[h:h + 1, :], -jnp.inf)
        m_inter = bc + m_prev
        m_t = jnp.maximum(jnp.max(dlog, axis=-1, keepdims=True), m_inter)
        scores = _dot_nt(qh.astype(BF16), kh.astype(BF16)) * jnp.exp(dlog - m_t)
        inter_w = jnp.exp(m_inter - m_t)
        num = _dot(scores.astype(BF16), v_aug) + _rep(inter_w, 3) * _dot(qh.astype(BF16), ct_prev.astype(BF16))
        den = jnp.maximum(jnp.abs(num[:, ML_HEAD_DIM:]), jnp.exp(-m_t))
        hout = num[:, :ML_HEAD_DIM] / _rep(den, 2)

        b_last = bc[L - 1:L, :]
        w_state = b_last - bc + ig
        m_new = jnp.maximum(b_last + m_prev, jnp.max(w_state, axis=0, keepdims=True))
        decay = jnp.exp(b_last + m_prev - m_new)
        kw = (kh * _rep(jnp.exp(w_state - m_new), 2)).T.astype(BF16)
        ct_ref[h] = _rep(decay, 3) * ct_prev + _dot(kw, v_aug)
        m_ref[h] = jnp.broadcast_to(m_new, m_ref.shape[1:])

        mu = jnp.mean(hout, axis=-1, keepdims=True)
        d = hout - mu
        var = jnp.mean(d * d, axis=-1, keepdims=True)
        outs.append(d * lax.rsqrt(var + EPS))
    hm = jnp.concatenate(outs, axis=1) * nw_ref[...]
    o_ref[...] = (hm + skip_ref[...] * xc) * _silu(mz_ref[...])


def _mlstm(mx, mz, cw, cb, wq, wk, wv, wif, wifT, bif, bifT, nw, skip, *, batch, seq):
    L = CHUNK
    nc = seq // L
    row = lambda w: pl.BlockSpec((L, w), lambda b, c: (b * nc + c, 0))
    consts = [cw, cb, wq, wk, wv, wif, wifT, bif, bifT, nw, skip]
    return pl.pallas_call(
        _mlstm_kernel,
        grid=(batch, nc),
        in_specs=[row(ML_D_INNER), row(ML_D_INNER)] + [_const_spec(a.shape) for a in consts],
        out_specs=row(ML_D_INNER),
        out_shape=jax.ShapeDtypeStruct((batch * seq, ML_D_INNER), F32),
        scratch_shapes=[pltpu.VMEM((CONV_HALO + L, ML_D_INNER), F32),
                        pltpu.VMEM((ML_HEADS, ML_HEAD_DIM, ML_HEAD_DIM + LANES), F32),
                        pltpu.VMEM((ML_HEADS, 8, LANES), F32)],
        compiler_params=_cparams(("parallel", "arbitrary")),
        name="mlstm",
    )(mx, mz, *consts)


def _s5_param_kernel(are_ref, aim_ref, lstep_ref, bre_ref, bim_ref, lre_ref, lim_ref, bbre_ref, bbim_ref):
    a_re, a_im = are_ref[...], aim_ref[...]
    step = jnp.exp(lstep_ref[...])
    mag = jnp.exp(a_re * step)
    lam_re = mag * jnp.cos(a_im * step)
    lam_im = mag * jnp.sin(a_im * step)
    den = a_re * a_re + a_im * a_im
    coef_re = ((lam_re - 1.0) * a_re + lam_im * a_im) / den
    coef_im = (lam_im * a_re - (lam_re - 1.0) * a_im) / den
    b_re, b_im = bre_ref[...], bim_ref[...]
    lre_ref[...] = lam_re
    lim_ref[...] = lam_im
    bbre_ref[...] = coef_re * b_re - coef_im * b_im
    bbim_ref[...] = coef_re * b_im + coef_im * b_re


def _s5_params(a_re, a_im, log_step, b_re, b_im):
    rep = lambda t: jnp.repeat(t, S5_GROUP, axis=0)
    are, aim = rep(a_re), rep(a_im)
    lstep = jnp.broadcast_to(rep(log_step[:, None]), are.shape)
    breT = jnp.swapaxes(b_re, 1, 2).reshape(D_MODEL, S5_STATE)
    bimT = jnp.swapaxes(b_im, 1, 2).reshape(D_MODEL, S5_STATE)
    shp = jax.ShapeDtypeStruct((D_MODEL, S5_STATE), F32)
    return pl.pallas_call(_s5_param_kernel, out_shape=[shp, shp, shp, shp], name="s5_params")(
        are, aim, lstep, breT, bimT)


def _s5_kernel(x_ref, nw_ref, bs_ref, lam_ref, cs_ref, dsk_ref, wa_ref, ba_ref, wb_ref, bb_ref, o_ref,
               bu_ref, st_ref, *, batch):
    rows = x_ref.shape[0]
    steps = rows // batch

    @pl.when(pl.program_id(0) == 0)
    def _():
        st_ref[...] = jnp.zeros(st_ref.shape, F32)

    x = x_ref[...]
    u = _rms(x, nw_ref[...])
    u_b = u.astype(BF16)
    ys = []
    for k in range(S5_SLABS):
        bu_ref[...] = _dot(u_b[:, k * S5_SLAB_IN:(k + 1) * S5_SLAB_IN], bs_ref[k])
        for w in range(S5_SLAB_ST // S5_SCAN_W):
            re_l = slice(w * S5_SCAN_W, (w + 1) * S5_SCAN_W)
            im_l = slice(S5_SLAB_ST + w * S5_SCAN_W, S5_SLAB_ST + (w + 1) * S5_SCAN_W)
            lr = jnp.broadcast_to(lam_ref[k, 0:1, re_l], (batch, S5_SCAN_W))
            li = jnp.broadcast_to(lam_ref[k, 0:1, im_l], (batch, S5_SCAN_W))

            def step(t, carry, re_l=re_l, im_l=im_l, lr=lr, li=li):
                xr, xi = carry
                r0 = pl.multiple_of(t * batch, batch)
                br = bu_ref[pl.ds(r0, batch), re_l]
                bi = bu_ref[pl.ds(r0, batch), im_l]
                nr = lr * xr - li * xi + br
                ni = lr * xi + li * xr + bi
                bu_ref[pl.ds(r0, batch), re_l] = nr
                bu_ref[pl.ds(r0, batch), im_l] = ni
                return nr, ni

            xr, xi = lax.fori_loop(0, steps, step, (st_ref[k, :, re_l], st_ref[k, :, im_l]), unroll=4)
            st_ref[k, :, re_l] = xr
            st_ref[k, :, im_l] = xi
        ys.append(_dot(bu_ref[...].astype(BF16), cs_ref[k]))
    y = jnp.concatenate(ys, axis=1) + dsk_ref[...] * u
    g = jax.nn.gelu(y).astype(BF16)
    out = (_dot(g, wa_ref[...]) + ba_ref[...]) * _sigmoid(_dot(g, wb_ref[...]) + bb_ref[...])
    o_ref[...] = x + out


def _s5(x_sb, nw, b_slabs, lam_slabs, c_slabs, dsk, wa, ba, wb, bb, *, batch, seq, ts=32):
    rows = ts * batch
    consts = [nw, b_slabs, lam_slabs, c_slabs, dsk, wa, ba, wb, bb]
    return pl.pallas_call(
        functools.partial(_s5_kernel, batch=batch),
        grid=(seq // ts,),
        in_specs=[pl.BlockSpec((rows, D_MODEL), lambda i: (i, 0))] + [_const_spec(a.shape) for a in consts],
        out_specs=pl.BlockSpec((rows, D_MODEL), lambda i: (i, 0)),
        out_shape=jax.ShapeDtypeStruct((seq * batch, D_MODEL), F32),
        scratch_shapes=[pltpu.VMEM((rows, 2 * S5_SLAB_ST), F32),
                        pltpu.VMEM((S5_SLABS, batch, 2 * S5_SLAB_ST), F32)],
        compiler_params=_cparams(("arbitrary",)),
        name="s5",
    )(x_sb, *consts)


def _s5_slab_weights(lam_re, lam_im, bb_re, bb_im, c_re, c_im):
    eye = jnp.eye(S5_SLAB_GROUPS, dtype=F32)

    def in_slab(bb):
        t = bb.reshape(S5_SLABS, S5_SLAB_GROUPS, S5_GROUP, S5_STATE)
        t = t[:, :, :, None, :] * eye[None, :, None, :, None]
        return t.reshape(S5_SLABS, S5_SLAB_IN, S5_SLAB_ST)

    def out_slab(cc):
        t = jnp.swapaxes(cc, 1, 2).reshape(S5_SLABS, S5_SLAB_GROUPS, S5_STATE, S5_GROUP)
        t = t[:, :, :, None, :] * eye[None, :, None, :, None]
        return t.reshape(S5_SLABS, S5_SLAB_ST, S5_SLAB_IN)

    b_slabs = jnp.concatenate([in_slab(bb_re), in_slab(bb_im)], axis=2).astype(BF16)
    c_slabs = jnp.concatenate([out_slab(c_re), out_slab(-c_im)], axis=1).astype(BF16)
    lam = lambda t: t[::S5_GROUP].reshape(S5_SLABS, 1, S5_SLAB_ST)
    lam_slabs = jnp.concatenate([lam(lam_re), lam(lam_im)], axis=2)
    return b_slabs, jnp.broadcast_to(lam_slabs, (S5_SLABS, 8, 2 * S5_SLAB_ST)), c_slabs


def _blockdiag_tiles(w):
    nb = w.shape[0]
    per = nb // ML_HEADS
    wt = jnp.swapaxes(w, 1, 2).reshape(ML_HEADS, per, ML_QKV_BLOCK, ML_QKV_BLOCK)
    eye = jnp.eye(per, dtype=w.dtype)
    t = wt[:, :, :, None, :] * eye[None, :, None, :, None]
    return t.reshape(ML_HEADS, per * ML_QKV_BLOCK, per * ML_QKV_BLOCK)


def _pad_lanes(t, n=LANES):
    return jnp.pad(t, ((0, 0), (0, n - t.shape[1])))


def kernel(x, ffn1_norm, ffn1_w_gate, ffn1_w_up, ffn1_w_down, mix_norm, ffn2_norm, ffn2_w_gate, ffn2_w_up, ffn2_w_down, hy_w_in, ssd_conv_w, ssd_conv_b, ssd_dt_bias, ssd_a_log, ssd_d, ssd_norm_w, ml_conv_w, ml_conv_b, ml_w_q, ml_w_k, ml_w_v, ml_w_if, ml_b_if, ml_norm_w, ml_skip, hy_w_out, s5_a_re, s5_a_im, s5_log_step, s5_b_re, s5_b_im, s5_c_re, s5_c_im, s5_d, s5_w_a, s5_b_a, s5_w_b, s5_b_b, final_norm):
    batch, seq, _ = x.shape
    assert seq % 512 == 0 and batch % 8 == 0
    row = lambda t: t.reshape(1, -1)
    bf = lambda t: t.astype(BF16)
    xf = x.reshape(batch * seq, D_MODEL)

    x1 = _ffn(xf, row(ffn1_norm[0]), bf(ffn1_w_gate[0]), bf(ffn1_w_up[0]), bf(ffn1_w_down[0]),
              batch=batch, seq=seq)

    w_in = hy_w_in[0]
    o1 = SSD_D_INNER
    o2 = o1 + SSD_CONV_CH
    o3 = o2 + SSD_HEADS
    w_main = bf(jnp.concatenate([w_in[:, :o2], w_in[:, o3:]], axis=1))
    w_dt = w_in[:, o2:o3]
    z_s, xbc, m_x, m_z, dt_raw, dt_rawT = _inproj(x1, row(mix_norm[0]), w_main, bf(_pad_lanes(w_dt)),
                                                   bf(w_dt.T))

    y_ssd = _ssd(xbc, dt_raw, dt_rawT, z_s, ssd_conv_w[0], row(ssd_conv_b[0]),
                 _pad_lanes(row(ssd_dt_bias[0])), ssd_dt_bias[0].reshape(-1, 1),
                 _pad_lanes(row(ssd_a_log[0])), ssd_a_log[0].reshape(-1, 1),
                 row(jnp.repeat(ssd_d[0], SSD_HEAD_DIM)), row(ssd_norm_w[0]), batch=batch, seq=seq)

    k_scale = 1.0 / math.sqrt(ML_HEAD_DIM)
    w_if = ml_w_if[0]
    w_if = jnp.concatenate([w_if[:ML_D_INNER], w_if[ML_D_INNER:2 * ML_D_INNER] / k_scale, w_if[2 * ML_D_INNER:]],
                           axis=0)
    b_if = ml_b_if[0]
    y_ml = _mlstm(m_x, m_z, ml_conv_w[0], row(ml_conv_b[0]),
                  bf(_blockdiag_tiles(ml_w_q[0])), bf(_blockdiag_tiles(ml_w_k[0]) * k_scale),
                  bf(_blockdiag_tiles(ml_w_v[0])), bf(_pad_lanes(w_if)), bf(w_if.T),
                  _pad_lanes(row(b_if)), b_if.reshape(-1, 1), row(ml_norm_w[0]), row(ml_skip[0]),
                  batch=batch, seq=seq)

    w_out = bf(hy_w_out[0])
    x3 = _ffn(x1, row(ffn2_norm[0]), bf(ffn2_w_gate[0]), bf(ffn2_w_up[0]), bf(ffn2_w_down[0]),
              batch=batch, seq=seq, pre=(y_ssd, y_ml, w_out[:SSD_D_INNER], w_out[SSD_D_INNER:]))

    x4 = _ffn(x3.reshape(batch // SUBLANES, SUBLANES, seq, D_MODEL), row(ffn1_norm[1]), bf(ffn1_w_gate[1]),
              bf(ffn1_w_up[1]), bf(ffn1_w_down[1]), batch=batch, seq=seq, in_layout="bt", out_layout="tb")
    lam_re, lam_im, bb_re, bb_im = _s5_params(s5_a_re[0], s5_a_im[0], s5_log_step[0], s5_b_re[0], s5_b_im[0])
    b_slabs, lam_slabs, c_slabs = _s5_slab_weights(lam_re, lam_im, bb_re, bb_im, s5_c_re[0], s5_c_im[0])
    x5 = _s5(x4.reshape(seq * batch, D_MODEL), row(mix_norm[1]), b_slabs, lam_slabs, c_slabs, row(s5_d[0]),
             bf(s5_w_a[0]), row(s5_b_a[0]), bf(s5_w_b[0]), row(s5_b_b[0]), batch=batch, seq=seq)
    out = _ffn(x5.reshape(seq, batch // SUBLANES, SUBLANES, D_MODEL), row(ffn2_norm[1]), bf(ffn2_w_gate[1]),
               bf(ffn2_w_up[1]), bf(ffn2_w_down[1]), batch=batch, seq=seq, in_layout="tb", out_layout="bt",
               final_w=row(final_norm))
    return out.reshape(batch, seq, D_MODEL)
```

```python
import functools
import math

import jax
import jax.numpy as jnp
from jax import lax
from jax.experimental import pallas as pl
from jax.experimental.pallas import tpu as pltpu

F32 = jnp.float32
BF16 = jnp.bfloat16
HIGHEST = lax.Precision.HIGHEST

D_MODEL = 1024
EPS = 1e-6
D_FF = 2816
FFN_RES = 0.5
CONV_W = 4
CONV_HALO = 8

SSD_HEADS = 16
SSD_HEAD_DIM = 64
SSD_GROUPS = 2
SSD_D_STATE = 128
SSD_D_INNER = SSD_HEADS * SSD_HEAD_DIM
SSD_BC = SSD_GROUPS * SSD_D_STATE
SSD_CONV_CH = SSD_D_INNER + 2 * SSD_BC
SSD_GROUP_W = SSD_D_INNER // SSD_GROUPS

ML_HEADS = 4
ML_HEAD_DIM = 256
ML_D_INNER = ML_HEADS * ML_HEAD_DIM
ML_QKV_BLOCK = 4

CHUNK = 128
LANES = 128
SUBLANES = 8
SEQS_PER_STEP = 2

S5_GROUP = 16
S5_GROUPS = D_MODEL // S5_GROUP
S5_STATE = 64
S5_SLAB_GROUPS = 16
S5_SLABS = S5_GROUPS // S5_SLAB_GROUPS
S5_SLAB_IN = S5_SLAB_GROUPS * S5_GROUP
S5_SLAB_ST = S5_SLAB_GROUPS * S5_STATE
S5_SCAN_W = 512

VMEM_LIMIT_BYTES = 56 * 1024 * 1024


def _cparams(sem):
    return pltpu.CompilerParams(dimension_semantics=sem, vmem_limit_bytes=VMEM_LIMIT_BYTES)


def _dot(a, b):
    return jnp.dot(a, b, preferred_element_type=F32)


def _dot_f32(a, b):
    return jnp.dot(a, b, preferred_element_type=F32, precision=HIGHEST)


def _dot_nt(a, b):
    return lax.dot_general(a, b, (((1,), (1,)), ((), ())), preferred_element_type=F32)


def _rms(x, w):
    return x * lax.rsqrt(jnp.mean(x * x, axis=-1, keepdims=True) + EPS) * w


def _sigmoid(x):
    return 1.0 / (1.0 + jnp.exp(-x))


def _silu(x):
    return x * _sigmoid(x)


def _softplus(x):
    return jnp.maximum(x, 0.0) + jnp.log1p(jnp.exp(-jnp.abs(x)))


def _rep(x, n):
    return jnp.concatenate([x] * n, axis=-1)


def _tri(n, lower):
    r = lax.broadcasted_iota(jnp.int32, (n, n), 0)
    c = lax.broadcasted_iota(jnp.int32, (n, n), 1)
    return r >= c if lower else r <= c


def _const_spec(shape):
    nd = len(shape)
    return pl.BlockSpec(shape, lambda *_: (0,) * nd, pipeline_mode=pl.Buffered(1))


def _ffn_kernel(*refs, has_pre, has_final, in_layout, out_layout):
    it = iter(refs)
    x_ref = next(it)
    if has_pre:
        ys_ref, ym_ref, wos_ref, wom_ref = next(it), next(it), next(it), next(it)
    nw_ref, wg_ref, wu_ref, wd_ref = next(it), next(it), next(it), next(it)
    if has_final:
        fw_ref = next(it)
    o_ref = next(it)
    scr = next(it, None)

    if in_layout == "tb":
        ts = x_ref.shape[0]
        pitch = ts + 1
        for t in range(ts):
            for j in range(scr.shape[0]):
                scr[j, pl.ds(t, SUBLANES, stride=pitch), :] = x_ref[t, :, j * LANES:(j + 1) * LANES]
        x = jnp.concatenate(
            [jnp.concatenate([scr[j, b * pitch:b * pitch + ts, :] for b in range(SUBLANES)], axis=0)
             for j in range(scr.shape[0])], axis=1)
    elif in_layout == "bt":
        x = x_ref[...].reshape(-1, D_MODEL)
    else:
        x = x_ref[...]
    if has_pre:
        x = x + _dot(ys_ref[...].astype(BF16), wos_ref[...]) + _dot(ym_ref[...].astype(BF16), wom_ref[...])
    h = _rms(x, nw_ref[...]).astype(BF16)
    g = _dot(h, wg_ref[...])
    u = _dot(h, wu_ref[...])
    a = (_silu(g) * u).astype(BF16)
    x = x + FFN_RES * _dot(a, wd_ref[...])
    if has_final:
        x = _rms(x, fw_ref[...])
    if out_layout == "tb":
        ts = o_ref.shape[0]
        pitch = ts + 1
        for j in range(scr.shape[0]):
            for b in range(SUBLANES):
                scr[j, b * pitch:b * pitch + ts, :] = x[b * ts:(b + 1) * ts, j * LANES:(j + 1) * LANES]
        for t in range(ts):
            for j in range(scr.shape[0]):
                o_ref[t, :, j * LANES:(j + 1) * LANES] = scr[j, pl.ds(t, SUBLANES, stride=pitch), :]
    elif out_layout == "bt":
        o_ref[...] = x.reshape(o_ref.shape)
    else:
        o_ref[...] = x


def _row_spec(layout, tm, n_inner):
    if layout == "bs":
        return pl.BlockSpec((tm, D_MODEL), lambda b, i: (b * n_inner + i, 0))
    ts = tm // SUBLANES
    if layout == "bt":
        return pl.BlockSpec((None, SUBLANES, ts, D_MODEL), lambda b, i: (b, 0, i, 0))
    return pl.BlockSpec((ts, None, SUBLANES, D_MODEL), lambda b, i: (i, b, 0, 0))


def _ffn(x, nw, wg, wu, wd, *, batch, seq, in_layout="bs", out_layout="bs", tm=512, pre=None, final_w=None):
    if in_layout == "bs":
        grid = (batch, seq // tm)
        n_inner = seq // tm
    else:
        grid = (batch // SUBLANES, seq * SUBLANES // tm)
        n_inner = None
    args = [x]
    specs = [_row_spec(in_layout, tm, n_inner)]
    if pre is not None:
        ys, ym, wos, wom = pre
        args += [ys, ym, wos, wom]
        specs += [_row_spec("bs", tm, n_inner), _row_spec("bs", tm, n_inner),
                  _const_spec(wos.shape), _const_spec(wom.shape)]
    args += [nw, wg, wu, wd]
    specs += [_const_spec(nw.shape), _const_spec(wg.shape), _const_spec(wu.shape), _const_spec(wd.shape)]
    if final_w is not None:
        args.append(final_w)
        specs.append(_const_spec(final_w.shape))
    out_shape = {"bs": (batch * seq, D_MODEL),
                 "bt": (batch // SUBLANES, SUBLANES, seq, D_MODEL),
                 "tb": (seq, batch // SUBLANES, SUBLANES, D_MODEL)}[out_layout]
    scratch = ([pltpu.VMEM((D_MODEL // LANES, tm + SUBLANES, LANES), F32)]
               if "tb" in (in_layout, out_layout) else [])
    return pl.pallas_call(
        functools.partial(_ffn_kernel, has_pre=pre is not None, has_final=final_w is not None,
                          in_layout=in_layout, out_layout=out_layout),
        grid=grid,
        in_specs=specs,
        out_specs=_row_spec(out_layout, tm, n_inner),
        out_shape=jax.ShapeDtypeStruct(out_shape, F32),
        scratch_shapes=scratch,
        compiler_params=_cparams(("parallel", "parallel")),
        name="ffn",
    )(*args)


def _inproj_kernel(x_ref, nw_ref, wm_ref, wdt_ref, wdtT_ref, z_ref, xbc_ref, mx_ref, mz_ref, dt_ref, dtT_ref):
    u = _rms(x_ref[...], nw_ref[...]).astype(BF16)
    proj = _dot(u, wm_ref[...])
    o1 = SSD_D_INNER
    o2 = o1 + SSD_CONV_CH
    o3 = o2 + ML_D_INNER
    z_ref[...] = proj[:, :o1]
    xbc_ref[...] = proj[:, o1:o2]
    mx_ref[...] = proj[:, o2:o3]
    mz_ref[...] = proj[:, o3:]
    dt_ref[...] = _dot(u, wdt_ref[...])
    dtT_ref[...] = _dot_nt(wdtT_ref[...], u)


def _inproj(x, nw, w_main, w_dt, w_dtT, *, tm=512):
    rows = x.shape[0]
    n = rows // tm
    row = lambda w: pl.BlockSpec((tm, w), lambda i: (i, 0))
    return pl.pallas_call(
        _inproj_kernel,
        grid=(n,),
        in_specs=[row(D_MODEL), _const_spec(nw.shape), _const_spec(w_main.shape),
                  _const_spec(w_dt.shape), _const_spec(w_dtT.shape)],
        out_specs=[row(SSD_D_INNER), row(SSD_CONV_CH), row(ML_D_INNER), row(ML_D_INNER), row(LANES),
                   pl.BlockSpec((SSD_HEADS, tm), lambda i: (0, i))],
        out_shape=[jax.ShapeDtypeStruct((rows, SSD_D_INNER), F32),
                   jax.ShapeDtypeStruct((rows, SSD_CONV_CH), F32),
                   jax.ShapeDtypeStruct((rows, ML_D_INNER), F32),
                   jax.ShapeDtypeStruct((rows, ML_D_INNER), F32),
                   jax.ShapeDtypeStruct((rows, LANES), F32),
                   jax.ShapeDtypeStruct((SSD_HEADS, rows), F32)],
        compiler_params=_cparams(("parallel",)),
        name="inproj",
    )(x, nw, w_main, w_dt, w_dtT)


def _causal_conv_silu(cbuf, x_ref, w_ref, b_ref):
    L = x_ref.shape[0]
    cbuf[CONV_HALO:CONV_HALO + L, :] = x_ref[...]
    ext = cbuf[...]
    prev = pltpu.roll(ext, 1, axis=0)
    near = ext * w_ref[3:4, :] + prev * w_ref[2:3, :]
    far = ext * w_ref[1:2, :] + prev * w_ref[0:1, :]
    acc = b_ref[...] + near[CONV_HALO:, :] + pltpu.roll(far, 2, axis=0)[CONV_HALO:, :]
    cbuf[0:CONV_HALO, :] = ext[L:L + CONV_HALO, :]
    return _silu(acc)


def _ssd_kernel(*refs):
    xbc_ref, dt_ref = refs[:2]
    dtT_refs = refs[2:2 + SEQS_PER_STEP]
    z_ref = refs[2 + SEQS_PER_STEP]
    consts = refs[3 + SEQS_PER_STEP:-3]
    o_ref, cbuf, st_ref = refs[-3:]

    @pl.when(pl.program_id(1) == 0)
    def _():
        st_ref[...] = jnp.zeros(st_ref.shape, F32)
        cbuf[:, 0:CONV_HALO, :] = jnp.zeros((SEQS_PER_STEP, CONV_HALO, cbuf.shape[2]), F32)

    for s in range(SEQS_PER_STEP):
        _ssd_chunk(xbc_ref.at[s], dt_ref.at[s], dtT_refs[s], z_ref.at[s], *consts,
                   o_ref.at[s], cbuf.at[s], st_ref.at[s])


def _ssd_chunk(xbc_ref, dt_ref, dtT_ref, z_ref, cw_ref, cb_ref, dtb_ref, dtbT_ref, alog_ref, alogT_ref,
               dexp_ref, nw_ref, o_ref, cbuf, st_ref):
    L = CHUNK
    xbc = _causal_conv_silu(cbuf, xbc_ref, cw_ref, cb_ref)
    xs = xbc[:, :SSD_D_INNER]
    bm = xbc[:, SSD_D_INNER:SSD_D_INNER + SSD_BC]
    cm = xbc[:, SSD_D_INNER + SSD_BC:]

    dt = _softplus(dt_ref[...] + dtb_ref[...])
    adt = dt * (-jnp.exp(alog_ref[...]))
    dtT = _softplus(dtT_ref[...] + dtbT_ref[...])
    adtT = dtT * (-jnp.exp(alogT_ref[...]))
    tril = _tri(L, True)
    a_cs = _dot_f32(tril.astype(F32), adt)
    a_csT = _dot_f32(adtT, _tri(L, False).astype(F32))
    w_st = dt * jnp.exp(a_cs[L - 1:L, :] - a_cs)
    left = lax.broadcasted_iota(jnp.int32, (L, LANES), 1) < SSD_HEAD_DIM

    def col(a, h):
        return jnp.broadcast_to(a[:, h:h + 1], (L, LANES))

    y_parts = []
    heads_per_group = SSD_HEADS // SSD_GROUPS
    for g in range(SSD_GROUPS):
        gs = slice(g * SSD_GROUP_W, (g + 1) * SSD_GROUP_W)
        bm_g = bm[:, g * SSD_D_STATE:(g + 1) * SSD_D_STATE]
        cm_g = cm[:, g * SSD_D_STATE:(g + 1) * SSD_D_STATE].astype(BF16)
        cb = _dot_nt(cm_g, bm_g.astype(BF16))
        st_prev = st_ref[:, gs]
        y_off = _dot(cm_g, st_prev.astype(BF16))
        diag, acs_pairs, xdec_pairs = [], [], []
        for hp in range(heads_per_group // 2):
            h0 = g * heads_per_group + 2 * hp
            cols = [col(a_cs, h0), col(a_cs, h0 + 1)]
            ms = []
            for h, acs_col in zip((h0, h0 + 1), cols):
                lm = jnp.exp(jnp.where(tril, acs_col - a_csT[h:h + 1, :], -jnp.inf))
                ms.append((cb * lm * dtT[h:h + 1, :]).astype(BF16))
            slab = xs[:, h0 * SSD_HEAD_DIM:(h0 + 2) * SSD_HEAD_DIM]
            rhs = jnp.concatenate([jnp.where(left, slab, 0.0), jnp.where(left, 0.0, slab)], axis=0)
            diag.append(_dot(jnp.concatenate(ms, axis=1), rhs.astype(BF16)))
            acs_pairs.append(jnp.where(left, cols[0], cols[1]))
            xdec_pairs.append(slab * jnp.where(left, col(w_st, h0), col(w_st, h0 + 1)))
        acs_e = jnp.concatenate(acs_pairs, axis=1)
        y_parts.append(jnp.concatenate(diag, axis=1) + y_off * jnp.exp(acs_e))
        xd_dec = jnp.concatenate(xdec_pairs, axis=1).astype(BF16)
        st_ref[:, gs] = st_prev * jnp.exp(acs_e[L - 1:L, :]) + _dot(bm_g.T.astype(BF16), xd_dec)

    y = jnp.concatenate(y_parts, axis=1) + dexp_ref[...] * xs
    yg = y * _silu(z_ref[...])
    outs = []
    for g in range(SSD_GROUPS):
        v = yg[:, g * SSD_GROUP_W:(g + 1) * SSD_GROUP_W]
        outs.append(v * lax.rsqrt(jnp.mean(v * v, axis=-1, keepdims=True) + EPS))
    o_ref[...] = jnp.concatenate(outs, axis=1) * nw_ref[...]


def _ssd(xbc, dt, dtT, z, cw, cb, dtb, dtbT, alog, alogT, dexp, nw, *, batch, seq):
    L = CHUNK
    nc = seq // L
    n = SEQS_PER_STEP
    seqs = lambda t: t.reshape(batch // n, n, seq, t.shape[-1])
    row = lambda w: pl.BlockSpec((None, n, L, w), lambda b, c: (b, 0, c, 0))
    dtT_spec = lambda s: pl.BlockSpec((SSD_HEADS, L), lambda b, c: (0, (b * n + s) * nc + c))
    consts = [cw, cb, dtb, dtbT, alog, alogT, dexp, nw]
    out = pl.pallas_call(
        _ssd_kernel,
        grid=(batch // n, nc),
        in_specs=[row(SSD_CONV_CH), row(LANES)] + [dtT_spec(s) for s in range(n)] + [row(SSD_D_INNER)]
                 + [_const_spec(a.shape) for a in consts],
        out_specs=row(SSD_D_INNER),
        out_shape=jax.ShapeDtypeStruct((batch // n, n, seq, SSD_D_INNER), F32),
        scratch_shapes=[pltpu.VMEM((n, CONV_HALO + L, SSD_CONV_CH), F32),
                        pltpu.VMEM((n, SSD_D_STATE, SSD_D_INNER), F32)],
        compiler_params=_cparams(("parallel", "arbitrary")),
        name="ssd",
    )(seqs(xbc), seqs(dt), *([dtT] * n), seqs(z), *consts)
    return out.reshape(batch * seq, SSD_D_INNER)


def _mlstm_kernel(*refs):
    mx_ref, mz_ref = refs[:2]
    cw_ref, cb_ref, wq_ref, wk_ref, wv_ref, wif_ref, bif_ref, nw_ref, skip_ref = refs[2:-4]
    o_ref, cbuf, ct_ref, m_ref = refs[-4:]
    seqs = range(SEQS_PER_STEP)

    @pl.when(pl.program_id(1) == 0)
    def _():
        ct_ref[...] = jnp.zeros(ct_ref.shape, F32)
        m_ref[...] = jnp.full(m_ref.shape, -1e30, F32)
        cbuf[:, 0:CONV_HALO, :] = jnp.zeros((SEQS_PER_STEP, CONV_HALO, cbuf.shape[2]), F32)

    pre = [_mlstm_project(mx_ref.at[s], cbuf.at[s], cw_ref, cb_ref, wq_ref, wk_ref, wv_ref, wif_ref, bif_ref)
           for s in seqs]
    heads = [[] for _ in seqs]
    for h in range(ML_HEADS):
        for s in seqs:
            heads[s].append(_mlstm_head(h, pre[s], ct_ref.at[s, h], m_ref.at[s, h]))
    for s in seqs:
        hm = jnp.concatenate(heads[s], axis=1) * nw_ref[...]
        o_ref[s] = (hm + skip_ref[...] * pre[s]["xc"]) * _silu(mz_ref[s])


def _mlstm_project(mx_ref, cbuf, cw_ref, cb_ref, wq_ref, wk_ref, wv_ref, wif_ref, bif_ref):
    L = CHUNK
    mx = mx_ref[...]
    xc = _causal_conv_silu(cbuf, mx_ref, cw_ref, cb_ref)
    xc_b = xc.astype(BF16)
    mx_b = mx.astype(BF16)
    tile = 2 * LANES
    q = jnp.concatenate([_dot(xc_b[:, t * tile:(t + 1) * tile], wq_ref[t]) for t in range(ML_HEADS)], axis=1)
    k = jnp.concatenate([_dot(xc_b[:, t * tile:(t + 1) * tile], wk_ref[t]) for t in range(ML_HEADS)], axis=1)
    v = jnp.concatenate([_dot(mx_b[:, t * tile:(t + 1) * tile], wv_ref[t]) for t in range(ML_HEADS)], axis=1)
    qkv = jnp.concatenate([q, k, v], axis=1).astype(BF16)
    gates = _dot(qkv, wif_ref[...]) + bif_ref[...]
    gatesT = gates.T[0:2 * ML_HEADS, :]
    logf = -_softplus(-gates)
    logfT = -_softplus(-gatesT)
    bcum = _dot_f32(_tri(L, True).astype(F32), logf)
    bcumT = _dot_f32(logfT, _tri(L, False).astype(F32))
    return dict(xc=xc, qkv=qkv, k=k, gates=gates, gatesT=gatesT, bcum=bcum, bcumT=bcumT)


def _mlstm_head(h, pre, ct_ref, m_ref):
    L = CHUNK
    tril = _tri(L, True)
    qkv, gates, gatesT, bcum, bcumT = pre["qkv"], pre["gates"], pre["gatesT"], pre["bcum"], pre["bcumT"]
    hs = slice(h * ML_HEAD_DIM, (h + 1) * ML_HEAD_DIM)
    qh = qkv[:, hs]
    kh = qkv[:, ML_D_INNER + hs.start:ML_D_INNER + hs.stop]
    vh = qkv[:, 2 * ML_D_INNER + hs.start:2 * ML_D_INNER + hs.stop]
    v_aug = jnp.concatenate([vh, jnp.ones((L, LANES), BF16)], axis=1)
    bc = jnp.broadcast_to(bcum[:, ML_HEADS + h:ML_HEADS + h + 1], (L, LANES))
    ig = jnp.broadcast_to(gates[:, h:h + 1], (L, LANES))
    m_prev = m_ref[0:1, :]
    ct_prev = ct_ref[...]

    dlog = jnp.where(tril, bc - bcumT[ML_HEADS + h:ML_HEADS + h + 1, :] + gatesT[h:h + 1, :], -jnp.inf)
    m_inter = bc + m_prev
    m_t = jnp.maximum(jnp.max(dlog, axis=-1, keepdims=True), m_inter)
    scores = _dot_nt(qh, kh) * jnp.exp(dlog - m_t)
    inter_w = jnp.exp(m_inter - m_t)
    num = _dot(scores.astype(BF16), v_aug) + _rep(inter_w, 3) * _dot(qh, ct_prev.astype(BF16))
    den = jnp.maximum(jnp.abs(num[:, ML_HEAD_DIM:]), jnp.exp(-m_t))
    hout = num[:, :ML_HEAD_DIM] / _rep(den, 2)

    b_last = bc[L - 1:L, :]
    w_state = b_last - bc + ig
    m_new = jnp.maximum(b_last + m_prev, jnp.max(w_state, axis=0, keepdims=True))
    decay = jnp.exp(b_last + m_prev - m_new)
    kw = (pre["k"][:, hs] * _rep(jnp.exp(w_state - m_new), 2)).T.astype(BF16)
    ct_ref[...] = _rep(decay, 3) * ct_prev + _dot(kw, v_aug)
    m_ref[...] = jnp.broadcast_to(m_new, m_ref.shape)

    mu = jnp.mean(hout, axis=-1, keepdims=True)
    d = hout - mu
    var = jnp.mean(d * d, axis=-1, keepdims=True)
    return d * lax.rsqrt(var + EPS)


def _mlstm(mx, mz, cw, cb, wq, wk, wv, wif, bif, nw, skip, *, batch, seq):
    L = CHUNK
    nc = seq // L
    n = SEQS_PER_STEP
    seqs = lambda t: t.reshape(batch // n, n, seq, t.shape[-1])
    row = lambda w: pl.BlockSpec((None, n, L, w), lambda b, c: (b, 0, c, 0))
    consts = [cw, cb, wq, wk, wv, wif, bif, nw, skip]
    out = pl.pallas_call(
        _mlstm_kernel,
        grid=(batch // n, nc),
        in_specs=[row(ML_D_INNER), row(ML_D_INNER)] + [_const_spec(a.shape) for a in consts],
        out_specs=row(ML_D_INNER),
        out_shape=jax.ShapeDtypeStruct((batch // n, n, seq, ML_D_INNER), F32),
        scratch_shapes=[pltpu.VMEM((n, CONV_HALO + L, ML_D_INNER), F32),
                        pltpu.VMEM((n, ML_HEADS, ML_HEAD_DIM, ML_HEAD_DIM + LANES), F32),
                        pltpu.VMEM((n, ML_HEADS, SUBLANES, LANES), F32)],
        compiler_params=_cparams(("parallel", "arbitrary")),
        name="mlstm",
    )(seqs(mx), seqs(mz), *consts)
    return out.reshape(batch * seq, ML_D_INNER)


def _s5_param_kernel(are_ref, aim_ref, lstep_ref, bre_ref, bim_ref, lre_ref, lim_ref, bbre_ref, bbim_ref):
    a_re, a_im = are_ref[...], aim_ref[...]
    step = jnp.exp(lstep_ref[...])
    mag = jnp.exp(a_re * step)
    lam_re = mag * jnp.cos(a_im * step)
    lam_im = mag * jnp.sin(a_im * step)
    den = a_re * a_re + a_im * a_im
    coef_re = ((lam_re - 1.0) * a_re + lam_im * a_im) / den
    coef_im = (lam_im * a_re - (lam_re - 1.0) * a_im) / den
    b_re, b_im = bre_ref[...], bim_ref[...]
    lre_ref[...] = lam_re
    lim_ref[...] = lam_im
    bbre_ref[...] = coef_re * b_re - coef_im * b_im
    bbim_ref[...] = coef_re * b_im + coef_im * b_re


def _s5_params(a_re, a_im, log_step, b_re, b_im):
    rep = lambda t: jnp.repeat(t, S5_GROUP, axis=0)
    are, aim = rep(a_re), rep(a_im)
    lstep = jnp.broadcast_to(rep(log_step[:, None]), are.shape)
    breT = jnp.swapaxes(b_re, 1, 2).reshape(D_MODEL, S5_STATE)
    bimT = jnp.swapaxes(b_im, 1, 2).reshape(D_MODEL, S5_STATE)
    shp = jax.ShapeDtypeStruct((D_MODEL, S5_STATE), F32)
    return pl.pallas_call(_s5_param_kernel, out_shape=[shp, shp, shp, shp], name="s5_params")(
        are, aim, lstep, breT, bimT)


def _s5_kernel(x_ref, nw_ref, bs_ref, lam_ref, cs_ref, dsk_ref, wa_ref, ba_ref, wb_ref, bb_ref, o_ref,
               bu_ref, xs_ref, st_ref, *, batch):
    rows = x_ref.shape[0]
    steps = rows // batch

    @pl.when(pl.program_id(0) == 0)
    def _():
        st_ref[...] = jnp.zeros(st_ref.shape, F32)

    x = x_ref[...]
    u = _rms(x, nw_ref[...])
    u_b = u.astype(BF16)
    def bu_slab(k):
        bu_ref[k % 2] = _dot(u_b[:, k * S5_SLAB_IN:(k + 1) * S5_SLAB_IN], bs_ref[k])

    ys = []
    bu_slab(0)
    for k in range(S5_SLABS):
        if k + 1 < S5_SLABS:
            bu_slab(k + 1)
        for w in range(S5_SLAB_ST // S5_SCAN_W):
            re_l = slice(w * S5_SCAN_W, (w + 1) * S5_SCAN_W)
            im_l = slice(S5_SLAB_ST + w * S5_SCAN_W, S5_SLAB_ST + (w + 1) * S5_SCAN_W)
            lr = jnp.broadcast_to(lam_ref[k, 0:1, re_l], (batch, S5_SCAN_W))
            li = jnp.broadcast_to(lam_ref[k, 0:1, im_l], (batch, S5_SCAN_W))
            xr, xi = st_ref[k, :, re_l], st_ref[k, :, im_l]
            for t in range(steps):
                rows_t = slice(t * batch, (t + 1) * batch)
                xr, xi = (lr * xr - li * xi + bu_ref[k % 2, rows_t, re_l],
                          lr * xi + li * xr + bu_ref[k % 2, rows_t, im_l])
                xs_ref[k % 2, rows_t, re_l] = xr.astype(BF16)
                xs_ref[k % 2, rows_t, im_l] = xi.astype(BF16)
            st_ref[k, :, re_l] = xr
            st_ref[k, :, im_l] = xi
        ys.append(_dot(xs_ref[k % 2], cs_ref[k]))
    y = jnp.concatenate(ys, axis=1) + dsk_ref[...] * u
    g = jax.nn.gelu(y).astype(BF16)
    out = (_dot(g, wa_ref[...]) + ba_ref[...]) * _sigmoid(_dot(g, wb_ref[...]) + bb_ref[...])
    o_ref[...] = x + out


def _s5(x_sb, nw, b_slabs, lam_slabs, c_slabs, dsk, wa, ba, wb, bb, *, batch, seq, ts=32):
    rows = ts * batch
    consts = [nw, b_slabs, lam_slabs, c_slabs, dsk, wa, ba, wb, bb]
    return pl.pallas_call(
        functools.partial(_s5_kernel, batch=batch),
        grid=(seq // ts,),
        in_specs=[pl.BlockSpec((rows, D_MODEL), lambda i: (i, 0))] + [_const_spec(a.shape) for a in consts],
        out_specs=pl.BlockSpec((rows, D_MODEL), lambda i: (i, 0)),
        out_shape=jax.ShapeDtypeStruct((seq * batch, D_MODEL), F32),
        scratch_shapes=[pltpu.VMEM((2, rows, 2 * S5_SLAB_ST), F32),
                        pltpu.VMEM((2, rows, 2 * S5_SLAB_ST), BF16),
                        pltpu.VMEM((S5_SLABS, batch, 2 * S5_SLAB_ST), F32)],
        compiler_params=_cparams(("arbitrary",)),
        name="s5",
    )(x_sb, *consts)


def _s5_slab_weights(lam_re, lam_im, bb_re, bb_im, c_re, c_im):
    eye = jnp.eye(S5_SLAB_GROUPS, dtype=F32)

    def in_slab(bb):
        t = bb.reshape(S5_SLABS, S5_SLAB_GROUPS, S5_GROUP, S5_STATE)
        t = t[:, :, :, None, :] * eye[None, :, None, :, None]
        return t.reshape(S5_SLABS, S5_SLAB_IN, S5_SLAB_ST)

    def out_slab(cc):
        t = jnp.swapaxes(cc, 1, 2).reshape(S5_SLABS, S5_SLAB_GROUPS, S5_STATE, S5_GROUP)
        t = t[:, :, :, None, :] * eye[None, :, None, :, None]
        return t.reshape(S5_SLABS, S5_SLAB_ST, S5_SLAB_IN)

    b_slabs = jnp.concatenate([in_slab(bb_re), in_slab(bb_im)], axis=2).astype(BF16)
    c_slabs = jnp.concatenate([out_slab(c_re), out_slab(-c_im)], axis=1).astype(BF16)
    lam = lambda t: t[::S5_GROUP].reshape(S5_SLABS, 1, S5_SLAB_ST)
    lam_slabs = jnp.concatenate([lam(lam_re), lam(lam_im)], axis=2)
    return b_slabs, jnp.broadcast_to(lam_slabs, (S5_SLABS, 8, 2 * S5_SLAB_ST)), c_slabs


def _blockdiag_tiles(w):
    nb = w.shape[0]
    tile = nb * ML_QKV_BLOCK // ML_HEADS
    rows = jnp.swapaxes(w, 1, 2).reshape(nb * ML_QKV_BLOCK, ML_QKV_BLOCK)
    sel = (jnp.arange(tile)[None, :] % ML_QKV_BLOCK == jnp.arange(ML_QKV_BLOCK)[:, None]).astype(w.dtype)
    wide = jnp.dot(rows, sel, precision=HIGHEST)
    blk_r = (jnp.arange(nb * ML_QKV_BLOCK) % tile) // ML_QKV_BLOCK
    blk_c = jnp.arange(tile) // ML_QKV_BLOCK
    wide = jnp.where(blk_r[:, None] == blk_c[None, :], wide, 0.0)
    return wide.reshape(ML_HEADS, tile, tile)


def _pad_lanes(t, n=LANES):
    return jnp.pad(t, ((0, 0), (0, n - t.shape[1])))


def kernel(x, ffn1_norm, ffn1_w_gate, ffn1_w_up, ffn1_w_down, mix_norm, ffn2_norm, ffn2_w_gate, ffn2_w_up, ffn2_w_down, hy_w_in, ssd_conv_w, ssd_conv_b, ssd_dt_bias, ssd_a_log, ssd_d, ssd_norm_w, ml_conv_w, ml_conv_b, ml_w_q, ml_w_k, ml_w_v, ml_w_if, ml_b_if, ml_norm_w, ml_skip, hy_w_out, s5_a_re, s5_a_im, s5_log_step, s5_b_re, s5_b_im, s5_c_re, s5_c_im, s5_d, s5_w_a, s5_b_a, s5_w_b, s5_b_b, final_norm):
    batch, seq, _ = x.shape
    assert seq % 512 == 0 and batch % 8 == 0
    row = lambda t: t.reshape(1, -1)
    bf = lambda t: t.astype(BF16)
    xf = x.reshape(batch * seq, D_MODEL)

    x1 = _ffn(xf, row(ffn1_norm[0]), bf(ffn1_w_gate[0]), bf(ffn1_w_up[0]), bf(ffn1_w_down[0]),
              batch=batch, seq=seq)

    w_in = hy_w_in[0]
    o1 = SSD_D_INNER
    o2 = o1 + SSD_CONV_CH
    o3 = o2 + SSD_HEADS
    w_main = bf(jnp.concatenate([w_in[:, :o2], w_in[:, o3:]], axis=1))
    w_dt = w_in[:, o2:o3]
    z_s, xbc, m_x, m_z, dt_raw, dt_rawT = _inproj(x1, row(mix_norm[0]), w_main, bf(_pad_lanes(w_dt)),
                                                   bf(w_dt.T))

    y_ssd = _ssd(xbc, dt_raw, dt_rawT, z_s, ssd_conv_w[0], row(ssd_conv_b[0]),
                 _pad_lanes(row(ssd_dt_bias[0])), ssd_dt_bias[0].reshape(-1, 1),
                 _pad_lanes(row(ssd_a_log[0])), ssd_a_log[0].reshape(-1, 1),
                 row(jnp.repeat(ssd_d[0], SSD_HEAD_DIM)), row(ssd_norm_w[0]), batch=batch, seq=seq)

    k_scale = 1.0 / math.sqrt(ML_HEAD_DIM)
    w_if = ml_w_if[0]
    w_if = jnp.concatenate([w_if[:ML_D_INNER], w_if[ML_D_INNER:2 * ML_D_INNER] / k_scale, w_if[2 * ML_D_INNER:]],
                           axis=0)
    b_if = ml_b_if[0]
    y_ml = _mlstm(m_x, m_z, ml_conv_w[0], row(ml_conv_b[0]),
                  bf(_blockdiag_tiles(ml_w_q[0])), bf(_blockdiag_tiles(ml_w_k[0]) * k_scale),
                  bf(_blockdiag_tiles(ml_w_v[0])), bf(_pad_lanes(w_if)),
                  _pad_lanes(row(b_if)), row(ml_norm_w[0]), row(ml_skip[0]),
                  batch=batch, seq=seq)

    w_out = bf(hy_w_out[0])
    x3 = _ffn(x1, row(ffn2_norm[0]), bf(ffn2_w_gate[0]), bf(ffn2_w_up[0]), bf(ffn2_w_down[0]),
              batch=batch, seq=seq, pre=(y_ssd, y_ml, w_out[:SSD_D_INNER], w_out[SSD_D_INNER:]))

    x4 = _ffn(x3.reshape(batch // SUBLANES, SUBLANES, seq, D_MODEL), row(ffn1_norm[1]), bf(ffn1_w_gate[1]),
              bf(ffn1_w_up[1]), bf(ffn1_w_down[1]), batch=batch, seq=seq, in_layout="bt", out_layout="tb")
    lam_re, lam_im, bb_re, bb_im = _s5_params(s5_a_re[0], s5_a_im[0], s5_log_step[0], s5_b_re[0], s5_b_im[0])
    b_slabs, lam_slabs, c_slabs = _s5_slab_weights(lam_re, lam_im, bb_re, bb_im, s5_c_re[0], s5_c_im[0])
    x5 = _s5(x4.reshape(seq * batch, D_MODEL), row(mix_norm[1]), b_slabs, lam_slabs, c_slabs, row(s5_d[0]),
             bf(s5_w_a[0]), row(s5_b_a[0]), bf(s5_w_b[0]), row(s5_b_b[0]), batch=batch, seq=seq)
    out = _ffn(x5.reshape(seq, batch // SUBLANES, SUBLANES, D_MODEL), row(ffn2_norm[1]), bf(ffn2_w_gate[1]),
               bf(ffn2_w_up[1]), bf(ffn2_w_down[1]), batch=batch, seq=seq, in_layout="tb", out_layout="bt",
               final_w=row(final_norm))
    return out.reshape(batch, seq, D_MODEL)
```

```python
import functools
import math

import jax
import jax.numpy as jnp
from jax import lax
from jax.experimental import pallas as pl
from jax.experimental.pallas import tpu as pltpu

F32 = jnp.float32
BF16 = jnp.bfloat16
HIGHEST = lax.Precision.HIGHEST

D_MODEL = 1024
EPS = 1e-6
D_FF = 2816
FFN_RES = 0.5
CONV_W = 4
CONV_HALO = 8

SSD_HEADS = 16
SSD_HEAD_DIM = 64
SSD_GROUPS = 2
SSD_D_STATE = 128
SSD_D_INNER = SSD_HEADS * SSD_HEAD_DIM
SSD_BC = SSD_GROUPS * SSD_D_STATE
SSD_CONV_CH = SSD_D_INNER + 2 * SSD_BC
SSD_GROUP_W = SSD_D_INNER // SSD_GROUPS

ML_HEADS = 4
ML_HEAD_DIM = 256
ML_D_INNER = ML_HEADS * ML_HEAD_DIM
ML_QKV_BLOCK = 4

CHUNK = 128
LANES = 128
SUBLANES = 8
SEQS_PER_STEP = 2

S5_GROUP = 16
S5_GROUPS = D_MODEL // S5_GROUP
S5_STATE = 64
S5_SLAB_GROUPS = 16
S5_SLABS = S5_GROUPS // S5_SLAB_GROUPS
S5_SLAB_IN = S5_SLAB_GROUPS * S5_GROUP
S5_SLAB_ST = S5_SLAB_GROUPS * S5_STATE
S5_SCAN_W = 512

VMEM_LIMIT_BYTES = 56 * 1024 * 1024


def _cparams(sem):
    return pltpu.CompilerParams(dimension_semantics=sem, vmem_limit_bytes=VMEM_LIMIT_BYTES)


def _dot(a, b):
    return jnp.dot(a, b, preferred_element_type=F32)


def _dot_f32(a, b):
    return jnp.dot(a, b, preferred_element_type=F32, precision=HIGHEST)


def _dot_nt(a, b):
    return lax.dot_general(a, b, (((1,), (1,)), ((), ())), preferred_element_type=F32)


def _rms(x, w):
    return x * lax.rsqrt(jnp.mean(x * x, axis=-1, keepdims=True) + EPS) * w


def _sigmoid(x):
    return 1.0 / (1.0 + jnp.exp(-x))


def _silu(x):
    return x * _sigmoid(x)


def _softplus(x):
    return jnp.maximum(x, 0.0) + jnp.log1p(jnp.exp(-jnp.abs(x)))


def _rep(x, n):
    return jnp.concatenate([x] * n, axis=-1)


def _tri(n, lower):
    r = lax.broadcasted_iota(jnp.int32, (n, n), 0)
    c = lax.broadcasted_iota(jnp.int32, (n, n), 1)
    return r >= c if lower else r <= c


def _const_spec(shape):
    nd = len(shape)
    return pl.BlockSpec(shape, lambda *_: (0,) * nd, pipeline_mode=pl.Buffered(1))


def _layer_spec(shape, layer):
    return pl.BlockSpec((None,) + tuple(shape[1:]), lambda *_: (layer, 0, 0), pipeline_mode=pl.Buffered(1))


def _to_bf16_kernel(*refs):
    n = len(refs) // 2
    for src, dst in zip(refs[:n], refs[n:]):
        dst[...] = src[...].astype(BF16)


def _to_bf16(weights, blocks=8):
    specs = [pl.BlockSpec((1, w.shape[1] // blocks, w.shape[2]), lambda l, i: (l, i, 0)) for w in weights]
    return pl.pallas_call(
        _to_bf16_kernel,
        grid=(weights[0].shape[0], blocks),
        in_specs=specs,
        out_specs=specs,
        out_shape=[jax.ShapeDtypeStruct(w.shape, BF16) for w in weights],
        compiler_params=_cparams(("parallel", "parallel")),
        name="to_bf16",
    )(*weights)


def _ffn_kernel(*refs, has_pre, has_final, in_layout, out_layout):
    it = iter(refs)
    x_ref = next(it)
    if has_pre:
        ys_ref, ym_ref, wos_ref, wom_ref = next(it), next(it), next(it), next(it)
    nw_ref, wg_ref, wu_ref, wd_ref = next(it), next(it), next(it), next(it)
    if has_final:
        fw_ref = next(it)
    o_ref = next(it)
    scr = next(it, None)

    if in_layout == "tb":
        ts = x_ref.shape[0]
        pitch = ts + 1
        for t in range(ts):
            for j in range(scr.shape[0]):
                scr[j, pl.ds(t, SUBLANES, stride=pitch), :] = x_ref[t, :, j * LANES:(j + 1) * LANES]
        x = jnp.concatenate(
            [jnp.concatenate([scr[j, b * pitch:b * pitch + ts, :] for b in range(SUBLANES)], axis=0)
             for j in range(scr.shape[0])], axis=1)
    elif in_layout == "bt":
        x = x_ref[...].reshape(-1, D_MODEL)
    else:
        x = x_ref[...]
    if has_pre:
        x = x + _dot(ys_ref[...].astype(BF16), wos_ref[...]) + _dot(ym_ref[...].astype(BF16), wom_ref[...])
    h = _rms(x, nw_ref[...]).astype(BF16)
    g = _dot(h, wg_ref[...])
    u = _dot(h, wu_ref[...])
    a = (_silu(g) * u).astype(BF16)
    x = x + FFN_RES * _dot(a, wd_ref[...])
    if has_final:
        x = _rms(x, fw_ref[...])
    if out_layout == "tb":
        ts = o_ref.shape[0]
        pitch = ts + 1
        for j in range(scr.shape[0]):
            for b in range(SUBLANES):
                scr[j, b * pitch:b * pitch + ts, :] = x[b * ts:(b + 1) * ts, j * LANES:(j + 1) * LANES]
        for t in range(ts):
            for j in range(scr.shape[0]):
                o_ref[t, :, j * LANES:(j + 1) * LANES] = scr[j, pl.ds(t, SUBLANES, stride=pitch), :]
    elif out_layout == "bt":
        o_ref[...] = x.reshape(o_ref.shape)
    else:
        o_ref[...] = x


def _row_spec(layout, tm, n_inner):
    if layout == "bs":
        return pl.BlockSpec((tm, D_MODEL), lambda b, i: (b * n_inner + i, 0))
    ts = tm // SUBLANES
    if layout == "bt":
        return pl.BlockSpec((None, SUBLANES, ts, D_MODEL), lambda b, i: (b, 0, i, 0))
    return pl.BlockSpec((ts, None, SUBLANES, D_MODEL), lambda b, i: (i, b, 0, 0))


def _ffn(x, nw, wg, wu, wd, layer, *, batch, seq, in_layout="bs", out_layout="bs", tm=512, pre=None,
         final_w=None):
    if in_layout == "bs":
        grid = (batch, seq // tm)
        n_inner = seq // tm
    else:
        grid = (batch // SUBLANES, seq * SUBLANES // tm)
        n_inner = None
    args = [x]
    specs = [_row_spec(in_layout, tm, n_inner)]
    if pre is not None:
        ys, ym, wos, wom = pre
        args += [ys, ym, wos, wom]
        specs += [_row_spec("bs", tm, n_inner), _row_spec("bs", tm, n_inner),
                  _const_spec(wos.shape), _const_spec(wom.shape)]
    args += [nw, wg, wu, wd]
    specs += [_const_spec(nw.shape), _layer_spec(wg.shape, layer), _layer_spec(wu.shape, layer),
              _layer_spec(wd.shape, layer)]
    if final_w is not None:
        args.append(final_w)
        specs.append(_const_spec(final_w.shape))
    out_shape = {"bs": (batch * seq, D_MODEL),
                 "bt": (batch // SUBLANES, SUBLANES, seq, D_MODEL),
                 "tb": (seq, batch // SUBLANES, SUBLANES, D_MODEL)}[out_layout]
    scratch = ([pltpu.VMEM((D_MODEL // LANES, tm + SUBLANES, LANES), F32)]
               if "tb" in (in_layout, out_layout) else [])
    return pl.pallas_call(
        functools.partial(_ffn_kernel, has_pre=pre is not None, has_final=final_w is not None,
                          in_layout=in_layout, out_layout=out_layout),
        grid=grid,
        in_specs=specs,
        out_specs=_row_spec(out_layout, tm, n_inner),
        out_shape=jax.ShapeDtypeStruct(out_shape, F32),
        scratch_shapes=scratch,
        compiler_params=_cparams(("parallel", "parallel")),
        name="ffn",
    )(*args)


def _inproj_kernel(x_ref, nw_ref, wm_ref, wdt_ref, wdtT_ref, z_ref, xbc_ref, mx_ref, mz_ref, dt_ref, dtT_ref):
    u = _rms(x_ref[...], nw_ref[...]).astype(BF16)
    proj = _dot(u, wm_ref[...])
    o1 = SSD_D_INNER
    o2 = o1 + SSD_CONV_CH
    o3 = o2 + ML_D_INNER
    z_ref[...] = proj[:, :o1]
    xbc_ref[...] = proj[:, o1:o2]
    mx_ref[...] = proj[:, o2:o3]
    mz_ref[...] = proj[:, o3:]
    dt_ref[...] = _dot(u, wdt_ref[...])
    dtT_ref[...] = _dot_nt(wdtT_ref[...], u)


def _inproj(x, nw, w_main, w_dt, w_dtT, *, tm=512):
    rows = x.shape[0]
    n = rows // tm
    row = lambda w: pl.BlockSpec((tm, w), lambda i: (i, 0))
    return pl.pallas_call(
        _inproj_kernel,
        grid=(n,),
        in_specs=[row(D_MODEL), _const_spec(nw.shape), _const_spec(w_main.shape),
                  _const_spec(w_dt.shape), _const_spec(w_dtT.shape)],
        out_specs=[row(SSD_D_INNER), row(SSD_CONV_CH), row(ML_D_INNER), row(ML_D_INNER), row(LANES),
                   pl.BlockSpec((SSD_HEADS, tm), lambda i: (0, i))],
        out_shape=[jax.ShapeDtypeStruct((rows, SSD_D_INNER), F32),
                   jax.ShapeDtypeStruct((rows, SSD_CONV_CH), F32),
                   jax.ShapeDtypeStruct((rows, ML_D_INNER), F32),
                   jax.ShapeDtypeStruct((rows, ML_D_INNER), F32),
                   jax.ShapeDtypeStruct((rows, LANES), F32),
                   jax.ShapeDtypeStruct((SSD_HEADS, rows), F32)],
        compiler_params=_cparams(("parallel",)),
        name="inproj",
    )(x, nw, w_main, w_dt, w_dtT)


def _causal_conv_silu(cbuf, x_ref, w_ref, b_ref):
    L = x_ref.shape[0]
    cbuf[CONV_HALO:CONV_HALO + L, :] = x_ref[...]
    ext = cbuf[...]
    prev = pltpu.roll(ext, 1, axis=0)
    near = ext * w_ref[3:4, :] + prev * w_ref[2:3, :]
    far = ext * w_ref[1:2, :] + prev * w_ref[0:1, :]
    acc = b_ref[...] + near[CONV_HALO:, :] + pltpu.roll(far, 2, axis=0)[CONV_HALO:, :]
    cbuf[0:CONV_HALO, :] = ext[L:L + CONV_HALO, :]
    return _silu(acc)


def _ssd_kernel(*refs):
    xbc_ref, dt_ref = refs[:2]
    dtT_refs = refs[2:2 + SEQS_PER_STEP]
    z_ref = refs[2 + SEQS_PER_STEP]
    consts = refs[3 + SEQS_PER_STEP:-3]
    o_ref, cbuf, st_ref = refs[-3:]

    @pl.when(pl.program_id(1) == 0)
    def _():
        st_ref[...] = jnp.zeros(st_ref.shape, F32)
        cbuf[:, 0:CONV_HALO, :] = jnp.zeros((SEQS_PER_STEP, CONV_HALO, cbuf.shape[2]), F32)

    for s in range(SEQS_PER_STEP):
        _ssd_chunk(xbc_ref.at[s], dt_ref.at[s], dtT_refs[s], z_ref.at[s], *consts,
                   o_ref.at[s], cbuf.at[s], st_ref.at[s])


def _ssd_chunk(xbc_ref, dt_ref, dtT_ref, z_ref, cw_ref, cb_ref, dtb_ref, dtbT_ref, alog_ref, alogT_ref,
               dexp_ref, nw_ref, o_ref, cbuf, st_ref):
    L = CHUNK
    xbc = _causal_conv_silu(cbuf, xbc_ref, cw_ref, cb_ref)
    xs = xbc[:, :SSD_D_INNER]
    bm = xbc[:, SSD_D_INNER:SSD_D_INNER + SSD_BC]
    cm = xbc[:, SSD_D_INNER + SSD_BC:]

    dt = _softplus(dt_ref[...] + dtb_ref[...])
    adt = dt * (-jnp.exp(alog_ref[...]))
    dtT = _softplus(dtT_ref[...] + dtbT_ref[...])
    adtT = dtT * (-jnp.exp(alogT_ref[...]))
    tril = _tri(L, True)
    a_cs = _dot_f32(tril.astype(F32), adt)
    a_csT = _dot_f32(adtT, _tri(L, False).astype(F32))
    w_st = dt * jnp.exp(a_cs[L - 1:L, :] - a_cs)
    left = lax.broadcasted_iota(jnp.int32, (L, LANES), 1) < SSD_HEAD_DIM

    def col(a, h):
        return jnp.broadcast_to(a[:, h:h + 1], (L, LANES))

    y_parts = []
    heads_per_group = SSD_HEADS // SSD_GROUPS
    for g in range(SSD_GROUPS):
        gs = slice(g * SSD_GROUP_W, (g + 1) * SSD_GROUP_W)
        bm_g = bm[:, g * SSD_D_STATE:(g + 1) * SSD_D_STATE]
        cm_g = cm[:, g * SSD_D_STATE:(g + 1) * SSD_D_STATE].astype(BF16)
        cb = _dot_nt(cm_g, bm_g.astype(BF16))
        st_prev = st_ref[:, gs]
        y_off = _dot(cm_g, st_prev.astype(BF16))
        diag, acs_pairs, xdec_pairs = [], [], []
        for hp in range(heads_per_group // 2):
            h0 = g * heads_per_group + 2 * hp
            cols = [col(a_cs, h0), col(a_cs, h0 + 1)]
            ms = []
            for h, acs_col in zip((h0, h0 + 1), cols):
                lm = jnp.exp(jnp.where(tril, acs_col - a_csT[h:h + 1, :], -jnp.inf))
                ms.append((cb * lm * dtT[h:h + 1, :]).astype(BF16))
            slab = xs[:, h0 * SSD_HEAD_DIM:(h0 + 2) * SSD_HEAD_DIM]
            rhs = jnp.concatenate([jnp.where(left, slab, 0.0), jnp.where(left, 0.0, slab)], axis=0)
            diag.append(_dot(jnp.concatenate(ms, axis=1), rhs.astype(BF16)))
            acs_pairs.append(jnp.where(left, cols[0], cols[1]))
            xdec_pairs.append(slab * jnp.where(left, col(w_st, h0), col(w_st, h0 + 1)))
        acs_e = jnp.concatenate(acs_pairs, axis=1)
        y_parts.append(jnp.concatenate(diag, axis=1) + y_off * jnp.exp(acs_e))
        xd_dec = jnp.concatenate(xdec_pairs, axis=1).astype(BF16)
        st_ref[:, gs] = st_prev * jnp.exp(acs_e[L - 1:L, :]) + _dot(bm_g.T.astype(BF16), xd_dec)

    y = jnp.concatenate(y_parts, axis=1) + dexp_ref[...] * xs
    yg = y * _silu(z_ref[...])
    outs = []
    for g in range(SSD_GROUPS):
        v = yg[:, g * SSD_GROUP_W:(g + 1) * SSD_GROUP_W]
        outs.append(v * lax.rsqrt(jnp.mean(v * v, axis=-1, keepdims=True) + EPS))
    o_ref[...] = jnp.concatenate(outs, axis=1) * nw_ref[...]


def _ssd(xbc, dt, dtT, z, cw, cb, dtb, dtbT, alog, alogT, dexp, nw, *, batch, seq):
    L = CHUNK
    nc = seq // L
    n = SEQS_PER_STEP
    seqs = lambda t: t.reshape(batch // n, n, seq, t.shape[-1])
    row = lambda w: pl.BlockSpec((None, n, L, w), lambda b, c: (b, 0, c, 0))
    dtT_spec = lambda s: pl.BlockSpec((SSD_HEADS, L), lambda b, c: (0, (b * n + s) * nc + c))
    consts = [cw, cb, dtb, dtbT, alog, alogT, dexp, nw]
    out = pl.pallas_call(
        _ssd_kernel,
        grid=(batch // n, nc),
        in_specs=[row(SSD_CONV_CH), row(LANES)] + [dtT_spec(s) for s in range(n)] + [row(SSD_D_INNER)]
                 + [_const_spec(a.shape) for a in consts],
        out_specs=row(SSD_D_INNER),
        out_shape=jax.ShapeDtypeStruct((batch // n, n, seq, SSD_D_INNER), F32),
        scratch_shapes=[pltpu.VMEM((n, CONV_HALO + L, SSD_CONV_CH), F32),
                        pltpu.VMEM((n, SSD_D_STATE, SSD_D_INNER), F32)],
        compiler_params=_cparams(("parallel", "arbitrary")),
        name="ssd",
    )(seqs(xbc), seqs(dt), *([dtT] * n), seqs(z), *consts)
    return out.reshape(batch * seq, SSD_D_INNER)


def _mlstm_kernel(*refs):
    mx_ref, mz_ref = refs[:2]
    cw_ref, cb_ref, wq_ref, wk_ref, wv_ref, wif_ref, bif_ref, nw_ref, skip_ref = refs[2:-5]
    o_ref, cbuf, ct_ref, n_ref, m_ref = refs[-5:]
    seqs = range(SEQS_PER_STEP)

    @pl.when(pl.program_id(1) == 0)
    def _():
        ct_ref[...] = jnp.zeros(ct_ref.shape, F32)
        n_ref[...] = jnp.zeros(n_ref.shape, F32)
        m_ref[...] = jnp.full(m_ref.shape, -1e30, F32)
        cbuf[:, 0:CONV_HALO, :] = jnp.zeros((SEQS_PER_STEP, CONV_HALO, cbuf.shape[2]), F32)

    pre = [_mlstm_project(mx_ref.at[s], cbuf.at[s], cw_ref, cb_ref, wq_ref, wk_ref, wv_ref, wif_ref, bif_ref)
           for s in seqs]
    heads = [[] for _ in seqs]
    for h in range(ML_HEADS):
        for s in seqs:
            heads[s].append(_mlstm_head(h, pre[s], ct_ref.at[s, h], n_ref.at[s, h], m_ref.at[s, h]))
    for s in seqs:
        hm = jnp.concatenate(heads[s], axis=1) * nw_ref[...]
        o_ref[s] = (hm + skip_ref[...] * pre[s]["xc"]) * _silu(mz_ref[s])


def _mlstm_project(mx_ref, cbuf, cw_ref, cb_ref, wq_ref, wk_ref, wv_ref, wif_ref, bif_ref):
    L = CHUNK
    mx = mx_ref[...]
    xc = _causal_conv_silu(cbuf, mx_ref, cw_ref, cb_ref)
    xc_b = xc.astype(BF16)
    mx_b = mx.astype(BF16)
    tile = 2 * LANES
    q = jnp.concatenate([_dot(xc_b[:, t * tile:(t + 1) * tile], wq_ref[t]) for t in range(ML_HEADS)], axis=1)
    k = jnp.concatenate([_dot(xc_b[:, t * tile:(t + 1) * tile], wk_ref[t]) for t in range(ML_HEADS)], axis=1)
    v = jnp.concatenate([_dot(mx_b[:, t * tile:(t + 1) * tile], wv_ref[t]) for t in range(ML_HEADS)], axis=1)
    qkv = jnp.concatenate([q, k, v], axis=1).astype(BF16)
    gates = _dot(qkv, wif_ref[...]) + bif_ref[...]
    gatesT = gates.T[0:2 * ML_HEADS, :]
    logf = -_softplus(-gates)
    logfT = -_softplus(-gatesT)
    bcum = _dot_f32(_tri(L, True).astype(F32), logf)
    bcumT = _dot_f32(logfT, _tri(L, False).astype(F32))
    return dict(xc=xc, qkv=qkv, q=q, k=k, gates=gates, gatesT=gatesT, bcum=bcum, bcumT=bcumT)


def _mlstm_head(h, pre, ct_ref, n_ref, m_ref):
    L = CHUNK
    tril = _tri(L, True)
    qkv, gates, gatesT, bcum, bcumT = pre["qkv"], pre["gates"], pre["gatesT"], pre["bcum"], pre["bcumT"]
    hs = slice(h * ML_HEAD_DIM, (h + 1) * ML_HEAD_DIM)
    qh = qkv[:, hs]
    kh = qkv[:, ML_D_INNER + hs.start:ML_D_INNER + hs.stop]
    vh = qkv[:, 2 * ML_D_INNER + hs.start:2 * ML_D_INNER + hs.stop]
    bc = jnp.broadcast_to(bcum[:, ML_HEADS + h:ML_HEADS + h + 1], (L, LANES))
    ig = jnp.broadcast_to(gates[:, h:h + 1], (L, LANES))
    m_prev = m_ref[0:1, :]
    n_prev = n_ref[0:1, :]
    ct_prev = ct_ref[...]

    dlog = jnp.where(tril, bc - bcumT[ML_HEADS + h:ML_HEADS + h + 1, :] + gatesT[h:h + 1, :], -jnp.inf)
    m_inter = bc + m_prev
    m_t = jnp.maximum(jnp.max(dlog, axis=-1, keepdims=True), m_inter)
    scores = _dot_nt(qh, kh) * jnp.exp(dlog - m_t)
    inter_w = jnp.exp(m_inter - m_t)
    num = _dot(scores.astype(BF16), vh) + _rep(inter_w, 2) * _dot(qh, ct_prev.astype(BF16))
    den = (jnp.sum(scores, axis=-1, keepdims=True)
           + inter_w * jnp.sum(pre["q"][:, hs] * n_prev, axis=-1, keepdims=True))
    hout = num / _rep(jnp.maximum(jnp.abs(den), jnp.exp(-m_t)), 2)

    b_last = bc[L - 1:L, :]
    w_state = b_last - bc + ig
    m_new = jnp.maximum(b_last + m_prev, jnp.max(w_state, axis=0, keepdims=True))
    decay = _rep(jnp.exp(b_last + m_prev - m_new), 2)
    kw = pre["k"][:, hs] * _rep(jnp.exp(w_state - m_new), 2)
    ct_ref[...] = decay * ct_prev + _dot(kw.T.astype(BF16), vh)
    n_ref[...] = jnp.broadcast_to(decay * n_prev + jnp.sum(kw, axis=0, keepdims=True), n_ref.shape)
    m_ref[...] = jnp.broadcast_to(m_new, m_ref.shape)

    mu = jnp.mean(hout, axis=-1, keepdims=True)
    d = hout - mu
    var = jnp.mean(d * d, axis=-1, keepdims=True)
    return d * lax.rsqrt(var + EPS)


def _mlstm(mx, mz, cw, cb, wq, wk, wv, wif, bif, nw, skip, *, batch, seq):
    L = CHUNK
    nc = seq // L
    n = SEQS_PER_STEP
    seqs = lambda t: t.reshape(batch // n, n, seq, t.shape[-1])
    row = lambda w: pl.BlockSpec((None, n, L, w), lambda b, c: (b, 0, c, 0))
    consts = [cw, cb, wq, wk, wv, wif, bif, nw, skip]
    out = pl.pallas_call(
        _mlstm_kernel,
        grid=(batch // n, nc),
        in_specs=[row(ML_D_INNER), row(ML_D_INNER)] + [_const_spec(a.shape) for a in consts],
        out_specs=row(ML_D_INNER),
        out_shape=jax.ShapeDtypeStruct((batch // n, n, seq, ML_D_INNER), F32),
        scratch_shapes=[pltpu.VMEM((n, CONV_HALO + L, ML_D_INNER), F32),
                        pltpu.VMEM((n, ML_HEADS, ML_HEAD_DIM, ML_HEAD_DIM), F32),
                        pltpu.VMEM((n, ML_HEADS, SUBLANES, ML_HEAD_DIM), F32),
                        pltpu.VMEM((n, ML_HEADS, SUBLANES, LANES), F32)],
        compiler_params=_cparams(("parallel", "arbitrary")),
        name="mlstm",
    )(seqs(mx), seqs(mz), *consts)
    return out.reshape(batch * seq, ML_D_INNER)


def _s5_param_kernel(are_ref, aim_ref, lstep_ref, bre_ref, bim_ref, lre_ref, lim_ref, bbre_ref, bbim_ref):
    a_re, a_im = are_ref[...], aim_ref[...]
    step = jnp.exp(lstep_ref[...])
    mag = jnp.exp(a_re * step)
    lam_re = mag * jnp.cos(a_im * step)
    lam_im = mag * jnp.sin(a_im * step)
    den = a_re * a_re + a_im * a_im
    coef_re = ((lam_re - 1.0) * a_re + lam_im * a_im) / den
    coef_im = (lam_im * a_re - (lam_re - 1.0) * a_im) / den
    b_re, b_im = bre_ref[...], bim_ref[...]
    lre_ref[...] = lam_re
    lim_ref[...] = lam_im
    bbre_ref[...] = coef_re * b_re - coef_im * b_im
    bbim_ref[...] = coef_re * b_im + coef_im * b_re


def _s5_params(a_re, a_im, log_step, b_re, b_im):
    rep = lambda t: jnp.repeat(t, S5_GROUP, axis=0)
    are, aim = rep(a_re), rep(a_im)
    lstep = jnp.broadcast_to(rep(log_step[:, None]), are.shape)
    breT = jnp.swapaxes(b_re, 1, 2).reshape(D_MODEL, S5_STATE)
    bimT = jnp.swapaxes(b_im, 1, 2).reshape(D_MODEL, S5_STATE)
    shp = jax.ShapeDtypeStruct((D_MODEL, S5_STATE), F32)
    return pl.pallas_call(_s5_param_kernel, out_shape=[shp, shp, shp, shp], name="s5_params")(
        are, aim, lstep, breT, bimT)


def _s5_kernel(x_ref, nw_ref, bs_ref, lam_ref, cs_ref, dsk_ref, wa_ref, ba_ref, wb_ref, bb_ref, o_ref,
               bu_ref, xs_ref, st_ref, *, batch):
    rows = x_ref.shape[0]
    steps = rows // batch

    @pl.when(pl.program_id(0) == 0)
    def _():
        st_ref[...] = jnp.zeros(st_ref.shape, F32)

    x = x_ref[...]
    u = _rms(x, nw_ref[...])
    u_b = u.astype(BF16)
    def bu_slab(k):
        bu_ref[k % 2] = _dot(u_b[:, k * S5_SLAB_IN:(k + 1) * S5_SLAB_IN], bs_ref[k])

    ys = []
    bu_slab(0)
    for k in range(S5_SLABS):
        if k + 1 < S5_SLABS:
            bu_slab(k + 1)
        for w in range(S5_SLAB_ST // S5_SCAN_W):
            re_l = slice(w * S5_SCAN_W, (w + 1) * S5_SCAN_W)
            im_l = slice(S5_SLAB_ST + w * S5_SCAN_W, S5_SLAB_ST + (w + 1) * S5_SCAN_W)
            lr = jnp.broadcast_to(lam_ref[k, 0:1, re_l], (batch, S5_SCAN_W))
            li = jnp.broadcast_to(lam_ref[k, 0:1, im_l], (batch, S5_SCAN_W))
            xr, xi = st_ref[k, :, re_l], st_ref[k, :, im_l]
            for t in range(steps):
                rows_t = slice(t * batch, (t + 1) * batch)
                xr, xi = (lr * xr - li * xi + bu_ref[k % 2, rows_t, re_l],
                          lr * xi + li * xr + bu_ref[k % 2, rows_t, im_l])
                xs_ref[k % 2, rows_t, re_l] = xr.astype(BF16)
                xs_ref[k % 2, rows_t, im_l] = xi.astype(BF16)
            st_ref[k, :, re_l] = xr
            st_ref[k, :, im_l] = xi
        ys.append(_dot(xs_ref[k % 2], cs_ref[k]))
    y = jnp.concatenate(ys, axis=1) + dsk_ref[...] * u
    g = jax.nn.gelu(y).astype(BF16)
    out = (_dot(g, wa_ref[...]) + ba_ref[...]) * _sigmoid(_dot(g, wb_ref[...]) + bb_ref[...])
    o_ref[...] = x + out


def _s5(x_sb, nw, b_slabs, lam_slabs, c_slabs, dsk, wa, ba, wb, bb, *, batch, seq, ts=32):
    rows = ts * batch
    consts = [nw, b_slabs, lam_slabs, c_slabs, dsk, wa, ba, wb, bb]
    return pl.pallas_call(
        functools.partial(_s5_kernel, batch=batch),
        grid=(seq // ts,),
        in_specs=[pl.BlockSpec((rows, D_MODEL), lambda i: (i, 0))] + [_const_spec(a.shape) for a in consts],
        out_specs=pl.BlockSpec((rows, D_MODEL), lambda i: (i, 0)),
        out_shape=jax.ShapeDtypeStruct((seq * batch, D_MODEL), F32),
        scratch_shapes=[pltpu.VMEM((2, rows, 2 * S5_SLAB_ST), F32),
                        pltpu.VMEM((2, rows, 2 * S5_SLAB_ST), BF16),
                        pltpu.VMEM((S5_SLABS, batch, 2 * S5_SLAB_ST), F32)],
        compiler_params=_cparams(("arbitrary",)),
        name="s5",
    )(x_sb, *consts)


def _s5_slab_weights(lam_re, lam_im, bb_re, bb_im, c_re, c_im):
    eye = jnp.eye(S5_SLAB_GROUPS, dtype=F32)

    def in_slab(bb):
        t = bb.reshape(S5_SLABS, S5_SLAB_GROUPS, S5_GROUP, S5_STATE)
        t = t[:, :, :, None, :] * eye[None, :, None, :, None]
        return t.reshape(S5_SLABS, S5_SLAB_IN, S5_SLAB_ST)

    def out_slab(cc):
        t = jnp.swapaxes(cc, 1, 2).reshape(S5_SLABS, S5_SLAB_GROUPS, S5_STATE, S5_GROUP)
        t = t[:, :, :, None, :] * eye[None, :, None, :, None]
        return t.reshape(S5_SLABS, S5_SLAB_ST, S5_SLAB_IN)

    b_slabs = jnp.concatenate([in_slab(bb_re), in_slab(bb_im)], axis=2).astype(BF16)
    c_slabs = jnp.concatenate([out_slab(c_re), out_slab(-c_im)], axis=1).astype(BF16)
    lam = lambda t: t[::S5_GROUP].reshape(S5_SLABS, 1, S5_SLAB_ST)
    lam_slabs = jnp.concatenate([lam(lam_re), lam(lam_im)], axis=2)
    return b_slabs, jnp.broadcast_to(lam_slabs, (S5_SLABS, 8, 2 * S5_SLAB_ST)), c_slabs


def _blockdiag_tiles(w):
    nb = w.shape[0]
    tile = nb * ML_QKV_BLOCK // ML_HEADS
    rows = jnp.swapaxes(w, 1, 2).reshape(nb * ML_QKV_BLOCK, ML_QKV_BLOCK)
    sel = (jnp.arange(tile)[None, :] % ML_QKV_BLOCK == jnp.arange(ML_QKV_BLOCK)[:, None]).astype(w.dtype)
    wide = jnp.dot(rows, sel, precision=HIGHEST)
    blk_r = (jnp.arange(nb * ML_QKV_BLOCK) % tile) // ML_QKV_BLOCK
    blk_c = jnp.arange(tile) // ML_QKV_BLOCK
    wide = jnp.where(blk_r[:, None] == blk_c[None, :], wide, 0.0)
    return wide.reshape(ML_HEADS, tile, tile)


def _pad_lanes(t, n=LANES):
    return jnp.pad(t, ((0, 0), (0, n - t.shape[1])))


def kernel(x, ffn1_norm, ffn1_w_gate, ffn1_w_up, ffn1_w_down, mix_norm, ffn2_norm, ffn2_w_gate, ffn2_w_up, ffn2_w_down, hy_w_in, ssd_conv_w, ssd_conv_b, ssd_dt_bias, ssd_a_log, ssd_d, ssd_norm_w, ml_conv_w, ml_conv_b, ml_w_q, ml_w_k, ml_w_v, ml_w_if, ml_b_if, ml_norm_w, ml_skip, hy_w_out, s5_a_re, s5_a_im, s5_log_step, s5_b_re, s5_b_im, s5_c_re, s5_c_im, s5_d, s5_w_a, s5_b_a, s5_w_b, s5_b_b, final_norm):
    batch, seq, _ = x.shape
    assert seq % 512 == 0 and batch % 8 == 0
    row = lambda t: t.reshape(1, -1)
    bf = lambda t: t.astype(BF16)
    xf = x.reshape(batch * seq, D_MODEL)

    g1, u1, d1, g2, u2, d2 = _to_bf16([ffn1_w_gate, ffn1_w_up, ffn1_w_down, ffn2_w_gate, ffn2_w_up, ffn2_w_down])
    x1 = _ffn(xf, row(ffn1_norm[0]), g1, u1, d1, 0, batch=batch, seq=seq)

    w_in = hy_w_in[0]
    o1 = SSD_D_INNER
    o2 = o1 + SSD_CONV_CH
    o3 = o2 + SSD_HEADS
    w_main = bf(jnp.concatenate([w_in[:, :o2], w_in[:, o3:]], axis=1))
    w_dt = w_in[:, o2:o3]
    z_s, xbc, m_x, m_z, dt_raw, dt_rawT = _inproj(x1, row(mix_norm[0]), w_main, bf(_pad_lanes(w_dt)),
                                                   bf(w_dt.T))

    y_ssd = _ssd(xbc, dt_raw, dt_rawT, z_s, ssd_conv_w[0], row(ssd_conv_b[0]),
                 _pad_lanes(row(ssd_dt_bias[0])), ssd_dt_bias[0].reshape(-1, 1),
                 _pad_lanes(row(ssd_a_log[0])), ssd_a_log[0].reshape(-1, 1),
                 row(jnp.repeat(ssd_d[0], SSD_HEAD_DIM)), row(ssd_norm_w[0]), batch=batch, seq=seq)

    k_scale = 1.0 / math.sqrt(ML_HEAD_DIM)
    w_if = ml_w_if[0]
    w_if = jnp.concatenate([w_if[:ML_D_INNER], w_if[ML_D_INNER:2 * ML_D_INNER] / k_scale, w_if[2 * ML_D_INNER:]],
                           axis=0)
    b_if = ml_b_if[0]
    y_ml = _mlstm(m_x, m_z, ml_conv_w[0], row(ml_conv_b[0]),
                  bf(_blockdiag_tiles(ml_w_q[0])), bf(_blockdiag_tiles(ml_w_k[0]) * k_scale),
                  bf(_blockdiag_tiles(ml_w_v[0])), bf(_pad_lanes(w_if)),
                  _pad_lanes(row(b_if)), row(ml_norm_w[0]), row(ml_skip[0]),
                  batch=batch, seq=seq)

    w_out = bf(hy_w_out[0])
    x3 = _ffn(x1, row(ffn2_norm[0]), g2, u2, d2, 0, batch=batch, seq=seq,
              pre=(y_ssd, y_ml, w_out[:SSD_D_INNER], w_out[SSD_D_INNER:]))

    x4 = _ffn(x3.reshape(batch // SUBLANES, SUBLANES, seq, D_MODEL), row(ffn1_norm[1]), g1, u1, d1, 1,
              batch=batch, seq=seq, in_layout="bt", out_layout="tb")
    lam_re, lam_im, bb_re, bb_im = _s5_params(s5_a_re[0], s5_a_im[0], s5_log_step[0], s5_b_re[0], s5_b_im[0])
    b_slabs, lam_slabs, c_slabs = _s5_slab_weights(lam_re, lam_im, bb_re, bb_im, s5_c_re[0], s5_c_im[0])
    x5 = _s5(x4.reshape(seq * batch, D_MODEL), row(mix_norm[1]), b_slabs, lam_slabs, c_slabs, row(s5_d[0]),
             bf(s5_w_a[0]), row(s5_b_a[0]), bf(s5_w_b[0]), row(s5_b_b[0]), batch=batch, seq=seq)
    out = _ffn(x5.reshape(seq, batch // SUBLANES, SUBLANES, D_MODEL), row(ffn2_norm[1]), g2, u2, d2, 1,
               batch=batch, seq=seq, in_layout="tb", out_layout="bt", final_w=row(final_norm))
    return out.reshape(batch, seq, D_MODEL)
```

```python
import functools
import math

import jax
import jax.numpy as jnp
from jax import lax
from jax.experimental import pallas as pl
from jax.experimental.pallas import tpu as pltpu

F32 = jnp.float32
BF16 = jnp.bfloat16
HIGHEST = lax.Precision.HIGHEST

D_MODEL = 1024
EPS = 1e-6
D_FF = 2816
FFN_RES = 0.5
CONV_W = 4
CONV_HALO = 8

SSD_HEADS = 16
SSD_HEAD_DIM = 64
SSD_GROUPS = 2
SSD_D_STATE = 128
SSD_D_INNER = SSD_HEADS * SSD_HEAD_DIM
SSD_BC = SSD_GROUPS * SSD_D_STATE
SSD_CONV_CH = SSD_D_INNER + 2 * SSD_BC
SSD_GROUP_W = SSD_D_INNER // SSD_GROUPS

ML_HEADS = 4
ML_HEAD_DIM = 256
ML_D_INNER = ML_HEADS * ML_HEAD_DIM
ML_QKV_BLOCK = 4

CHUNK = 128
LANES = 128
SUBLANES = 8
SEQS_PER_STEP = 2

S5_GROUP = 16
S5_GROUPS = D_MODEL // S5_GROUP
S5_STATE = 64
S5_SLAB_GROUPS = 16
S5_SLABS = S5_GROUPS // S5_SLAB_GROUPS
S5_SLAB_IN = S5_SLAB_GROUPS * S5_GROUP
S5_SLAB_ST = S5_SLAB_GROUPS * S5_STATE
S5_SCAN_W = 512

VMEM_LIMIT_BYTES = 56 * 1024 * 1024


def _cparams(sem):
    return pltpu.CompilerParams(dimension_semantics=sem, vmem_limit_bytes=VMEM_LIMIT_BYTES)


def _dot(a, b):
    return jnp.dot(a, b, preferred_element_type=F32)


def _dot_f32(a, b):
    return jnp.dot(a, b, preferred_element_type=F32, precision=HIGHEST)


def _dot_nt(a, b):
    return lax.dot_general(a, b, (((1,), (1,)), ((), ())), preferred_element_type=F32)


def _rms(x, w):
    return x * lax.rsqrt(jnp.mean(x * x, axis=-1, keepdims=True) + EPS) * w


def _sigmoid(x):
    return 1.0 / (1.0 + jnp.exp(-x))


def _silu(x):
    return x * _sigmoid(x)


def _softplus(x):
    return jnp.maximum(x, 0.0) + jnp.log1p(jnp.exp(-jnp.abs(x)))


def _rep(x, n):
    return jnp.concatenate([x] * n, axis=-1)


def _tri(n, lower):
    r = lax.broadcasted_iota(jnp.int32, (n, n), 0)
    c = lax.broadcasted_iota(jnp.int32, (n, n), 1)
    return r >= c if lower else r <= c


def _const_spec(shape):
    nd = len(shape)
    return pl.BlockSpec(shape, lambda *_: (0,) * nd, pipeline_mode=pl.Buffered(1))


def _layer_spec(shape, layer):
    return pl.BlockSpec((None,) + tuple(shape[1:]), lambda *_: (layer, 0, 0), pipeline_mode=pl.Buffered(1))


def _to_bf16_kernel(*refs):
    n = len(refs) // 2
    for src, dst in zip(refs[:n], refs[n:]):
        dst[...] = src[...].astype(BF16)


def _to_bf16(weights, blocks=8):
    specs = [pl.BlockSpec((1, w.shape[1] // blocks, w.shape[2]), lambda l, i: (l, i, 0)) for w in weights]
    return pl.pallas_call(
        _to_bf16_kernel,
        grid=(weights[0].shape[0], blocks),
        in_specs=specs,
        out_specs=specs,
        out_shape=[jax.ShapeDtypeStruct(w.shape, BF16) for w in weights],
        compiler_params=_cparams(("parallel", "parallel")),
        name="to_bf16",
    )(*weights)


def _ffn_kernel(*refs, has_pre, has_final, in_layout, out_layout):
    it = iter(refs)
    x_ref = next(it)
    if has_pre:
        ys_ref, ym_ref, wos_ref, wom_ref = next(it), next(it), next(it), next(it)
    nw_ref, wg_ref, wu_ref, wd_ref = next(it), next(it), next(it), next(it)
    if has_final:
        fw_ref = next(it)
    o_ref = next(it)
    scr = next(it, None)

    if in_layout == "tb":
        ts = x_ref.shape[0]
        pitch = ts + 1
        for t in range(ts):
            for j in range(scr.shape[0]):
                scr[j, pl.ds(t, SUBLANES, stride=pitch), :] = x_ref[t, :, j * LANES:(j + 1) * LANES]
        x = jnp.concatenate(
            [jnp.concatenate([scr[j, b * pitch:b * pitch + ts, :] for b in range(SUBLANES)], axis=0)
             for j in range(scr.shape[0])], axis=1)
    elif in_layout == "bt":
        x = x_ref[...].reshape(-1, D_MODEL)
    else:
        x = x_ref[...]
    if has_pre:
        x = x + _dot(ys_ref[...].astype(BF16), wos_ref[...]) + _dot(ym_ref[...].astype(BF16), wom_ref[...])
    h = _rms(x, nw_ref[...]).astype(BF16)
    g = _dot(h, wg_ref[...])
    u = _dot(h, wu_ref[...])
    a = (_silu(g) * u).astype(BF16)
    x = x + FFN_RES * _dot(a, wd_ref[...])
    if has_final:
        x = _rms(x, fw_ref[...])
    if out_layout == "tb":
        ts = o_ref.shape[0]
        pitch = ts + 1
        for j in range(scr.shape[0]):
            for b in range(SUBLANES):
                scr[j, b * pitch:b * pitch + ts, :] = x[b * ts:(b + 1) * ts, j * LANES:(j + 1) * LANES]
        for t in range(ts):
            for j in range(scr.shape[0]):
                o_ref[t, :, j * LANES:(j + 1) * LANES] = scr[j, pl.ds(t, SUBLANES, stride=pitch), :]
    elif out_layout == "bt":
        o_ref[...] = x.reshape(o_ref.shape)
    else:
        o_ref[...] = x


def _row_spec(layout, tm, n_inner):
    if layout == "bs":
        return pl.BlockSpec((tm, D_MODEL), lambda b, i: (b * n_inner + i, 0))
    ts = tm // SUBLANES
    if layout == "bt":
        return pl.BlockSpec((None, SUBLANES, ts, D_MODEL), lambda b, i: (b, 0, i, 0))
    return pl.BlockSpec((ts, None, SUBLANES, D_MODEL), lambda b, i: (i, b, 0, 0))


def _ffn(x, nw, wg, wu, wd, layer, *, batch, seq, in_layout="bs", out_layout="bs", tm=512, pre=None,
         final_w=None):
    if in_layout == "bs":
        grid = (batch, seq // tm)
        n_inner = seq // tm
    else:
        grid = (batch // SUBLANES, seq * SUBLANES // tm)
        n_inner = None
    args = [x]
    specs = [_row_spec(in_layout, tm, n_inner)]
    if pre is not None:
        ys, ym, wos, wom = pre
        args += [ys, ym, wos, wom]
        specs += [_row_spec("bs", tm, n_inner), _row_spec("bs", tm, n_inner),
                  _const_spec(wos.shape), _const_spec(wom.shape)]
    args += [nw, wg, wu, wd]
    specs += [_const_spec(nw.shape), _layer_spec(wg.shape, layer), _layer_spec(wu.shape, layer),
              _layer_spec(wd.shape, layer)]
    if final_w is not None:
        args.append(final_w)
        specs.append(_const_spec(final_w.shape))
    out_shape = {"bs": (batch * seq, D_MODEL),
                 "bt": (batch // SUBLANES, SUBLANES, seq, D_MODEL),
                 "tb": (seq, batch // SUBLANES, SUBLANES, D_MODEL)}[out_layout]
    scratch = ([pltpu.VMEM((D_MODEL // LANES, tm + SUBLANES, LANES), F32)]
               if "tb" in (in_layout, out_layout) else [])
    return pl.pallas_call(
        functools.partial(_ffn_kernel, has_pre=pre is not None, has_final=final_w is not None,
                          in_layout=in_layout, out_layout=out_layout),
        grid=grid,
        in_specs=specs,
        out_specs=_row_spec(out_layout, tm, n_inner),
        out_shape=jax.ShapeDtypeStruct(out_shape, F32),
        scratch_shapes=scratch,
        compiler_params=_cparams(("parallel", "parallel")),
        name="ffn",
    )(*args)


def _inproj_kernel(x_ref, nw_ref, wm_ref, wdt_ref, wdtT_ref, z_ref, xbc_ref, mx_ref, mz_ref, dt_ref, dtT_ref):
    u = _rms(x_ref[...], nw_ref[...]).astype(BF16)
    o1 = SSD_D_INNER
    o2 = o1 + SSD_CONV_CH
    o3 = o2 + ML_D_INNER
    pz = _dot(u, wm_ref[:, :o1])
    pmid = _dot(u, wm_ref[:, o1:o3])
    pmz = _dot(u, wm_ref[:, o3:])
    z_ref[...] = _silu(pz)
    xbc_ref[...] = pmid[:, :SSD_CONV_CH]
    mx_ref[...] = pmid[:, SSD_CONV_CH:]
    mz_ref[...] = _silu(pmz)
    dt_ref[...] = _dot(u, wdt_ref[...])
    dtT_ref[...] = _dot_nt(wdtT_ref[...], u)


def _inproj(x, nw, w_main, w_dt, w_dtT, *, tm=512):
    rows = x.shape[0]
    row = lambda w: pl.BlockSpec((tm, w), lambda i: (i, 0))
    consts = [nw, w_main, w_dt, w_dtT]
    widths = [SSD_D_INNER, SSD_CONV_CH, ML_D_INNER, ML_D_INNER, LANES]
    return pl.pallas_call(
        _inproj_kernel,
        grid=(rows // tm,),
        in_specs=[row(D_MODEL)] + [_const_spec(a.shape) for a in consts],
        out_specs=[row(w) for w in widths] + [pl.BlockSpec((SSD_HEADS, tm), lambda i: (0, i))],
        out_shape=[jax.ShapeDtypeStruct((rows, w), F32) for w in widths]
                  + [jax.ShapeDtypeStruct((SSD_HEADS, rows), F32)],
        compiler_params=_cparams(("parallel",)),
        name="inproj",
    )(x, *consts)


def _causal_conv_silu(cbuf, x_ref, w_ref, b_ref):
    L = x_ref.shape[0]
    cbuf[CONV_HALO:CONV_HALO + L, :] = x_ref[...]
    ext = cbuf[...]
    prev = pltpu.roll(ext, 1, axis=0)
    near = ext * w_ref[3:4, :] + prev * w_ref[2:3, :]
    far = ext * w_ref[1:2, :] + prev * w_ref[0:1, :]
    acc = b_ref[...] + near[CONV_HALO:, :] + pltpu.roll(far, 2, axis=0)[CONV_HALO:, :]
    cbuf[0:CONV_HALO, :] = ext[L:L + CONV_HALO, :]
    return _silu(acc)


def _ssd_chunk(xbc_ref, dt_ref, dtT_ref, zact_ref, cw_ref, cb_ref, dtb_ref, dtbT_ref, alog_ref, alogT_ref,
               dexp_ref, nw_ref, o_ref, cbuf, st_ref):
    L = CHUNK
    xbc = _causal_conv_silu(cbuf, xbc_ref, cw_ref, cb_ref)
    xs = xbc[:, :SSD_D_INNER]
    bm = xbc[:, SSD_D_INNER:SSD_D_INNER + SSD_BC]
    cm = xbc[:, SSD_D_INNER + SSD_BC:]

    dt = _softplus(dt_ref[...] + dtb_ref[...])
    adt = dt * (-jnp.exp(alog_ref[...]))
    dtT = _softplus(dtT_ref[...] + dtbT_ref[...])
    adtT = dtT * (-jnp.exp(alogT_ref[...]))
    tril = _tri(L, True)
    a_cs = _dot_f32(tril.astype(F32), adt)
    a_csT = _dot_f32(adtT, _tri(L, False).astype(F32))
    w_st = dt * jnp.exp(a_cs[L - 1:L, :] - a_cs)
    left = lax.broadcasted_iota(jnp.int32, (L, LANES), 1) < SSD_HEAD_DIM

    def col(a, h):
        return jnp.broadcast_to(a[:, h:h + 1], (L, LANES))

    y_parts = []
    heads_per_group = SSD_HEADS // SSD_GROUPS
    for g in range(SSD_GROUPS):
        gs = slice(g * SSD_GROUP_W, (g + 1) * SSD_GROUP_W)
        bm_g = bm[:, g * SSD_D_STATE:(g + 1) * SSD_D_STATE]
        cm_g = cm[:, g * SSD_D_STATE:(g + 1) * SSD_D_STATE].astype(BF16)
        cb = _dot_nt(cm_g, bm_g.astype(BF16))
        st_prev = st_ref[:, gs]
        y_off = _dot(cm_g, st_prev.astype(BF16))
        diag, acs_pairs, xdec_pairs = [], [], []
        for hp in range(heads_per_group // 2):
            h0 = g * heads_per_group + 2 * hp
            cols = [col(a_cs, h0), col(a_cs, h0 + 1)]
            ms = []
            for h, acs_col in zip((h0, h0 + 1), cols):
                lm = jnp.exp(jnp.where(tril, acs_col - a_csT[h:h + 1, :], -jnp.inf))
                ms.append((cb * lm * dtT[h:h + 1, :]).astype(BF16))
            slab = xs[:, h0 * SSD_HEAD_DIM:(h0 + 2) * SSD_HEAD_DIM]
            rhs = jnp.concatenate([jnp.where(left, slab, 0.0), jnp.where(left, 0.0, slab)], axis=0)
            diag.append(_dot(jnp.concatenate(ms, axis=1), rhs.astype(BF16)))
            acs_pairs.append(jnp.where(left, cols[0], cols[1]))
            xdec_pairs.append(slab * jnp.where(left, col(w_st, h0), col(w_st, h0 + 1)))
        acs_e = jnp.concatenate(acs_pairs, axis=1)
        y_parts.append(jnp.concatenate(diag, axis=1) + y_off * jnp.exp(acs_e))
        xd_dec = jnp.concatenate(xdec_pairs, axis=1).astype(BF16)
        st_ref[:, gs] = st_prev * jnp.exp(acs_e[L - 1:L, :]) + _dot(bm_g.T.astype(BF16), xd_dec)

    y = jnp.concatenate(y_parts, axis=1) + dexp_ref[...] * xs
    yg = y * zact_ref[...]
    outs = []
    for g in range(SSD_GROUPS):
        v = yg[:, g * SSD_GROUP_W:(g + 1) * SSD_GROUP_W]
        outs.append(v * lax.rsqrt(jnp.mean(v * v, axis=-1, keepdims=True) + EPS))
    o_ref[...] = jnp.concatenate(outs, axis=1) * nw_ref[...]


def _mlstm_project(mx_ref, cbuf, cw_ref, cb_ref, wq_ref, wk_ref, wv_ref, wif_ref, bif_ref):
    L = CHUNK
    mx = mx_ref[...]
    xc = _causal_conv_silu(cbuf, mx_ref, cw_ref, cb_ref)
    xc_b = xc.astype(BF16)
    mx_b = mx.astype(BF16)
    tile = 2 * LANES
    q = jnp.concatenate([_dot(xc_b[:, t * tile:(t + 1) * tile], wq_ref[t]) for t in range(ML_HEADS)], axis=1)
    k = jnp.concatenate([_dot(xc_b[:, t * tile:(t + 1) * tile], wk_ref[t]) for t in range(ML_HEADS)], axis=1)
    v = jnp.concatenate([_dot(mx_b[:, t * tile:(t + 1) * tile], wv_ref[t]) for t in range(ML_HEADS)], axis=1)
    qkv = jnp.concatenate([q, k, v], axis=1).astype(BF16)
    gates = _dot(qkv, wif_ref[...]) + bif_ref[...]
    gatesT = gates.T[0:2 * ML_HEADS, :]
    logf = -_softplus(-gates)
    logfT = -_softplus(-gatesT)
    bcum = _dot_f32(_tri(L, True).astype(F32), logf)
    bcumT = _dot_f32(logfT, _tri(L, False).astype(F32))
    return dict(xc=xc, qkv=qkv, q=q, k=k, gates=gates, gatesT=gatesT, bcum=bcum, bcumT=bcumT)


def _mlstm_head(h, pre, ct_ref, n_ref, m_ref):
    L = CHUNK
    tril = _tri(L, True)
    qkv, gates, gatesT, bcum, bcumT = pre["qkv"], pre["gates"], pre["gatesT"], pre["bcum"], pre["bcumT"]
    hs = slice(h * ML_HEAD_DIM, (h + 1) * ML_HEAD_DIM)
    qh = qkv[:, hs]
    kh = qkv[:, ML_D_INNER + hs.start:ML_D_INNER + hs.stop]
    vh = qkv[:, 2 * ML_D_INNER + hs.start:2 * ML_D_INNER + hs.stop]
    bc = jnp.broadcast_to(bcum[:, ML_HEADS + h:ML_HEADS + h + 1], (L, LANES))
    ig = jnp.broadcast_to(gates[:, h:h + 1], (L, LANES))
    m_prev = m_ref[0:1, :]
    n_prev = n_ref[0:1, :]
    ct_prev = ct_ref[...]

    dlog = jnp.where(tril, bc - bcumT[ML_HEADS + h:ML_HEADS + h + 1, :] + gatesT[h:h + 1, :], -jnp.inf)
    m_inter = bc + m_prev
    m_t = jnp.maximum(jnp.max(dlog, axis=-1, keepdims=True), m_inter)
    scores = _dot_nt(qh, kh) * jnp.exp(dlog - m_t)
    inter_w = jnp.exp(m_inter - m_t)
    num = _dot(scores.astype(BF16), vh) + _rep(inter_w, 2) * _dot(qh, ct_prev.astype(BF16))
    den = (jnp.sum(scores, axis=-1, keepdims=True)
           + inter_w * jnp.sum(pre["q"][:, hs] * n_prev, axis=-1, keepdims=True))
    hout = num / _rep(jnp.maximum(jnp.abs(den), jnp.exp(-m_t)), 2)

    b_last = bc[L - 1:L, :]
    w_state = b_last - bc + ig
    m_new = jnp.maximum(b_last + m_prev, jnp.max(w_state, axis=0, keepdims=True))
    decay = _rep(jnp.exp(b_last + m_prev - m_new), 2)
    kw = pre["k"][:, hs] * _rep(jnp.exp(w_state - m_new), 2)
    ct_ref[...] = decay * ct_prev + _dot(kw.T.astype(BF16), vh)
    n_ref[...] = jnp.broadcast_to(decay * n_prev + jnp.sum(kw, axis=0, keepdims=True), n_ref.shape)
    m_ref[...] = jnp.broadcast_to(m_new, m_ref.shape)

    mu = jnp.mean(hout, axis=-1, keepdims=True)
    d = hout - mu
    var = jnp.mean(d * d, axis=-1, keepdims=True)
    return d * lax.rsqrt(var + EPS)


N_SSD_CONSTS = 8
N_ML_CONSTS = 9


def _mixer_kernel(*refs):
    n = SEQS_PER_STEP
    it = iter(refs)
    take = lambda k: [next(it) for _ in range(k)]
    xbc_ref, dt_ref = take(2)
    dtT_refs = take(n)
    (zact_ref,) = take(1)
    ssd_consts = take(N_SSD_CONSTS)
    mx_ref, mzact_ref = take(2)
    cw_ref, cb_ref, wq_ref, wk_ref, wv_ref, wif_ref, bif_ref, nw_ref, skip_ref = take(N_ML_CONSTS)
    ys_ref, ym_ref, cbuf_s, st_ref, cbuf_m, ct_ref, n_ref, m_ref = take(8)

    @pl.when(pl.program_id(1) == 0)
    def _():
        st_ref[...] = jnp.zeros(st_ref.shape, F32)
        cbuf_s[:, 0:CONV_HALO, :] = jnp.zeros((n, CONV_HALO, cbuf_s.shape[2]), F32)
        ct_ref[...] = jnp.zeros(ct_ref.shape, F32)
        n_ref[...] = jnp.zeros(n_ref.shape, F32)
        m_ref[...] = jnp.full(m_ref.shape, -1e30, F32)
        cbuf_m[:, 0:CONV_HALO, :] = jnp.zeros((n, CONV_HALO, cbuf_m.shape[2]), F32)

    pre = []
    for s in range(n):
        pre.append(_mlstm_project(mx_ref.at[s], cbuf_m.at[s], cw_ref, cb_ref, wq_ref, wk_ref, wv_ref,
                                  wif_ref, bif_ref))
        _ssd_chunk(xbc_ref.at[s], dt_ref.at[s], dtT_refs[s], zact_ref.at[s], *ssd_consts,
                   ys_ref.at[s], cbuf_s.at[s], st_ref.at[s])
    heads = [[] for _ in range(n)]
    for h in range(ML_HEADS):
        for s in range(n):
            heads[s].append(_mlstm_head(h, pre[s], ct_ref.at[s, h], n_ref.at[s, h], m_ref.at[s, h]))
    for s in range(n):
        hm = jnp.concatenate(heads[s], axis=1) * nw_ref[...]
        ym_ref[s] = (hm + skip_ref[...] * pre[s]["xc"]) * mzact_ref[s]


def _mixer(xbc, dt, dtT, zact, ssd_consts, mx, mzact, ml_consts, *, batch, seq):
    L = CHUNK
    nc = seq // L
    n = SEQS_PER_STEP
    seqs = lambda t: t.reshape(batch // n, n, seq, t.shape[-1])
    row = lambda w: pl.BlockSpec((None, n, L, w), lambda b, c: (b, 0, c, 0))
    dtT_spec = lambda s: pl.BlockSpec((SSD_HEADS, L), lambda b, c: (0, (b * n + s) * nc + c))
    assert len(ssd_consts) == N_SSD_CONSTS and len(ml_consts) == N_ML_CONSTS
    y_shape = jax.ShapeDtypeStruct((batch // n, n, seq, D_MODEL), F32)
    ys, ym = pl.pallas_call(
        _mixer_kernel,
        grid=(batch // n, nc),
        in_specs=[row(SSD_CONV_CH), row(LANES)] + [dtT_spec(s) for s in range(n)] + [row(SSD_D_INNER)]
                 + [_const_spec(a.shape) for a in ssd_consts]
                 + [row(ML_D_INNER)] * 2 + [_const_spec(a.shape) for a in ml_consts],
        out_specs=[row(SSD_D_INNER), row(ML_D_INNER)],
        out_shape=[y_shape, y_shape],
        scratch_shapes=[pltpu.VMEM((n, CONV_HALO + L, SSD_CONV_CH), F32),
                        pltpu.VMEM((n, SSD_D_STATE, SSD_D_INNER), F32),
                        pltpu.VMEM((n, CONV_HALO + L, ML_D_INNER), F32),
                        pltpu.VMEM((n, ML_HEADS, ML_HEAD_DIM, ML_HEAD_DIM), F32),
                        pltpu.VMEM((n, ML_HEADS, SUBLANES, ML_HEAD_DIM), F32),
                        pltpu.VMEM((n, ML_HEADS, SUBLANES, LANES), F32)],
        compiler_params=_cparams(("parallel", "arbitrary")),
        name="mixer",
    )(seqs(xbc), seqs(dt), *([dtT] * n), seqs(zact), *ssd_consts, seqs(mx), seqs(mzact), *ml_consts)
    return ys.reshape(batch * seq, D_MODEL), ym.reshape(batch * seq, D_MODEL)


def _s5_param_kernel(are_ref, aim_ref, lstep_ref, bre_ref, bim_ref, lre_ref, lim_ref, bbre_ref, bbim_ref):
    a_re, a_im = are_ref[...], aim_ref[...]
    step = jnp.exp(lstep_ref[...])
    mag = jnp.exp(a_re * step)
    lam_re = mag * jnp.cos(a_im * step)
    lam_im = mag * jnp.sin(a_im * step)
    den = a_re * a_re + a_im * a_im
    coef_re = ((lam_re - 1.0) * a_re + lam_im * a_im) / den
    coef_im = (lam_im * a_re - (lam_re - 1.0) * a_im) / den
    b_re, b_im = bre_ref[...], bim_ref[...]
    lre_ref[...] = lam_re
    lim_ref[...] = lam_im
    bbre_ref[...] = coef_re * b_re - coef_im * b_im
    bbim_ref[...] = coef_re * b_im + coef_im * b_re


def _s5_params(a_re, a_im, log_step, b_re, b_im):
    rep = lambda t: jnp.repeat(t, S5_GROUP, axis=0)
    are, aim = rep(a_re), rep(a_im)
    lstep = jnp.broadcast_to(rep(log_step[:, None]), are.shape)
    breT = jnp.swapaxes(b_re, 1, 2).reshape(D_MODEL, S5_STATE)
    bimT = jnp.swapaxes(b_im, 1, 2).reshape(D_MODEL, S5_STATE)
    shp = jax.ShapeDtypeStruct((D_MODEL, S5_STATE), F32)
    return pl.pallas_call(_s5_param_kernel, out_shape=[shp, shp, shp, shp], name="s5_params")(
        are, aim, lstep, breT, bimT)


def _s5_kernel(x_ref, nw_ref, bs_ref, lam_ref, cs_ref, dsk_ref, wa_ref, ba_ref, wb_ref, bb_ref, o_ref,
               bu_ref, xs_ref, st_ref, *, batch):
    rows = x_ref.shape[0]
    steps = rows // batch

    @pl.when(pl.program_id(0) == 0)
    def _():
        st_ref[...] = jnp.zeros(st_ref.shape, F32)

    x = x_ref[...]
    u = _rms(x, nw_ref[...])
    u_b = u.astype(BF16)
    def bu_slab(k):
        bu_ref[k % 2] = _dot(u_b[:, k * S5_SLAB_IN:(k + 1) * S5_SLAB_IN], bs_ref[k])

    ys = []
    bu_slab(0)
    for k in range(S5_SLABS):
        if k + 1 < S5_SLABS:
            bu_slab(k + 1)
        for w in range(S5_SLAB_ST // S5_SCAN_W):
            re_l = slice(w * S5_SCAN_W, (w + 1) * S5_SCAN_W)
            im_l = slice(S5_SLAB_ST + w * S5_SCAN_W, S5_SLAB_ST + (w + 1) * S5_SCAN_W)
            lr = jnp.broadcast_to(lam_ref[k, 0:1, re_l], (batch, S5_SCAN_W))
            li = jnp.broadcast_to(lam_ref[k, 0:1, im_l], (batch, S5_SCAN_W))
            xr, xi = st_ref[k, :, re_l], st_ref[k, :, im_l]
            for t in range(steps):
                rows_t = slice(t * batch, (t + 1) * batch)
                xr, xi = (lr * xr - li * xi + bu_ref[k % 2, rows_t, re_l],
                          lr * xi + li * xr + bu_ref[k % 2, rows_t, im_l])
                xs_ref[k % 2, rows_t, re_l] = xr.astype(BF16)
                xs_ref[k % 2, rows_t, im_l] = xi.astype(BF16)
            st_ref[k, :, re_l] = xr
            st_ref[k, :, im_l] = xi
        ys.append(_dot(xs_ref[k % 2], cs_ref[k]))
    y = jnp.concatenate(ys, axis=1) + dsk_ref[...] * u
    g = jax.nn.gelu(y).astype(BF16)
    out = (_dot(g, wa_ref[...]) + ba_ref[...]) * _sigmoid(_dot(g, wb_ref[...]) + bb_ref[...])
    o_ref[...] = x + out


def _s5(x_sb, nw, b_slabs, lam_slabs, c_slabs, dsk, wa, ba, wb, bb, *, batch, seq, ts=32):
    rows = ts * batch
    consts = [nw, b_slabs, lam_slabs, c_slabs, dsk, wa, ba, wb, bb]
    return pl.pallas_call(
        functools.partial(_s5_kernel, batch=batch),
        grid=(seq // ts,),
        in_specs=[pl.BlockSpec((rows, D_MODEL), lambda i: (i, 0))] + [_const_spec(a.shape) for a in consts],
        out_specs=pl.BlockSpec((rows, D_MODEL), lambda i: (i, 0)),
        out_shape=jax.ShapeDtypeStruct((seq * batch, D_MODEL), F32),
        scratch_shapes=[pltpu.VMEM((2, rows, 2 * S5_SLAB_ST), F32),
                        pltpu.VMEM((2, rows, 2 * S5_SLAB_ST), BF16),
                        pltpu.VMEM((S5_SLABS, batch, 2 * S5_SLAB_ST), F32)],
        compiler_params=_cparams(("arbitrary",)),
        name="s5",
    )(x_sb, *consts)


def _s5_slab_weights(lam_re, lam_im, bb_re, bb_im, c_re, c_im):
    eye = jnp.eye(S5_SLAB_GROUPS, dtype=F32)

    def in_slab(bb):
        t = bb.reshape(S5_SLABS, S5_SLAB_GROUPS, S5_GROUP, S5_STATE)
        t = t[:, :, :, None, :] * eye[None, :, None, :, None]
        return t.reshape(S5_SLABS, S5_SLAB_IN, S5_SLAB_ST)

    def out_slab(cc):
        t = jnp.swapaxes(cc, 1, 2).reshape(S5_SLABS, S5_SLAB_GROUPS, S5_STATE, S5_GROUP)
        t = t[:, :, :, None, :] * eye[None, :, None, :, None]
        return t.reshape(S5_SLABS, S5_SLAB_ST, S5_SLAB_IN)

    b_slabs = jnp.concatenate([in_slab(bb_re), in_slab(bb_im)], axis=2).astype(BF16)
    c_slabs = jnp.concatenate([out_slab(c_re), out_slab(-c_im)], axis=1).astype(BF16)
    lam = lambda t: t[::S5_GROUP].reshape(S5_SLABS, 1, S5_SLAB_ST)
    lam_slabs = jnp.concatenate([lam(lam_re), lam(lam_im)], axis=2)
    return b_slabs, jnp.broadcast_to(lam_slabs, (S5_SLABS, 8, 2 * S5_SLAB_ST)), c_slabs


def _blockdiag_tiles(w):
    nb = w.shape[0]
    tile = nb * ML_QKV_BLOCK // ML_HEADS
    rows = jnp.swapaxes(w, 1, 2).reshape(nb * ML_QKV_BLOCK, ML_QKV_BLOCK)
    sel = (jnp.arange(tile)[None, :] % ML_QKV_BLOCK == jnp.arange(ML_QKV_BLOCK)[:, None]).astype(w.dtype)
    wide = jnp.dot(rows, sel, precision=HIGHEST)
    blk_r = (jnp.arange(nb * ML_QKV_BLOCK) % tile) // ML_QKV_BLOCK
    blk_c = jnp.arange(tile) // ML_QKV_BLOCK
    wide = jnp.where(blk_r[:, None] == blk_c[None, :], wide, 0.0)
    return wide.reshape(ML_HEADS, tile, tile)


def _pad_lanes(t, n=LANES):
    return jnp.pad(t, ((0, 0), (0, n - t.shape[1])))


def kernel(x, ffn1_norm, ffn1_w_gate, ffn1_w_up, ffn1_w_down, mix_norm, ffn2_norm, ffn2_w_gate, ffn2_w_up, ffn2_w_down, hy_w_in, ssd_conv_w, ssd_conv_b, ssd_dt_bias, ssd_a_log, ssd_d, ssd_norm_w, ml_conv_w, ml_conv_b, ml_w_q, ml_w_k, ml_w_v, ml_w_if, ml_b_if, ml_norm_w, ml_skip, hy_w_out, s5_a_re, s5_a_im, s5_log_step, s5_b_re, s5_b_im, s5_c_re, s5_c_im, s5_d, s5_w_a, s5_b_a, s5_w_b, s5_b_b, final_norm):
    batch, seq, _ = x.shape
    assert seq % 512 == 0 and batch % 8 == 0
    row = lambda t: t.reshape(1, -1)
    bf = lambda t: t.astype(BF16)
    xf = x.reshape(batch * seq, D_MODEL)

    g1, u1, d1, g2, u2, d2 = _to_bf16([ffn1_w_gate, ffn1_w_up, ffn1_w_down, ffn2_w_gate, ffn2_w_up, ffn2_w_down])
    x1 = _ffn(xf, row(ffn1_norm[0]), g1, u1, d1, 0, batch=batch, seq=seq)

    w_in = hy_w_in[0]
    o1 = SSD_D_INNER
    o2 = o1 + SSD_CONV_CH
    o3 = o2 + SSD_HEADS
    w_main = bf(jnp.concatenate([w_in[:, :o2], w_in[:, o3:]], axis=1))
    w_dt = w_in[:, o2:o3]
    z_act, xbc, m_x, mz_act, dt_raw, dt_rawT = _inproj(x1, row(mix_norm[0]), w_main, bf(_pad_lanes(w_dt)),
                                                       bf(w_dt.T))

    ssd_consts = [ssd_conv_w[0], row(ssd_conv_b[0]),
                  _pad_lanes(row(ssd_dt_bias[0])), ssd_dt_bias[0].reshape(-1, 1),
                  _pad_lanes(row(ssd_a_log[0])), ssd_a_log[0].reshape(-1, 1),
                  row(jnp.repeat(ssd_d[0], SSD_HEAD_DIM)), row(ssd_norm_w[0])]

    k_scale = 1.0 / math.sqrt(ML_HEAD_DIM)
    w_if = ml_w_if[0]
    w_if = jnp.concatenate([w_if[:ML_D_INNER], w_if[ML_D_INNER:2 * ML_D_INNER] / k_scale, w_if[2 * ML_D_INNER:]],
                           axis=0)
    b_if = ml_b_if[0]
    ml_consts = [ml_conv_w[0], row(ml_conv_b[0]),
                 bf(_blockdiag_tiles(ml_w_q[0])), bf(_blockdiag_tiles(ml_w_k[0]) * k_scale),
                 bf(_blockdiag_tiles(ml_w_v[0])), bf(_pad_lanes(w_if)),
                 _pad_lanes(row(b_if)), row(ml_norm_w[0]), row(ml_skip[0])]
    y_ssd, y_ml = _mixer(xbc, dt_raw, dt_rawT, z_act, ssd_consts, m_x, mz_act, ml_consts, batch=batch, seq=seq)

    w_out = bf(hy_w_out[0])
    x3 = _ffn(x1, row(ffn2_norm[0]), g2, u2, d2, 0, batch=batch, seq=seq,
              pre=(y_ssd, y_ml, w_out[:SSD_D_INNER], w_out[SSD_D_INNER:]))

    x4 = _ffn(x3.reshape(batch // SUBLANES, SUBLANES, seq, D_MODEL), row(ffn1_norm[1]), g1, u1, d1, 1,
              batch=batch, seq=seq, in_layout="bt", out_layout="tb")
    lam_re, lam_im, bb_re, bb_im = _s5_params(s5_a_re[0], s5_a_im[0], s5_log_step[0], s5_b_re[0], s5_b_im[0])
    b_slabs, lam_slabs, c_slabs = _s5_slab_weights(lam_re, lam_im, bb_re, bb_im, s5_c_re[0], s5_c_im[0])
    x5 = _s5(x4.reshape(seq * batch, D_MODEL), row(mix_norm[1]), b_slabs, lam_slabs, c_slabs, row(s5_d[0]),
             bf(s5_w_a[0]), row(s5_b_a[0]), bf(s5_w_b[0]), row(s5_b_b[0]), batch=batch, seq=seq)
    out = _ffn(x5.reshape(seq, batch // SUBLANES, SUBLANES, D_MODEL), row(ffn2_norm[1]), g2, u2, d2, 1,
               batch=batch, seq=seq, in_layout="tb", out_layout="bt", final_w=row(final_norm))
    return out.reshape(batch, seq, D_MODEL)
```

```python
import functools
import math

import jax
import jax.numpy as jnp
from jax import lax
from jax.experimental import pallas as pl
from jax.experimental.pallas import tpu as pltpu

F32 = jnp.float32
BF16 = jnp.bfloat16
HIGHEST = lax.Precision.HIGHEST

D_MODEL = 1024
EPS = 1e-6
D_FF = 2816
FFN_RES = 0.5
CONV_W = 4
CONV_HALO = 8

SSD_HEADS = 16
SSD_HEAD_DIM = 64
SSD_GROUPS = 2
SSD_D_STATE = 128
SSD_D_INNER = SSD_HEADS * SSD_HEAD_DIM
SSD_BC = SSD_GROUPS * SSD_D_STATE
SSD_CONV_CH = SSD_D_INNER + 2 * SSD_BC
SSD_GROUP_W = SSD_D_INNER // SSD_GROUPS

ML_HEADS = 4
ML_HEAD_DIM = 256
ML_D_INNER = ML_HEADS * ML_HEAD_DIM
ML_QKV_BLOCK = 4

CHUNK = 128
LANES = 128
SUBLANES = 8
SEQS_PER_STEP = 2

S5_GROUP = 16
S5_GROUPS = D_MODEL // S5_GROUP
S5_STATE = 64
S5_SLAB_GROUPS = 16
S5_SLABS = S5_GROUPS // S5_SLAB_GROUPS
S5_SLAB_IN = S5_SLAB_GROUPS * S5_GROUP
S5_SLAB_ST = S5_SLAB_GROUPS * S5_STATE
S5_SCAN_W = 512

VMEM_LIMIT_BYTES = 56 * 1024 * 1024


def _cparams(sem):
    return pltpu.CompilerParams(dimension_semantics=sem, vmem_limit_bytes=VMEM_LIMIT_BYTES)


def _dot(a, b):
    return jnp.dot(a, b, preferred_element_type=F32)


def _dot_f32(a, b):
    return jnp.dot(a, b, preferred_element_type=F32, precision=HIGHEST)


def _dot_nt(a, b):
    return lax.dot_general(a, b, (((1,), (1,)), ((), ())), preferred_element_type=F32)


def _rms(x, w):
    return x * lax.rsqrt(jnp.mean(x * x, axis=-1, keepdims=True) + EPS) * w


def _sigmoid(x):
    return 1.0 / (1.0 + jnp.exp(-x))


def _silu(x):
    return x * _sigmoid(x)


def _softplus(x):
    return jnp.maximum(x, 0.0) + jnp.log1p(jnp.exp(-jnp.abs(x)))


def _rep(x, n):
    return jnp.concatenate([x] * n, axis=-1)


def _tri(n, lower):
    r = lax.broadcasted_iota(jnp.int32, (n, n), 0)
    c = lax.broadcasted_iota(jnp.int32, (n, n), 1)
    return r >= c if lower else r <= c


def _const_spec(shape):
    nd = len(shape)
    return pl.BlockSpec(shape, lambda *_: (0,) * nd, pipeline_mode=pl.Buffered(1))


def _layer_spec(shape, layer):
    return pl.BlockSpec((None,) + tuple(shape[1:]), lambda *_: (layer, 0, 0), pipeline_mode=pl.Buffered(1))


def _to_bf16_kernel(*refs):
    n = len(refs) // 2
    for src, dst in zip(refs[:n], refs[n:]):
        dst[...] = src[...].astype(BF16)


def _to_bf16(weights, blocks=8):
    specs = [pl.BlockSpec((1, w.shape[1] // blocks, w.shape[2]), lambda l, i: (l, i, 0)) for w in weights]
    return pl.pallas_call(
        _to_bf16_kernel,
        grid=(weights[0].shape[0], blocks),
        in_specs=specs,
        out_specs=specs,
        out_shape=[jax.ShapeDtypeStruct(w.shape, BF16) for w in weights],
        compiler_params=_cparams(("parallel", "parallel")),
        name="to_bf16",
    )(*weights)


MXU_TILE = 256
FFN_CHUNK_ROWS = 512
FFN_BIG_ROWS = 1024


def _ff_chunks(rows):
    if rows <= FFN_CHUNK_ROWS:
        return [(0, D_FF)]
    tiles = D_FF // MXU_TILE
    per = -(-tiles // (2 * rows // FFN_CHUNK_ROWS))
    edges = list(range(0, tiles, per)) + [tiles]
    return [(a * MXU_TILE, b * MXU_TILE) for a, b in zip(edges[:-1], edges[1:])]


def _ffn_kernel(*refs, has_pre, has_final, in_layout, out_layout):
    it = iter(refs)
    x_ref = next(it)
    if has_pre:
        ys_ref, ym_ref, wos_ref, wom_ref = next(it), next(it), next(it), next(it)
    nw_ref, wg_ref, wu_ref, wd_ref = next(it), next(it), next(it), next(it)
    if has_final:
        fw_ref = next(it)
    o_ref = next(it)
    scr = next(it, None)

    if in_layout == "tb":
        ts = x_ref.shape[0]
        pitch = ts + 1
        for t in range(ts):
            for j in range(scr.shape[0]):
                scr[j, pl.ds(t, SUBLANES, stride=pitch), :] = x_ref[t, :, j * LANES:(j + 1) * LANES]
        x = jnp.concatenate(
            [jnp.concatenate([scr[j, b * pitch:b * pitch + ts, :] for b in range(SUBLANES)], axis=0)
             for j in range(scr.shape[0])], axis=1)
    elif in_layout == "bt":
        x = x_ref[...].reshape(-1, D_MODEL)
    else:
        x = x_ref[...]
    if has_pre:
        x = x + _dot(ys_ref[...].astype(BF16), wos_ref[...]) + _dot(ym_ref[...].astype(BF16), wom_ref[...])
    h = _rms(x, nw_ref[...]).astype(BF16)
    y = None
    for lo, hi in _ff_chunks(x.shape[0]):
        a = (_silu(_dot(h, wg_ref[:, lo:hi])) * _dot(h, wu_ref[:, lo:hi])).astype(BF16)
        part = _dot(a, wd_ref[lo:hi, :])
        y = part if y is None else y + part
    x = x + FFN_RES * y
    if has_final:
        x = _rms(x, fw_ref[...])
    if out_layout == "tb":
        ts = o_ref.shape[0]
        pitch = ts + 1
        for j in range(scr.shape[0]):
            for b in range(SUBLANES):
                scr[j, b * pitch:b * pitch + ts, :] = x[b * ts:(b + 1) * ts, j * LANES:(j + 1) * LANES]
        for t in range(ts):
            for j in range(scr.shape[0]):
                o_ref[t, :, j * LANES:(j + 1) * LANES] = scr[j, pl.ds(t, SUBLANES, stride=pitch), :]
    elif out_layout == "bt":
        o_ref[...] = x.reshape(o_ref.shape)
    else:
        o_ref[...] = x


def _row_spec(layout, tm, n_inner):
    if layout == "bs":
        return pl.BlockSpec((tm, D_MODEL), lambda b, i: (b * n_inner + i, 0))
    ts = tm // SUBLANES
    if layout == "bt":
        return pl.BlockSpec((None, SUBLANES, ts, D_MODEL), lambda b, i: (b, 0, i, 0))
    return pl.BlockSpec((ts, None, SUBLANES, D_MODEL), lambda b, i: (i, b, 0, 0))


def _ffn(x, nw, wg, wu, wd, layer, *, batch, seq, in_layout="bs", out_layout="bs", tm=512, pre=None,
         final_w=None):
    if in_layout == "bs":
        grid = (batch, seq // tm)
        n_inner = seq // tm
    else:
        grid = (batch // SUBLANES, seq * SUBLANES // tm)
        n_inner = None
    args = [x]
    specs = [_row_spec(in_layout, tm, n_inner)]
    if pre is not None:
        ys, ym, wos, wom = pre
        args += [ys, ym, wos, wom]
        specs += [_row_spec("bs", tm, n_inner), _row_spec("bs", tm, n_inner),
                  _const_spec(wos.shape), _const_spec(wom.shape)]
    args += [nw, wg, wu, wd]
    specs += [_const_spec(nw.shape), _layer_spec(wg.shape, layer), _layer_spec(wu.shape, layer),
              _layer_spec(wd.shape, layer)]
    if final_w is not None:
        args.append(final_w)
        specs.append(_const_spec(final_w.shape))
    out_shape = {"bs": (batch * seq, D_MODEL),
                 "bt": (batch // SUBLANES, SUBLANES, seq, D_MODEL),
                 "tb": (seq, batch // SUBLANES, SUBLANES, D_MODEL)}[out_layout]
    scratch = ([pltpu.VMEM((D_MODEL // LANES, tm + SUBLANES, LANES), F32)]
               if "tb" in (in_layout, out_layout) else [])
    return pl.pallas_call(
        functools.partial(_ffn_kernel, has_pre=pre is not None, has_final=final_w is not None,
                          in_layout=in_layout, out_layout=out_layout),
        grid=grid,
        in_specs=specs,
        out_specs=_row_spec(out_layout, tm, n_inner),
        out_shape=jax.ShapeDtypeStruct(out_shape, F32),
        scratch_shapes=scratch,
        compiler_params=_cparams(("parallel", "parallel")),
        name="ffn",
    )(*args)


def _inproj_kernel(x_ref, nw_ref, wm_ref, wdt_ref, wdtT_ref, z_ref, xbc_ref, mx_ref, mz_ref, dt_ref, dtT_ref):
    u = _rms(x_ref[...], nw_ref[...]).astype(BF16)
    o1 = SSD_D_INNER
    o2 = o1 + SSD_CONV_CH
    o3 = o2 + ML_D_INNER
    proj = _dot(u, wm_ref[...])
    z_ref[...] = proj[:, :o1]
    xbc_ref[...] = proj[:, o1:o2]
    mx_ref[...] = proj[:, o2:o3]
    mz_ref[...] = proj[:, o3:]
    dt_ref[...] = _dot(u, wdt_ref[...])
    dtT_ref[...] = _dot_nt(wdtT_ref[...], u)


def _inproj(x, nw, w_main, w_dt, w_dtT, *, tm=512):
    rows = x.shape[0]
    row = lambda w: pl.BlockSpec((tm, w), lambda i: (i, 0))
    consts = [nw, w_main, w_dt, w_dtT]
    widths = [SSD_D_INNER, SSD_CONV_CH, ML_D_INNER, ML_D_INNER, LANES]
    return pl.pallas_call(
        _inproj_kernel,
        grid=(rows // tm,),
        in_specs=[row(D_MODEL)] + [_const_spec(a.shape) for a in consts],
        out_specs=[row(w) for w in widths] + [pl.BlockSpec((SSD_HEADS, tm), lambda i: (0, i))],
        out_shape=[jax.ShapeDtypeStruct((rows, w), F32) for w in widths]
                  + [jax.ShapeDtypeStruct((SSD_HEADS, rows), F32)],
        compiler_params=_cparams(("parallel",)),
        name="inproj",
    )(x, *consts)


def _causal_conv_silu(cbuf, x_ref, w_ref, b_ref):
    L = x_ref.shape[0]
    cbuf[CONV_HALO:CONV_HALO + L, :] = x_ref[...]
    ext = cbuf[...]
    prev = pltpu.roll(ext, 1, axis=0)
    near = ext * w_ref[3:4, :] + prev * w_ref[2:3, :]
    far = ext * w_ref[1:2, :] + prev * w_ref[0:1, :]
    acc = b_ref[...] + near[CONV_HALO:, :] + pltpu.roll(far, 2, axis=0)[CONV_HALO:, :]
    cbuf[0:CONV_HALO, :] = ext[L:L + CONV_HALO, :]
    return _silu(acc)


def _ssd_chunk(xbc_ref, dt_ref, dtT_ref, z_ref, cw_ref, cb_ref, dtb_ref, dtbT_ref, alog_ref, alogT_ref,
               dexp_ref, nw_ref, o_ref, cbuf, st_ref):
    L = CHUNK
    xbc = _causal_conv_silu(cbuf, xbc_ref, cw_ref, cb_ref)
    xs = xbc[:, :SSD_D_INNER]
    bm = xbc[:, SSD_D_INNER:SSD_D_INNER + SSD_BC]
    cm = xbc[:, SSD_D_INNER + SSD_BC:]

    dt = _softplus(dt_ref[...] + dtb_ref[...])
    adt = dt * (-jnp.exp(alog_ref[...]))
    dtT = _softplus(dtT_ref[...] + dtbT_ref[...])
    adtT = dtT * (-jnp.exp(alogT_ref[...]))
    tril = _tri(L, True)
    a_cs = _dot_f32(tril.astype(F32), adt)
    a_csT = _dot_f32(adtT, _tri(L, False).astype(F32))
    w_st = dt * jnp.exp(a_cs[L - 1:L, :] - a_cs)
    left = lax.broadcasted_iota(jnp.int32, (L, LANES), 1) < SSD_HEAD_DIM

    def col(a, h):
        return jnp.broadcast_to(a[:, h:h + 1], (L, LANES))

    y_parts = []
    heads_per_group = SSD_HEADS // SSD_GROUPS
    for g in range(SSD_GROUPS):
        gs = slice(g * SSD_GROUP_W, (g + 1) * SSD_GROUP_W)
        bm_g = bm[:, g * SSD_D_STATE:(g + 1) * SSD_D_STATE]
        cm_g = cm[:, g * SSD_D_STATE:(g + 1) * SSD_D_STATE].astype(BF16)
        cb = _dot_nt(cm_g, bm_g.astype(BF16))
        st_prev = st_ref[:, gs]
        y_off = _dot(cm_g, st_prev.astype(BF16))
        diag, acs_pairs, xdec_pairs = [], [], []
        for hp in range(heads_per_group // 2):
            h0 = g * heads_per_group + 2 * hp
            cols = [col(a_cs, h0), col(a_cs, h0 + 1)]
            ms = []
            for h, acs_col in zip((h0, h0 + 1), cols):
                lm = jnp.exp(jnp.where(tril, acs_col - a_csT[h:h + 1, :], -jnp.inf))
                ms.append((cb * lm * dtT[h:h + 1, :]).astype(BF16))
            slab = xs[:, h0 * SSD_HEAD_DIM:(h0 + 2) * SSD_HEAD_DIM]
            rhs = jnp.concatenate([jnp.where(left, slab, 0.0), jnp.where(left, 0.0, slab)], axis=0)
            diag.append(_dot(jnp.concatenate(ms, axis=1), rhs.astype(BF16)))
            acs_pairs.append(jnp.where(left, cols[0], cols[1]))
            xdec_pairs.append(slab * jnp.where(left, col(w_st, h0), col(w_st, h0 + 1)))
        acs_e = jnp.concatenate(acs_pairs, axis=1)
        y_parts.append(jnp.concatenate(diag, axis=1) + y_off * jnp.exp(acs_e))
        xd_dec = jnp.concatenate(xdec_pairs, axis=1).astype(BF16)
        st_ref[:, gs] = st_prev * jnp.exp(acs_e[L - 1:L, :]) + _dot(bm_g.T.astype(BF16), xd_dec)

    y = jnp.concatenate(y_parts, axis=1) + dexp_ref[...] * xs
    yg = y * _silu(z_ref[...])
    outs = []
    for g in range(SSD_GROUPS):
        v = yg[:, g * SSD_GROUP_W:(g + 1) * SSD_GROUP_W]
        outs.append(v * lax.rsqrt(jnp.mean(v * v, axis=-1, keepdims=True) + EPS))
    o_ref[...] = jnp.concatenate(outs, axis=1) * nw_ref[...]


def _mlstm_project(mx_ref, cbuf, cw_ref, cb_ref, wq_ref, wk_ref, wv_ref, wif_ref, bif_ref):
    L = CHUNK
    mx = mx_ref[...]
    xc = _causal_conv_silu(cbuf, mx_ref, cw_ref, cb_ref)
    xc_b = xc.astype(BF16)
    mx_b = mx.astype(BF16)
    tile = 2 * LANES
    q = jnp.concatenate([_dot(xc_b[:, t * tile:(t + 1) * tile], wq_ref[t]) for t in range(ML_HEADS)], axis=1)
    k = jnp.concatenate([_dot(xc_b[:, t * tile:(t + 1) * tile], wk_ref[t]) for t in range(ML_HEADS)], axis=1)
    v = jnp.concatenate([_dot(mx_b[:, t * tile:(t + 1) * tile], wv_ref[t]) for t in range(ML_HEADS)], axis=1)
    qkv = jnp.concatenate([q, k, v], axis=1).astype(BF16)
    gates = _dot(qkv, wif_ref[...]) + bif_ref[...]
    gatesT = gates.T[0:2 * ML_HEADS, :]
    logf = -_softplus(-gates)
    logfT = -_softplus(-gatesT)
    bcum = _dot_f32(_tri(L, True).astype(F32), logf)
    bcumT = _dot_f32(logfT, _tri(L, False).astype(F32))
    return dict(xc=xc, qkv=qkv, q=q, k=k, gates=gates, gatesT=gatesT, bcum=bcum, bcumT=bcumT)


def _mlstm_head(h, pre, ct_ref, n_ref, m_ref):
    L = CHUNK
    tril = _tri(L, True)
    qkv, gates, gatesT, bcum, bcumT = pre["qkv"], pre["gates"], pre["gatesT"], pre["bcum"], pre["bcumT"]
    hs = slice(h * ML_HEAD_DIM, (h + 1) * ML_HEAD_DIM)
    qh = qkv[:, hs]
    kh = qkv[:, ML_D_INNER + hs.start:ML_D_INNER + hs.stop]
    vh = qkv[:, 2 * ML_D_INNER + hs.start:2 * ML_D_INNER + hs.stop]
    bc = jnp.broadcast_to(bcum[:, ML_HEADS + h:ML_HEADS + h + 1], (L, LANES))
    ig = jnp.broadcast_to(gates[:, h:h + 1], (L, LANES))
    m_prev = m_ref[0:1, :]
    n_prev = n_ref[0:1, :]
    ct_prev = ct_ref[...]

    dlog = jnp.where(tril, bc - bcumT[ML_HEADS + h:ML_HEADS + h + 1, :] + gatesT[h:h + 1, :], -jnp.inf)
    m_inter = bc + m_prev
    m_t = jnp.maximum(jnp.max(dlog, axis=-1, keepdims=True), m_inter)
    scores = _dot_nt(qh, kh) * jnp.exp(dlog - m_t)
    inter_w = jnp.exp(m_inter - m_t)
    num = _dot(scores.astype(BF16), vh) + _rep(inter_w, 2) * _dot(qh, ct_prev.astype(BF16))
    den = (jnp.sum(scores, axis=-1, keepdims=True)
           + inter_w * jnp.sum(pre["q"][:, hs] * n_prev, axis=-1, keepdims=True))
    hout = num / _rep(jnp.maximum(jnp.abs(den), jnp.exp(-m_t)), 2)

    b_last = bc[L - 1:L, :]
    w_state = b_last - bc + ig
    m_new = jnp.maximum(b_last + m_prev, jnp.max(w_state, axis=0, keepdims=True))
    decay = _rep(jnp.exp(b_last + m_prev - m_new), 2)
    kw = pre["k"][:, hs] * _rep(jnp.exp(w_state - m_new), 2)
    ct_ref[...] = decay * ct_prev + _dot(kw.T.astype(BF16), vh)
    n_ref[...] = jnp.broadcast_to(decay * n_prev + jnp.sum(kw, axis=0, keepdims=True), n_ref.shape)
    m_ref[...] = jnp.broadcast_to(m_new, m_ref.shape)

    mu = jnp.mean(hout, axis=-1, keepdims=True)
    d = hout - mu
    var = jnp.mean(d * d, axis=-1, keepdims=True)
    return d * lax.rsqrt(var + EPS)


N_SSD_CONSTS = 8
N_ML_CONSTS = 9


def _mixer_kernel(*refs):
    n = SEQS_PER_STEP
    it = iter(refs)
    take = lambda k: [next(it) for _ in range(k)]
    xbc_ref, dt_ref = take(2)
    dtT_refs = take(n)
    (z_ref,) = take(1)
    ssd_consts = take(N_SSD_CONSTS)
    mx_ref, mz_ref = take(2)
    cw_ref, cb_ref, wq_ref, wk_ref, wv_ref, wif_ref, bif_ref, nw_ref, skip_ref = take(N_ML_CONSTS)
    ys_ref, ym_ref, cbuf_s, st_ref, cbuf_m, ct_ref, n_ref, m_ref = take(8)

    @pl.when(pl.program_id(1) == 0)
    def _():
        st_ref[...] = jnp.zeros(st_ref.shape, F32)
        cbuf_s[:, 0:CONV_HALO, :] = jnp.zeros((n, CONV_HALO, cbuf_s.shape[2]), F32)
        ct_ref[...] = jnp.zeros(ct_ref.shape, F32)
        n_ref[...] = jnp.zeros(n_ref.shape, F32)
        m_ref[...] = jnp.full(m_ref.shape, -1e30, F32)
        cbuf_m[:, 0:CONV_HALO, :] = jnp.zeros((n, CONV_HALO, cbuf_m.shape[2]), F32)

    pre = []
    for s in range(n):
        pre.append(_mlstm_project(mx_ref.at[s], cbuf_m.at[s], cw_ref, cb_ref, wq_ref, wk_ref, wv_ref,
                                  wif_ref, bif_ref))
        _ssd_chunk(xbc_ref.at[s], dt_ref.at[s], dtT_refs[s], z_ref.at[s], *ssd_consts,
                   ys_ref.at[s], cbuf_s.at[s], st_ref.at[s])
    heads = [[] for _ in range(n)]
    for h in range(ML_HEADS):
        for s in range(n):
            heads[s].append(_mlstm_head(h, pre[s], ct_ref.at[s, h], n_ref.at[s, h], m_ref.at[s, h]))
    for s in range(n):
        hm = jnp.concatenate(heads[s], axis=1) * nw_ref[...]
        ym_ref[s] = (hm + skip_ref[...] * pre[s]["xc"]) * _silu(mz_ref[s])


def _mixer(xbc, dt, dtT, z, ssd_consts, mx, mz, ml_consts, *, batch, seq):
    L = CHUNK
    nc = seq // L
    n = SEQS_PER_STEP
    seqs = lambda t: t.reshape(batch // n, n, seq, t.shape[-1])
    row = lambda w: pl.BlockSpec((None, n, L, w), lambda b, c: (b, 0, c, 0))
    dtT_spec = lambda s: pl.BlockSpec((SSD_HEADS, L), lambda b, c: (0, (b * n + s) * nc + c))
    assert len(ssd_consts) == N_SSD_CONSTS and len(ml_consts) == N_ML_CONSTS
    y_shape = jax.ShapeDtypeStruct((batch // n, n, seq, D_MODEL), F32)
    ys, ym = pl.pallas_call(
        _mixer_kernel,
        grid=(batch // n, nc),
        in_specs=[row(SSD_CONV_CH), row(LANES)] + [dtT_spec(s) for s in range(n)] + [row(SSD_D_INNER)]
                 + [_const_spec(a.shape) for a in ssd_consts]
                 + [row(ML_D_INNER)] * 2 + [_const_spec(a.shape) for a in ml_consts],
        out_specs=[row(SSD_D_INNER), row(ML_D_INNER)],
        out_shape=[y_shape, y_shape],
        scratch_shapes=[pltpu.VMEM((n, CONV_HALO + L, SSD_CONV_CH), F32),
                        pltpu.VMEM((n, SSD_D_STATE, SSD_D_INNER), F32),
                        pltpu.VMEM((n, CONV_HALO + L, ML_D_INNER), F32),
                        pltpu.VMEM((n, ML_HEADS, ML_HEAD_DIM, ML_HEAD_DIM), F32),
                        pltpu.VMEM((n, ML_HEADS, SUBLANES, ML_HEAD_DIM), F32),
                        pltpu.VMEM((n, ML_HEADS, SUBLANES, LANES), F32)],
        compiler_params=_cparams(("parallel", "arbitrary")),
        name="mixer",
    )(seqs(xbc), seqs(dt), *([dtT] * n), seqs(z), *ssd_consts, seqs(mx), seqs(mz), *ml_consts)
    return ys.reshape(batch * seq, D_MODEL), ym.reshape(batch * seq, D_MODEL)


def _s5_param_kernel(are_ref, aim_ref, lstep_ref, bre_ref, bim_ref, lre_ref, lim_ref, bbre_ref, bbim_ref):
    a_re, a_im = are_ref[...], aim_ref[...]
    step = jnp.exp(lstep_ref[...])
    mag = jnp.exp(a_re * step)
    lam_re = mag * jnp.cos(a_im * step)
    lam_im = mag * jnp.sin(a_im * step)
    den = a_re * a_re + a_im * a_im
    coef_re = ((lam_re - 1.0) * a_re + lam_im * a_im) / den
    coef_im = (lam_im * a_re - (lam_re - 1.0) * a_im) / den
    b_re, b_im = bre_ref[...], bim_ref[...]
    lre_ref[...] = lam_re
    lim_ref[...] = lam_im
    bbre_ref[...] = coef_re * b_re - coef_im * b_im
    bbim_ref[...] = coef_re * b_im + coef_im * b_re


def _s5_params(a_re, a_im, log_step, b_re, b_im):
    rep = lambda t: jnp.repeat(t, S5_GROUP, axis=0)
    are, aim = rep(a_re), rep(a_im)
    lstep = jnp.broadcast_to(rep(log_step[:, None]), are.shape)
    breT = jnp.swapaxes(b_re, 1, 2).reshape(D_MODEL, S5_STATE)
    bimT = jnp.swapaxes(b_im, 1, 2).reshape(D_MODEL, S5_STATE)
    shp = jax.ShapeDtypeStruct((D_MODEL, S5_STATE), F32)
    return pl.pallas_call(_s5_param_kernel, out_shape=[shp, shp, shp, shp], name="s5_params")(
        are, aim, lstep, breT, bimT)


def _s5_kernel(x_ref, nw_ref, bs_ref, lam_ref, cs_ref, dsk_ref, wa_ref, ba_ref, wb_ref, bb_ref, o_ref,
               bu_ref, xs_ref, st_ref, *, batch):
    rows = x_ref.shape[0]
    steps = rows // batch

    @pl.when(pl.program_id(0) == 0)
    def _():
        st_ref[...] = jnp.zeros(st_ref.shape, F32)

    x = x_ref[...]
    u = _rms(x, nw_ref[...])
    u_b = u.astype(BF16)
    def bu_slab(k):
        bu_ref[k % 2] = _dot(u_b[:, k * S5_SLAB_IN:(k + 1) * S5_SLAB_IN], bs_ref[k])

    ys = []
    bu_slab(0)
    for k in range(S5_SLABS):
        if k + 1 < S5_SLABS:
            bu_slab(k + 1)
        for w in range(S5_SLAB_ST // S5_SCAN_W):
            re_l = slice(w * S5_SCAN_W, (w + 1) * S5_SCAN_W)
            im_l = slice(S5_SLAB_ST + w * S5_SCAN_W, S5_SLAB_ST + (w + 1) * S5_SCAN_W)
            lr = jnp.broadcast_to(lam_ref[k, 0:1, re_l], (batch, S5_SCAN_W))
            li = jnp.broadcast_to(lam_ref[k, 0:1, im_l], (batch, S5_SCAN_W))
            xr, xi = st_ref[k, :, re_l], st_ref[k, :, im_l]
            for t in range(steps):
                rows_t = slice(t * batch, (t + 1) * batch)
                xr, xi = (lr * xr - li * xi + bu_ref[k % 2, rows_t, re_l],
                          lr * xi + li * xr + bu_ref[k % 2, rows_t, im_l])
                xs_ref[k % 2, rows_t, re_l] = xr.astype(BF16)
                xs_ref[k % 2, rows_t, im_l] = xi.astype(BF16)
            st_ref[k, :, re_l] = xr
            st_ref[k, :, im_l] = xi
        ys.append(_dot(xs_ref[k % 2], cs_ref[k]))
    y = jnp.concatenate(ys, axis=1) + dsk_ref[...] * u
    g = jax.nn.gelu(y).astype(BF16)
    out = (_dot(g, wa_ref[...]) + ba_ref[...]) * _sigmoid(_dot(g, wb_ref[...]) + bb_ref[...])
    o_ref[...] = x + out


def _s5(x_sb, nw, b_slabs, lam_slabs, c_slabs, dsk, wa, ba, wb, bb, *, batch, seq, ts=32):
    rows = ts * batch
    consts = [nw, b_slabs, lam_slabs, c_slabs, dsk, wa, ba, wb, bb]
    return pl.pallas_call(
        functools.partial(_s5_kernel, batch=batch),
        grid=(seq // ts,),
        in_specs=[pl.BlockSpec((rows, D_MODEL), lambda i: (i, 0))] + [_const_spec(a.shape) for a in consts],
        out_specs=pl.BlockSpec((rows, D_MODEL), lambda i: (i, 0)),
        out_shape=jax.ShapeDtypeStruct((seq * batch, D_MODEL), F32),
        scratch_shapes=[pltpu.VMEM((2, rows, 2 * S5_SLAB_ST), F32),
                        pltpu.VMEM((2, rows, 2 * S5_SLAB_ST), BF16),
                        pltpu.VMEM((S5_SLABS, batch, 2 * S5_SLAB_ST), F32)],
        compiler_params=_cparams(("arbitrary",)),
        name="s5",
    )(x_sb, *consts)


def _s5_slab_weights(lam_re, lam_im, bb_re, bb_im, c_re, c_im):
    eye = jnp.eye(S5_SLAB_GROUPS, dtype=F32)

    def in_slab(bb):
        t = bb.reshape(S5_SLABS, S5_SLAB_GROUPS, S5_GROUP, S5_STATE)
        t = t[:, :, :, None, :] * eye[None, :, None, :, None]
        return t.reshape(S5_SLABS, S5_SLAB_IN, S5_SLAB_ST)

    def out_slab(cc):
        t = jnp.swapaxes(cc, 1, 2).reshape(S5_SLABS, S5_SLAB_GROUPS, S5_STATE, S5_GROUP)
        t = t[:, :, :, None, :] * eye[None, :, None, :, None]
        return t.reshape(S5_SLABS, S5_SLAB_ST, S5_SLAB_IN)

    b_slabs = jnp.concatenate([in_slab(bb_re), in_slab(bb_im)], axis=2).astype(BF16)
    c_slabs = jnp.concatenate([out_slab(c_re), out_slab(-c_im)], axis=1).astype(BF16)
    lam = lambda t: t[::S5_GROUP].reshape(S5_SLABS, 1, S5_SLAB_ST)
    lam_slabs = jnp.concatenate([lam(lam_re), lam(lam_im)], axis=2)
    return b_slabs, jnp.broadcast_to(lam_slabs, (S5_SLABS, 8, 2 * S5_SLAB_ST)), c_slabs


def _blockdiag_tiles(w):
    nb = w.shape[0]
    tile = nb * ML_QKV_BLOCK // ML_HEADS
    rows = jnp.swapaxes(w, 1, 2).reshape(nb * ML_QKV_BLOCK, ML_QKV_BLOCK)
    sel = (jnp.arange(tile)[None, :] % ML_QKV_BLOCK == jnp.arange(ML_QKV_BLOCK)[:, None]).astype(w.dtype)
    wide = jnp.dot(rows, sel, precision=HIGHEST)
    blk_r = (jnp.arange(nb * ML_QKV_BLOCK) % tile) // ML_QKV_BLOCK
    blk_c = jnp.arange(tile) // ML_QKV_BLOCK
    wide = jnp.where(blk_r[:, None] == blk_c[None, :], wide, 0.0)
    return wide.reshape(ML_HEADS, tile, tile)


def _pad_lanes(t, n=LANES):
    return jnp.pad(t, ((0, 0), (0, n - t.shape[1])))


def kernel(x, ffn1_norm, ffn1_w_gate, ffn1_w_up, ffn1_w_down, mix_norm, ffn2_norm, ffn2_w_gate, ffn2_w_up, ffn2_w_down, hy_w_in, ssd_conv_w, ssd_conv_b, ssd_dt_bias, ssd_a_log, ssd_d, ssd_norm_w, ml_conv_w, ml_conv_b, ml_w_q, ml_w_k, ml_w_v, ml_w_if, ml_b_if, ml_norm_w, ml_skip, hy_w_out, s5_a_re, s5_a_im, s5_log_step, s5_b_re, s5_b_im, s5_c_re, s5_c_im, s5_d, s5_w_a, s5_b_a, s5_w_b, s5_b_b, final_norm):
    batch, seq, _ = x.shape
    assert seq % 512 == 0 and batch % 8 == 0
    row = lambda t: t.reshape(1, -1)
    bf = lambda t: t.astype(BF16)
    xf = x.reshape(batch * seq, D_MODEL)

    g1, u1, d1, g2, u2, d2 = _to_bf16([ffn1_w_gate, ffn1_w_up, ffn1_w_down, ffn2_w_gate, ffn2_w_up, ffn2_w_down])
    x1 = _ffn(xf, row(ffn1_norm[0]), g1, u1, d1, 0, batch=batch, seq=seq, tm=FFN_BIG_ROWS)

    w_in = hy_w_in[0]
    o1 = SSD_D_INNER
    o2 = o1 + SSD_CONV_CH
    o3 = o2 + SSD_HEADS
    w_main = bf(jnp.concatenate([w_in[:, :o2], w_in[:, o3:]], axis=1))
    w_dt = w_in[:, o2:o3]
    z_s, xbc, m_x, m_z, dt_raw, dt_rawT = _inproj(x1, row(mix_norm[0]), w_main, bf(_pad_lanes(w_dt)),
                                                       bf(w_dt.T))

    ssd_consts = [ssd_conv_w[0], row(ssd_conv_b[0]),
                  _pad_lanes(row(ssd_dt_bias[0])), ssd_dt_bias[0].reshape(-1, 1),
                  _pad_lanes(row(ssd_a_log[0])), ssd_a_log[0].reshape(-1, 1),
                  row(jnp.repeat(ssd_d[0], SSD_HEAD_DIM)), row(ssd_norm_w[0])]

    k_scale = 1.0 / math.sqrt(ML_HEAD_DIM)
    w_if = ml_w_if[0]
    w_if = jnp.concatenate([w_if[:ML_D_INNER], w_if[ML_D_INNER:2 * ML_D_INNER] / k_scale, w_if[2 * ML_D_INNER:]],
                           axis=0)
    b_if = ml_b_if[0]
    ml_consts = [ml_conv_w[0], row(ml_conv_b[0]),
                 bf(_blockdiag_tiles(ml_w_q[0])), bf(_blockdiag_tiles(ml_w_k[0]) * k_scale),
                 bf(_blockdiag_tiles(ml_w_v[0])), bf(_pad_lanes(w_if)),
                 _pad_lanes(row(b_if)), row(ml_norm_w[0]), row(ml_skip[0])]
    y_ssd, y_ml = _mixer(xbc, dt_raw, dt_rawT, z_s, ssd_consts, m_x, m_z, ml_consts, batch=batch, seq=seq)

    w_out = bf(hy_w_out[0])
    x3 = _ffn(x1, row(ffn2_norm[0]), g2, u2, d2, 0, batch=batch, seq=seq,
              pre=(y_ssd, y_ml, w_out[:SSD_D_INNER], w_out[SSD_D_INNER:]))

    x4 = _ffn(x3.reshape(batch // SUBLANES, SUBLANES, seq, D_MODEL), row(ffn1_norm[1]), g1, u1, d1, 1,
              batch=batch, seq=seq, in_layout="bt", out_layout="tb", tm=FFN_BIG_ROWS)
    lam_re, lam_im, bb_re, bb_im = _s5_params(s5_a_re[0], s5_a_im[0], s5_log_step[0], s5_b_re[0], s5_b_im[0])
    b_slabs, lam_slabs, c_slabs = _s5_slab_weights(lam_re, lam_im, bb_re, bb_im, s5_c_re[0], s5_c_im[0])
    x5 = _s5(x4.reshape(seq * batch, D_MODEL), row(mix_norm[1]), b_slabs, lam_slabs, c_slabs, row(s5_d[0]),
             bf(s5_w_a[0]), row(s5_b_a[0]), bf(s5_w_b[0]), row(s5_b_b[0]), batch=batch, seq=seq)
    out = _ffn(x5.reshape(seq, batch // SUBLANES, SUBLANES, D_MODEL), row(ffn2_norm[1]), g2, u2, d2, 1,
               batch=batch, seq=seq, in_layout="tb", out_layout="bt", final_w=row(final_norm), tm=FFN_BIG_ROWS)
    return out.reshape(batch, seq, D_MODEL)
```

```python
import functools
import math

import jax
import jax.numpy as jnp
from jax import lax
from jax.experimental import pallas as pl
from jax.experimental.pallas import tpu as pltpu

F32 = jnp.float32
BF16 = jnp.bfloat16
HIGHEST = lax.Precision.HIGHEST

D_MODEL = 1024
EPS = 1e-6
D_FF = 2816
FFN_RES = 0.5
CONV_W = 4
CONV_HALO = 8

SSD_HEADS = 16
SSD_HEAD_DIM = 64
SSD_GROUPS = 2
SSD_D_STATE = 128
SSD_D_INNER = SSD_HEADS * SSD_HEAD_DIM
SSD_BC = SSD_GROUPS * SSD_D_STATE
SSD_CONV_CH = SSD_D_INNER + 2 * SSD_BC
SSD_GROUP_W = SSD_D_INNER // SSD_GROUPS

ML_HEADS = 4
ML_HEAD_DIM = 256
ML_D_INNER = ML_HEADS * ML_HEAD_DIM
ML_QKV_BLOCK = 4

CHUNK = 128
LANES = 128
SUBLANES = 8
SEQS_PER_STEP = 2

S5_GROUP = 16
S5_GROUPS = D_MODEL // S5_GROUP
S5_STATE = 64
S5_SLAB_GROUPS = 16
S5_SLABS = S5_GROUPS // S5_SLAB_GROUPS
S5_SLAB_IN = S5_SLAB_GROUPS * S5_GROUP
S5_SLAB_ST = S5_SLAB_GROUPS * S5_STATE
S5_SCAN_W = 512

VMEM_LIMIT_BYTES = 56 * 1024 * 1024


def _cparams(sem):
    return pltpu.CompilerParams(dimension_semantics=sem, vmem_limit_bytes=VMEM_LIMIT_BYTES)


def _dot(a, b):
    return jnp.dot(a, b, preferred_element_type=F32)


def _dot_f32(a, b):
    return jnp.dot(a, b, preferred_element_type=F32, precision=HIGHEST)


def _dot_nt(a, b):
    return lax.dot_general(a, b, (((1,), (1,)), ((), ())), preferred_element_type=F32)


def _rms(x, w):
    return x * lax.rsqrt(jnp.mean(x * x, axis=-1, keepdims=True) + EPS) * w


def _sigmoid(x):
    return 1.0 / (1.0 + jnp.exp(-x))


def _silu(x):
    return x * _sigmoid(x)


def _softplus(x):
    return jnp.maximum(x, 0.0) + jnp.log1p(jnp.exp(-jnp.abs(x)))


def _rep(x, n):
    return jnp.concatenate([x] * n, axis=-1)


def _tri(n, lower):
    r = lax.broadcasted_iota(jnp.int32, (n, n), 0)
    c = lax.broadcasted_iota(jnp.int32, (n, n), 1)
    return r >= c if lower else r <= c


def _const_spec(shape):
    nd = len(shape)
    return pl.BlockSpec(shape, lambda *_: (0,) * nd, pipeline_mode=pl.Buffered(1))


def _layer_spec(shape, layer):
    return pl.BlockSpec((None,) + tuple(shape[1:]), lambda *_: (layer, 0, 0), pipeline_mode=pl.Buffered(1))


def _to_bf16_kernel(*refs):
    n = len(refs) // 2
    for src, dst in zip(refs[:n], refs[n:]):
        dst[...] = src[...].astype(BF16)


def _to_bf16(weights, blocks=8):
    specs = [pl.BlockSpec((1, w.shape[1] // blocks, w.shape[2]), lambda l, i: (l, i, 0)) for w in weights]
    return pl.pallas_call(
        _to_bf16_kernel,
        grid=(weights[0].shape[0], blocks),
        in_specs=specs,
        out_specs=specs,
        out_shape=[jax.ShapeDtypeStruct(w.shape, BF16) for w in weights],
        compiler_params=_cparams(("parallel", "parallel")),
        name="to_bf16",
    )(*weights)


MXU_TILE = 256
FFN_CHUNK_ROWS = 512
FFN_BIG_ROWS = 1024


def _ff_chunks(rows):
    if rows <= FFN_CHUNK_ROWS:
        return [(0, D_FF)]
    tiles = D_FF // MXU_TILE
    per = -(-tiles // (2 * rows // FFN_CHUNK_ROWS))
    edges = list(range(0, tiles, per)) + [tiles]
    return [(a * MXU_TILE, b * MXU_TILE) for a, b in zip(edges[:-1], edges[1:])]


def _ffn_kernel(*refs, has_pre, has_final, in_layout, out_layout):
    it = iter(refs)
    x_ref = next(it)
    if has_pre:
        ys_ref, ym_ref, wos_ref, wom_ref = next(it), next(it), next(it), next(it)
    nw_ref, wg_ref, wu_ref, wd_ref = next(it), next(it), next(it), next(it)
    if has_final:
        fw_ref = next(it)
    o_ref = next(it)
    scr = next(it, None)

    if in_layout == "tb":
        ts = x_ref.shape[0]
        pitch = ts + 1
        for t in range(ts):
            for j in range(scr.shape[0]):
                scr[j, pl.ds(t, SUBLANES, stride=pitch), :] = x_ref[t, :, j * LANES:(j + 1) * LANES]
        x = jnp.concatenate(
            [jnp.concatenate([scr[j, b * pitch:b * pitch + ts, :] for b in range(SUBLANES)], axis=0)
             for j in range(scr.shape[0])], axis=1)
    elif in_layout == "bt":
        x = x_ref[...].reshape(-1, D_MODEL)
    else:
        x = x_ref[...]
    if has_pre:
        x = x + _dot(ys_ref[...], wos_ref[...]) + _dot(ym_ref[...], wom_ref[...])
    h = _rms(x, nw_ref[...]).astype(BF16)
    y = None
    for lo, hi in _ff_chunks(x.shape[0]):
        a = (_silu(_dot(h, wg_ref[:, lo:hi])) * _dot(h, wu_ref[:, lo:hi])).astype(BF16)
        part = _dot(a, wd_ref[lo:hi, :])
        y = part if y is None else y + part
    x = x + FFN_RES * y
    if has_final:
        x = _rms(x, fw_ref[...])
    if out_layout == "tb":
        ts = o_ref.shape[0]
        pitch = ts + 1
        for j in range(scr.shape[0]):
            for b in range(SUBLANES):
                scr[j, b * pitch:b * pitch + ts, :] = x[b * ts:(b + 1) * ts, j * LANES:(j + 1) * LANES]
        for t in range(ts):
            for j in range(scr.shape[0]):
                o_ref[t, :, j * LANES:(j + 1) * LANES] = scr[j, pl.ds(t, SUBLANES, stride=pitch), :]
    elif out_layout == "bt":
        o_ref[...] = x.reshape(o_ref.shape)
    else:
        o_ref[...] = x


def _row_spec(layout, tm, n_inner):
    if layout == "bs":
        return pl.BlockSpec((tm, D_MODEL), lambda b, i: (b * n_inner + i, 0))
    ts = tm // SUBLANES
    if layout == "bt":
        return pl.BlockSpec((None, SUBLANES, ts, D_MODEL), lambda b, i: (b, 0, i, 0))
    return pl.BlockSpec((ts, None, SUBLANES, D_MODEL), lambda b, i: (i, b, 0, 0))


def _ffn(x, nw, wg, wu, wd, layer, *, batch, seq, in_layout="bs", out_layout="bs", tm=512, pre=None,
         final_w=None):
    if in_layout == "bs":
        grid = (batch, seq // tm)
        n_inner = seq // tm
    else:
        grid = (batch // SUBLANES, seq * SUBLANES // tm)
        n_inner = None
    args = [x]
    specs = [_row_spec(in_layout, tm, n_inner)]
    if pre is not None:
        ys, ym, wos, wom = pre
        args += [ys, ym, wos, wom]
        specs += [_row_spec("bs", tm, n_inner), _row_spec("bs", tm, n_inner),
                  _const_spec(wos.shape), _const_spec(wom.shape)]
    args += [nw, wg, wu, wd]
    specs += [_const_spec(nw.shape), _layer_spec(wg.shape, layer), _layer_spec(wu.shape, layer),
              _layer_spec(wd.shape, layer)]
    if final_w is not None:
        args.append(final_w)
        specs.append(_const_spec(final_w.shape))
    out_shape = {"bs": (batch * seq, D_MODEL),
                 "bt": (batch // SUBLANES, SUBLANES, seq, D_MODEL),
                 "tb": (seq, batch // SUBLANES, SUBLANES, D_MODEL)}[out_layout]
    scratch = ([pltpu.VMEM((D_MODEL // LANES, tm + SUBLANES, LANES), F32)]
               if "tb" in (in_layout, out_layout) else [])
    return pl.pallas_call(
        functools.partial(_ffn_kernel, has_pre=pre is not None, has_final=final_w is not None,
                          in_layout=in_layout, out_layout=out_layout),
        grid=grid,
        in_specs=specs,
        out_specs=_row_spec(out_layout, tm, n_inner),
        out_shape=jax.ShapeDtypeStruct(out_shape, F32),
        scratch_shapes=scratch,
        compiler_params=_cparams(("parallel", "parallel")),
        name="ffn",
    )(*args)


def _inproj_kernel(x_ref, nw_ref, wm_ref, wdt_ref, wdtT_ref, z_ref, xbc_ref, mx_ref, mz_ref, dt_ref, dtT_ref):
    u = _rms(x_ref[...], nw_ref[...]).astype(BF16)
    o1 = SSD_D_INNER
    o2 = o1 + SSD_CONV_CH
    o3 = o2 + ML_D_INNER
    proj = _dot(u, wm_ref[...])
    z_ref[...] = proj[:, :o1]
    xbc_ref[...] = proj[:, o1:o2]
    mx_ref[...] = proj[:, o2:o3]
    mz_ref[...] = proj[:, o3:]
    dt_ref[...] = _dot(u, wdt_ref[...])
    dtT_ref[...] = _dot_nt(wdtT_ref[...], u)


def _inproj(x, nw, w_main, w_dt, w_dtT, *, tm=512):
    rows = x.shape[0]
    row = lambda w: pl.BlockSpec((tm, w), lambda i: (i, 0))
    consts = [nw, w_main, w_dt, w_dtT]
    widths = [SSD_D_INNER, SSD_CONV_CH, ML_D_INNER, ML_D_INNER, LANES]
    return pl.pallas_call(
        _inproj_kernel,
        grid=(rows // tm,),
        in_specs=[row(D_MODEL)] + [_const_spec(a.shape) for a in consts],
        out_specs=[row(w) for w in widths] + [pl.BlockSpec((SSD_HEADS, tm), lambda i: (0, i))],
        out_shape=[jax.ShapeDtypeStruct((rows, w), F32) for w in widths]
                  + [jax.ShapeDtypeStruct((SSD_HEADS, rows), F32)],
        compiler_params=_cparams(("parallel",)),
        name="inproj",
    )(x, *consts)


def _causal_conv_silu(cbuf, x_ref, w_ref, b_ref):
    L = x_ref.shape[0]
    cbuf[CONV_HALO:CONV_HALO + L, :] = x_ref[...]
    ext = cbuf[...]
    prev = pltpu.roll(ext, 1, axis=0)
    near = ext * w_ref[3:4, :] + prev * w_ref[2:3, :]
    far = ext * w_ref[1:2, :] + prev * w_ref[0:1, :]
    acc = b_ref[...] + near[CONV_HALO:, :] + pltpu.roll(far, 2, axis=0)[CONV_HALO:, :]
    cbuf[0:CONV_HALO, :] = ext[L:L + CONV_HALO, :]
    return _silu(acc)


def _ssd_prepare(xbc_ref, dt_ref, dtT_ref, cw_ref, cb_ref, dtb_ref, dtbT_ref, alog_ref, alogT_ref, cbuf):
    L = CHUNK
    xbc = _causal_conv_silu(cbuf, xbc_ref, cw_ref, cb_ref)
    dt = _softplus(dt_ref[...] + dtb_ref[...])
    adt = dt * (-jnp.exp(alog_ref[...]))
    dtT = _softplus(dtT_ref[...] + dtbT_ref[...])
    adtT = dtT * (-jnp.exp(alogT_ref[...]))
    a_cs = _dot_f32(_tri(L, True).astype(F32), adt)
    a_csT = _dot_f32(adtT, _tri(L, False).astype(F32))
    w_st = dt * jnp.exp(a_cs[L - 1:L, :] - a_cs)
    return dict(xs=xbc[:, :SSD_D_INNER], bm=xbc[:, SSD_D_INNER:SSD_D_INNER + SSD_BC],
                cm=xbc[:, SSD_D_INNER + SSD_BC:], dtT=dtT, a_cs=a_cs, a_csT=a_csT, w_st=w_st)


def _ssd_group(g, pre, z_ref, dexp_ref, nw_ref, o_ref, st_ref):
    L = CHUNK
    xs, bm, cm, dtT, a_cs, a_csT, w_st = (pre[k] for k in ("xs", "bm", "cm", "dtT", "a_cs", "a_csT", "w_st"))
    tril = _tri(L, True)
    left = lax.broadcasted_iota(jnp.int32, (L, LANES), 1) < SSD_HEAD_DIM

    def col(a, h):
        return jnp.broadcast_to(a[:, h:h + 1], (L, LANES))

    heads_per_group = SSD_HEADS // SSD_GROUPS
    gs = slice(g * SSD_GROUP_W, (g + 1) * SSD_GROUP_W)
    bm_g = bm[:, g * SSD_D_STATE:(g + 1) * SSD_D_STATE]
    cm_g = cm[:, g * SSD_D_STATE:(g + 1) * SSD_D_STATE].astype(BF16)
    cb = _dot_nt(cm_g, bm_g.astype(BF16))
    st_prev = st_ref[:, gs]
    y_off = _dot(cm_g, st_prev.astype(BF16))
    diag, acs_pairs, xdec_pairs = [], [], []
    for hp in range(heads_per_group // 2):
        h0 = g * heads_per_group + 2 * hp
        cols = [col(a_cs, h0), col(a_cs, h0 + 1)]
        ms = []
        for h, acs_col in zip((h0, h0 + 1), cols):
            lm = jnp.exp(jnp.where(tril, acs_col - a_csT[h:h + 1, :], -jnp.inf))
            ms.append((cb * lm * dtT[h:h + 1, :]).astype(BF16))
        slab = xs[:, h0 * SSD_HEAD_DIM:(h0 + 2) * SSD_HEAD_DIM]
        rhs = jnp.concatenate([jnp.where(left, slab, 0.0), jnp.where(left, 0.0, slab)], axis=0)
        diag.append(_dot(jnp.concatenate(ms, axis=1), rhs.astype(BF16)))
        acs_pairs.append(jnp.where(left, cols[0], cols[1]))
        xdec_pairs.append(slab * jnp.where(left, col(w_st, h0), col(w_st, h0 + 1)))
    acs_e = jnp.concatenate(acs_pairs, axis=1)
    y = jnp.concatenate(diag, axis=1) + y_off * jnp.exp(acs_e) + dexp_ref[:, gs] * xs[:, gs]
    xd_dec = jnp.concatenate(xdec_pairs, axis=1).astype(BF16)
    st_ref[:, gs] = st_prev * jnp.exp(acs_e[L - 1:L, :]) + _dot(bm_g.T.astype(BF16), xd_dec)
    v = y * _silu(z_ref[:, gs])
    out = v * lax.rsqrt(jnp.mean(v * v, axis=-1, keepdims=True) + EPS) * nw_ref[:, gs]
    o_ref[:, gs] = out.astype(o_ref.dtype)


def _mlstm_project(mx_ref, cbuf, cw_ref, cb_ref, wq_ref, wk_ref, wv_ref, wif_ref, bif_ref):
    L = CHUNK
    mx = mx_ref[...]
    xc = _causal_conv_silu(cbuf, mx_ref, cw_ref, cb_ref)
    xc_b = xc.astype(BF16)
    mx_b = mx.astype(BF16)
    tile = 2 * LANES
    q = jnp.concatenate([_dot(xc_b[:, t * tile:(t + 1) * tile], wq_ref[t]) for t in range(ML_HEADS)], axis=1)
    k = jnp.concatenate([_dot(xc_b[:, t * tile:(t + 1) * tile], wk_ref[t]) for t in range(ML_HEADS)], axis=1)
    v = jnp.concatenate([_dot(mx_b[:, t * tile:(t + 1) * tile], wv_ref[t]) for t in range(ML_HEADS)], axis=1)
    qkv = jnp.concatenate([q, k, v], axis=1).astype(BF16)
    gates = _dot(qkv, wif_ref[...]) + bif_ref[...]
    gatesT = gates.T[0:2 * ML_HEADS, :]
    logf = -_softplus(-gates)
    logfT = -_softplus(-gatesT)
    bcum = _dot_f32(_tri(L, True).astype(F32), logf)
    bcumT = _dot_f32(logfT, _tri(L, False).astype(F32))
    return dict(xc=xc, qkv=qkv, q=q, k=k, gates=gates, gatesT=gatesT, bcum=bcum, bcumT=bcumT)


def _mlstm_head(h, pre, ct_ref, n_ref, m_ref):
    L = CHUNK
    tril = _tri(L, True)
    qkv, gates, gatesT, bcum, bcumT = pre["qkv"], pre["gates"], pre["gatesT"], pre["bcum"], pre["bcumT"]
    hs = slice(h * ML_HEAD_DIM, (h + 1) * ML_HEAD_DIM)
    qh = qkv[:, hs]
    kh = qkv[:, ML_D_INNER + hs.start:ML_D_INNER + hs.stop]
    vh = qkv[:, 2 * ML_D_INNER + hs.start:2 * ML_D_INNER + hs.stop]
    bc = jnp.broadcast_to(bcum[:, ML_HEADS + h:ML_HEADS + h + 1], (L, LANES))
    ig = jnp.broadcast_to(gates[:, h:h + 1], (L, LANES))
    m_prev = m_ref[0:1, :]
    n_prev = n_ref[0:1, :]
    ct_prev = ct_ref[...]

    dlog = jnp.where(tril, bc - bcumT[ML_HEADS + h:ML_HEADS + h + 1, :] + gatesT[h:h + 1, :], -jnp.inf)
    m_inter = bc + m_prev
    m_t = jnp.maximum(jnp.max(dlog, axis=-1, keepdims=True), m_inter)
    scores = _dot_nt(qh, kh) * jnp.exp(dlog - m_t)
    inter_w = jnp.exp(m_inter - m_t)
    num = _dot(scores.astype(BF16), vh) + _rep(inter_w, 2) * _dot(qh, ct_prev.astype(BF16))
    den = (jnp.sum(scores, axis=-1, keepdims=True)
           + inter_w * jnp.sum(pre["q"][:, hs] * n_prev, axis=-1, keepdims=True))
    hout = num / _rep(jnp.maximum(jnp.abs(den), jnp.exp(-m_t)), 2)

    b_last = bc[L - 1:L, :]
    w_state = b_last - bc + ig
    m_new = jnp.maximum(b_last + m_prev, jnp.max(w_state, axis=0, keepdims=True))
    decay = _rep(jnp.exp(b_last + m_prev - m_new), 2)
    kw = pre["k"][:, hs] * _rep(jnp.exp(w_state - m_new), 2)
    ct_ref[...] = decay * ct_prev + _dot(kw.T.astype(BF16), vh)
    n_ref[...] = jnp.broadcast_to(decay * n_prev + jnp.sum(kw, axis=0, keepdims=True), n_ref.shape)
    m_ref[...] = jnp.broadcast_to(m_new, m_ref.shape)

    mu = jnp.mean(hout, axis=-1, keepdims=True)
    d = hout - mu
    var = jnp.mean(d * d, axis=-1, keepdims=True)
    return d * lax.rsqrt(var + EPS)


N_SSD_CONSTS = 8
N_ML_CONSTS = 9


def _mixer_kernel(*refs):
    n = SEQS_PER_STEP
    it = iter(refs)
    take = lambda k: [next(it) for _ in range(k)]
    xbc_ref, dt_ref = take(2)
    dtT_refs = take(n)
    (z_ref,) = take(1)
    ssd_consts = take(N_SSD_CONSTS)
    mx_ref, mz_ref = take(2)
    cw_ref, cb_ref, wq_ref, wk_ref, wv_ref, wif_ref, bif_ref, nw_ref, skip_ref = take(N_ML_CONSTS)
    ys_ref, ym_ref, cbuf_s, st_ref, cbuf_m, ct_ref, n_ref, m_ref = take(8)

    @pl.when(pl.program_id(1) == 0)
    def _():
        st_ref[...] = jnp.zeros(st_ref.shape, F32)
        cbuf_s[:, 0:CONV_HALO, :] = jnp.zeros((n, CONV_HALO, cbuf_s.shape[2]), F32)
        ct_ref[...] = jnp.zeros(ct_ref.shape, F32)
        n_ref[...] = jnp.zeros(n_ref.shape, F32)
        m_ref[...] = jnp.full(m_ref.shape, -1e30, F32)
        cbuf_m[:, 0:CONV_HALO, :] = jnp.zeros((n, CONV_HALO, cbuf_m.shape[2]), F32)

    scw_ref, scb_ref, dtb_ref, dtbT_ref, alog_ref, alogT_ref, dexp_ref, snw_ref = ssd_consts
    pre, ssd_pre = [], []
    for s in range(n):
        pre.append(_mlstm_project(mx_ref.at[s], cbuf_m.at[s], cw_ref, cb_ref, wq_ref, wk_ref, wv_ref,
                                  wif_ref, bif_ref))
        ssd_pre.append(_ssd_prepare(xbc_ref.at[s], dt_ref.at[s], dtT_refs[s], scw_ref, scb_ref,
                                    dtb_ref, dtbT_ref, alog_ref, alogT_ref, cbuf_s.at[s]))
    heads = [[] for _ in range(n)]
    for h in range(ML_HEADS):
        for s in range(n):
            if h < SSD_GROUPS:
                _ssd_group(h, ssd_pre[s], z_ref.at[s], dexp_ref, snw_ref, ys_ref.at[s], st_ref.at[s])
            heads[s].append(_mlstm_head(h, pre[s], ct_ref.at[s, h], n_ref.at[s, h], m_ref.at[s, h]))
    for s in range(n):
        hm = jnp.concatenate(heads[s], axis=1) * nw_ref[...]
        ym_ref[s] = ((hm + skip_ref[...] * pre[s]["xc"]) * _silu(mz_ref[s])).astype(ym_ref.dtype)


def _mixer(xbc, dt, dtT, z, ssd_consts, mx, mz, ml_consts, *, batch, seq):
    L = CHUNK
    nc = seq // L
    n = SEQS_PER_STEP
    seqs = lambda t: t.reshape(batch // n, n, seq, t.shape[-1])
    row = lambda w: pl.BlockSpec((None, n, L, w), lambda b, c: (b, 0, c, 0))
    dtT_spec = lambda s: pl.BlockSpec((SSD_HEADS, L), lambda b, c: (0, (b * n + s) * nc + c))
    assert len(ssd_consts) == N_SSD_CONSTS and len(ml_consts) == N_ML_CONSTS
    y_shape = jax.ShapeDtypeStruct((batch // n, n, seq, D_MODEL), BF16)
    ys, ym = pl.pallas_call(
        _mixer_kernel,
        grid=(batch // n, nc),
        in_specs=[row(SSD_CONV_CH), row(LANES)] + [dtT_spec(s) for s in range(n)] + [row(SSD_D_INNER)]
                 + [_const_spec(a.shape) for a in ssd_consts]
                 + [row(ML_D_INNER)] * 2 + [_const_spec(a.shape) for a in ml_consts],
        out_specs=[row(SSD_D_INNER), row(ML_D_INNER)],
        out_shape=[y_shape, y_shape],
        scratch_shapes=[pltpu.VMEM((n, CONV_HALO + L, SSD_CONV_CH), F32),
                        pltpu.VMEM((n, SSD_D_STATE, SSD_D_INNER), F32),
                        pltpu.VMEM((n, CONV_HALO + L, ML_D_INNER), F32),
                        pltpu.VMEM((n, ML_HEADS, ML_HEAD_DIM, ML_HEAD_DIM), F32),
                        pltpu.VMEM((n, ML_HEADS, SUBLANES, ML_HEAD_DIM), F32),
                        pltpu.VMEM((n, ML_HEADS, SUBLANES, LANES), F32)],
        compiler_params=_cparams(("parallel", "arbitrary")),
        name="mixer",
    )(seqs(xbc), seqs(dt), *([dtT] * n), seqs(z), *ssd_consts, seqs(mx), seqs(mz), *ml_consts)
    return ys.reshape(batch * seq, D_MODEL), ym.reshape(batch * seq, D_MODEL)


def _s5_param_kernel(are_ref, aim_ref, lstep_ref, bre_ref, bim_ref, cre_ref, cim_ref, bs_ref, cs_ref, lam_ref):
    a_re, a_im = are_ref[...], aim_ref[...]
    step = jnp.exp(lstep_ref[...])
    mag = jnp.exp(a_re * step)
    lam_re = mag * jnp.cos(a_im * step)
    lam_im = mag * jnp.sin(a_im * step)
    den = a_re * a_re + a_im * a_im
    coef_re = ((lam_re - 1.0) * a_re + lam_im * a_im) / den
    coef_im = (lam_im * a_re - (lam_re - 1.0) * a_im) / den
    b_re, b_im = bre_ref[...], bim_ref[...]
    bb_re = coef_re * b_re - coef_im * b_im
    bb_im = coef_re * b_im + coef_im * b_re

    rows, wide = S5_SLAB_IN, S5_SLAB_ST
    r = lax.broadcasted_iota(jnp.int32, (rows, wide), 0)
    c = lax.broadcasted_iota(jnp.int32, (rows, wide), 1)
    same_group = r // S5_GROUP == c // S5_STATE
    sel = (lax.broadcasted_iota(jnp.int32, (S5_STATE, wide), 1) % S5_STATE
           == lax.broadcasted_iota(jnp.int32, (S5_STATE, wide), 0))

    def spread(t, exact):
        if exact:
            wide_t = _dot_f32(t, sel.astype(F32))
        else:
            wide_t = _dot(t.astype(BF16), sel.astype(BF16))
        return jnp.where(same_group, wide_t, 0.0)

    bs_ref[...] = jnp.concatenate([spread(bb_re, False), spread(bb_im, False)], axis=1).astype(BF16)
    cs_ref[...] = jnp.concatenate([spread(cre_ref[...], False).T, spread(-cim_ref[...], False).T],
                                  axis=0).astype(BF16)
    first = r % S5_GROUP == 0
    lam = jnp.concatenate([jnp.sum(jnp.where(first, spread(lam_re, True), 0.0), axis=0, keepdims=True),
                           jnp.sum(jnp.where(first, spread(lam_im, True), 0.0), axis=0, keepdims=True)], axis=1)
    lam_ref[...] = jnp.broadcast_to(lam, lam_ref.shape)


def _s5_params(a_re, a_im, log_step, b_re, b_im, c_re, c_im):
    rep = lambda t: jnp.repeat(t, S5_GROUP, axis=0)
    are, aim = rep(a_re), rep(a_im)
    lstep = jnp.broadcast_to(rep(log_step[:, None]), are.shape)
    flat = lambda t: t.reshape(D_MODEL, S5_STATE)
    args = [are, aim, lstep, flat(jnp.swapaxes(b_re, 1, 2)), flat(jnp.swapaxes(b_im, 1, 2)),
            flat(c_re), flat(c_im)]
    bs, cs, lam = pl.pallas_call(
        _s5_param_kernel,
        grid=(S5_SLABS,),
        in_specs=[pl.BlockSpec((S5_SLAB_IN, S5_STATE), lambda k: (k, 0))] * len(args),
        out_specs=[pl.BlockSpec((None, S5_SLAB_IN, 2 * S5_SLAB_ST), lambda k: (k, 0, 0)),
                   pl.BlockSpec((None, 2 * S5_SLAB_ST, S5_SLAB_IN), lambda k: (k, 0, 0)),
                   pl.BlockSpec((None, SUBLANES, 2 * S5_SLAB_ST), lambda k: (k, 0, 0))],
        out_shape=[jax.ShapeDtypeStruct((S5_SLABS, S5_SLAB_IN, 2 * S5_SLAB_ST), BF16),
                   jax.ShapeDtypeStruct((S5_SLABS, 2 * S5_SLAB_ST, S5_SLAB_IN), BF16),
                   jax.ShapeDtypeStruct((S5_SLABS, SUBLANES, 2 * S5_SLAB_ST), F32)],
        compiler_params=_cparams(("parallel",)),
        name="s5_params",
    )(*args)
    return bs, lam, cs


def _s5_kernel(x_ref, nw_ref, bs_ref, lam_ref, cs_ref, dsk_ref, wa_ref, ba_ref, wb_ref, bb_ref, o_ref,
               bu_ref, xs_ref, st_ref, *, batch):
    rows = x_ref.shape[0]
    steps = rows // batch

    @pl.when(pl.program_id(0) == 0)
    def _():
        st_ref[...] = jnp.zeros(st_ref.shape, F32)

    x = x_ref[...]
    u = _rms(x, nw_ref[...])
    u_b = u.astype(BF16)
    def bu_slab(k):
        bu_ref[k % 2] = _dot(u_b[:, k * S5_SLAB_IN:(k + 1) * S5_SLAB_IN], bs_ref[k])

    ys = []
    bu_slab(0)
    for k in range(S5_SLABS):
        if k + 1 < S5_SLABS:
            bu_slab(k + 1)
        for w in range(S5_SLAB_ST // S5_SCAN_W):
            re_l = slice(w * S5_SCAN_W, (w + 1) * S5_SCAN_W)
            im_l = slice(S5_SLAB_ST + w * S5_SCAN_W, S5_SLAB_ST + (w + 1) * S5_SCAN_W)
            lr = jnp.broadcast_to(lam_ref[k, 0:1, re_l], (batch, S5_SCAN_W))
            li = jnp.broadcast_to(lam_ref[k, 0:1, im_l], (batch, S5_SCAN_W))
            xr, xi = st_ref[k, :, re_l], st_ref[k, :, im_l]
            for t in range(steps):
                rows_t = slice(t * batch, (t + 1) * batch)
                xr, xi = (lr * xr - li * xi + bu_ref[k % 2, rows_t, re_l],
                          lr * xi + li * xr + bu_ref[k % 2, rows_t, im_l])
                xs_ref[k % 2, rows_t, re_l] = xr.astype(BF16)
                xs_ref[k % 2, rows_t, im_l] = xi.astype(BF16)
            st_ref[k, :, re_l] = xr
            st_ref[k, :, im_l] = xi
        ys.append(_dot(xs_ref[k % 2], cs_ref[k]))
    y = jnp.concatenate(ys, axis=1) + dsk_ref[...] * u
    g = jax.nn.gelu(y).astype(BF16)
    out = (_dot(g, wa_ref[...]) + ba_ref[...]) * _sigmoid(_dot(g, wb_ref[...]) + bb_ref[...])
    o_ref[...] = x + out


def _s5(x_sb, nw, b_slabs, lam_slabs, c_slabs, dsk, wa, ba, wb, bb, *, batch, seq, ts=32):
    rows = ts * batch
    consts = [nw, b_slabs, lam_slabs, c_slabs, dsk, wa, ba, wb, bb]
    return pl.pallas_call(
        functools.partial(_s5_kernel, batch=batch),
        grid=(seq // ts,),
        in_specs=[pl.BlockSpec((rows, D_MODEL), lambda i: (i, 0))] + [_const_spec(a.shape) for a in consts],
        out_specs=pl.BlockSpec((rows, D_MODEL), lambda i: (i, 0)),
        out_shape=jax.ShapeDtypeStruct((seq * batch, D_MODEL), F32),
        scratch_shapes=[pltpu.VMEM((2, rows, 2 * S5_SLAB_ST), F32),
                        pltpu.VMEM((2, rows, 2 * S5_SLAB_ST), BF16),
                        pltpu.VMEM((S5_SLABS, batch, 2 * S5_SLAB_ST), F32)],
        compiler_params=_cparams(("arbitrary",)),
        name="s5",
    )(x_sb, *consts)


def _blockdiag_tiles(w):
    nb = w.shape[0]
    tile = nb * ML_QKV_BLOCK // ML_HEADS
    rows = jnp.swapaxes(w, 1, 2).reshape(nb * ML_QKV_BLOCK, ML_QKV_BLOCK)
    sel = (jnp.arange(tile)[None, :] % ML_QKV_BLOCK == jnp.arange(ML_QKV_BLOCK)[:, None]).astype(w.dtype)
    wide = jnp.dot(rows, sel, precision=HIGHEST)
    blk_r = (jnp.arange(nb * ML_QKV_BLOCK) % tile) // ML_QKV_BLOCK
    blk_c = jnp.arange(tile) // ML_QKV_BLOCK
    wide = jnp.where(blk_r[:, None] == blk_c[None, :], wide, 0.0)
    return wide.reshape(ML_HEADS, tile, tile)


def _pad_lanes(t, n=LANES):
    return jnp.pad(t, ((0, 0), (0, n - t.shape[1])))


def kernel(x, ffn1_norm, ffn1_w_gate, ffn1_w_up, ffn1_w_down, mix_norm, ffn2_norm, ffn2_w_gate, ffn2_w_up, ffn2_w_down, hy_w_in, ssd_conv_w, ssd_conv_b, ssd_dt_bias, ssd_a_log, ssd_d, ssd_norm_w, ml_conv_w, ml_conv_b, ml_w_q, ml_w_k, ml_w_v, ml_w_if, ml_b_if, ml_norm_w, ml_skip, hy_w_out, s5_a_re, s5_a_im, s5_log_step, s5_b_re, s5_b_im, s5_c_re, s5_c_im, s5_d, s5_w_a, s5_b_a, s5_w_b, s5_b_b, final_norm):
    batch, seq, _ = x.shape
    assert seq % 512 == 0 and batch % 8 == 0
    row = lambda t: t.reshape(1, -1)
    bf = lambda t: t.astype(BF16)
    xf = x.reshape(batch * seq, D_MODEL)

    g1, u1, d1, g2, u2, d2 = _to_bf16([ffn1_w_gate, ffn1_w_up, ffn1_w_down, ffn2_w_gate, ffn2_w_up, ffn2_w_down])
    x1 = _ffn(xf, row(ffn1_norm[0]), g1, u1, d1, 0, batch=batch, seq=seq, tm=FFN_BIG_ROWS)

    w_in = hy_w_in[0]
    o1 = SSD_D_INNER
    o2 = o1 + SSD_CONV_CH
    o3 = o2 + SSD_HEADS
    w_main = bf(jnp.concatenate([w_in[:, :o2], w_in[:, o3:]], axis=1))
    w_dt = w_in[:, o2:o3]
    z_s, xbc, m_x, m_z, dt_raw, dt_rawT = _inproj(x1, row(mix_norm[0]), w_main, bf(_pad_lanes(w_dt)),
                                                       bf(w_dt.T))

    ssd_consts = [ssd_conv_w[0], row(ssd_conv_b[0]),
                  _pad_lanes(row(ssd_dt_bias[0])), ssd_dt_bias[0].reshape(-1, 1),
                  _pad_lanes(row(ssd_a_log[0])), ssd_a_log[0].reshape(-1, 1),
                  row(jnp.repeat(ssd_d[0], SSD_HEAD_DIM)), row(ssd_norm_w[0])]

    k_scale = 1.0 / math.sqrt(ML_HEAD_DIM)
    w_if = ml_w_if[0]
    w_if = jnp.concatenate([w_if[:ML_D_INNER], w_if[ML_D_INNER:2 * ML_D_INNER] / k_scale, w_if[2 * ML_D_INNER:]],
                           axis=0)
    b_if = ml_b_if[0]
    ml_consts = [ml_conv_w[0], row(ml_conv_b[0]),
                 bf(_blockdiag_tiles(ml_w_q[0])), bf(_blockdiag_tiles(ml_w_k[0]) * k_scale),
                 bf(_blockdiag_tiles(ml_w_v[0])), bf(_pad_lanes(w_if)),
                 _pad_lanes(row(b_if)), row(ml_norm_w[0]), row(ml_skip[0])]
    y_ssd, y_ml = _mixer(xbc, dt_raw, dt_rawT, z_s, ssd_consts, m_x, m_z, ml_consts, batch=batch, seq=seq)

    w_out = bf(hy_w_out[0])
    x3 = _ffn(x1, row(ffn2_norm[0]), g2, u2, d2, 0, batch=batch, seq=seq,
              pre=(y_ssd, y_ml, w_out[:SSD_D_INNER], w_out[SSD_D_INNER:]))

    x4 = _ffn(x3.reshape(batch // SUBLANES, SUBLANES, seq, D_MODEL), row(ffn1_norm[1]), g1, u1, d1, 1,
              batch=batch, seq=seq, in_layout="bt", out_layout="tb", tm=FFN_BIG_ROWS)
    b_slabs, lam_slabs, c_slabs = _s5_params(s5_a_re[0], s5_a_im[0], s5_log_step[0], s5_b_re[0], s5_b_im[0],
                                             s5_c_re[0], s5_c_im[0])
    x5 = _s5(x4.reshape(seq * batch, D_MODEL), row(mix_norm[1]), b_slabs, lam_slabs, c_slabs, row(s5_d[0]),
             bf(s5_w_a[0]), row(s5_b_a[0]), bf(s5_w_b[0]), row(s5_b_b[0]), batch=batch, seq=seq)
    out = _ffn(x5.reshape(seq, batch // SUBLANES, SUBLANES, D_MODEL), row(ffn2_norm[1]), g2, u2, d2, 1,
               batch=batch, seq=seq, in_layout="tb", out_layout="bt", final_w=row(final_norm), tm=FFN_BIG_ROWS)
    return out.reshape(batch, seq, D_MODEL)
```

```python
import functools
import math

import jax
import jax.numpy as jnp
from jax import lax
from jax.experimental import pallas as pl
from jax.experimental.pallas import tpu as pltpu

F32 = jnp.float32
BF16 = jnp.bfloat16
HIGHEST = lax.Precision.HIGHEST

D_MODEL = 1024
EPS = 1e-6
D_FF = 2816
FFN_RES = 0.5
CONV_W = 4
CONV_HALO = 8

SSD_HEADS = 16
SSD_HEAD_DIM = 64
SSD_GROUPS = 2
SSD_D_STATE = 128
SSD_D_INNER = SSD_HEADS * SSD_HEAD_DIM
SSD_BC = SSD_GROUPS * SSD_D_STATE
SSD_CONV_CH = SSD_D_INNER + 2 * SSD_BC
SSD_GROUP_W = SSD_D_INNER // SSD_GROUPS

ML_HEADS = 4
ML_HEAD_DIM = 256
ML_D_INNER = ML_HEADS * ML_HEAD_DIM
ML_QKV_BLOCK = 4

CHUNK = 128
LANES = 128
SUBLANES = 8
SEQS_PER_STEP = 4

S5_GROUP = 16
S5_GROUPS = D_MODEL // S5_GROUP
S5_STATE = 64
S5_SLAB_GROUPS = 16
S5_SLABS = S5_GROUPS // S5_SLAB_GROUPS
S5_SLAB_IN = S5_SLAB_GROUPS * S5_GROUP
S5_SLAB_ST = S5_SLAB_GROUPS * S5_STATE
S5_SCAN_W = 512

VMEM_LIMIT_BYTES = 56 * 1024 * 1024


def _cparams(sem):
    return pltpu.CompilerParams(dimension_semantics=sem, vmem_limit_bytes=VMEM_LIMIT_BYTES)


def _dot(a, b):
    return jnp.dot(a, b, preferred_element_type=F32)


def _dot_f32(a, b):
    return jnp.dot(a, b, preferred_element_type=F32, precision=HIGHEST)


def _dot_nt(a, b):
    return lax.dot_general(a, b, (((1,), (1,)), ((), ())), preferred_element_type=F32)


def _rms(x, w):
    return x * lax.rsqrt(jnp.mean(x * x, axis=-1, keepdims=True) + EPS) * w


def _sigmoid(x):
    return 1.0 / (1.0 + jnp.exp(-x))


def _silu(x):
    return x * _sigmoid(x)


def _softplus(x):
    return jnp.maximum(x, 0.0) + jnp.log1p(jnp.exp(-jnp.abs(x)))


def _rep(x, n):
    return jnp.concatenate([x] * n, axis=-1)


def _tri(n, lower):
    r = lax.broadcasted_iota(jnp.int32, (n, n), 0)
    c = lax.broadcasted_iota(jnp.int32, (n, n), 1)
    return r >= c if lower else r <= c


def _const_spec(shape):
    nd = len(shape)
    return pl.BlockSpec(shape, lambda *_: (0,) * nd, pipeline_mode=pl.Buffered(1))


def _layer_spec(shape, layer):
    return pl.BlockSpec((None,) + tuple(shape[1:]), lambda *_: (layer, 0, 0), pipeline_mode=pl.Buffered(1))


def _to_bf16_kernel(*refs):
    n = len(refs) // 2
    for src, dst in zip(refs[:n], refs[n:]):
        dst[...] = src[...].astype(BF16)


def _to_bf16(weights, blocks=8):
    specs = [pl.BlockSpec((1, w.shape[1] // blocks, w.shape[2]), lambda l, i: (l, i, 0)) for w in weights]
    return pl.pallas_call(
        _to_bf16_kernel,
        grid=(weights[0].shape[0], blocks),
        in_specs=specs,
        out_specs=specs,
        out_shape=[jax.ShapeDtypeStruct(w.shape, BF16) for w in weights],
        compiler_params=_cparams(("parallel", "parallel")),
        name="to_bf16",
    )(*weights)


MXU_TILE = 256
FFN_CHUNK_ROWS = 512
FFN_BIG_ROWS = 1024


def _ff_chunks(rows):
    if rows <= FFN_CHUNK_ROWS:
        return [(0, D_FF)]
    tiles = D_FF // MXU_TILE
    per = -(-tiles // (2 * rows // FFN_CHUNK_ROWS))
    edges = list(range(0, tiles, per)) + [tiles]
    return [(a * MXU_TILE, b * MXU_TILE) for a, b in zip(edges[:-1], edges[1:])]


def _ffn_kernel(*refs, has_pre, has_final, in_layout, out_layout):
    it = iter(refs)
    x_ref = next(it)
    if has_pre:
        ys_ref, ym_ref, wos_ref, wom_ref = next(it), next(it), next(it), next(it)
    nw_ref, wg_ref, wu_ref, wd_ref = next(it), next(it), next(it), next(it)
    if has_final:
        fw_ref = next(it)
    o_ref = next(it)
    scr = next(it, None)

    if in_layout == "tb":
        ts = x_ref.shape[0]
        pitch = ts + 1
        for t in range(ts):
            for j in range(scr.shape[0]):
                scr[j, pl.ds(t, SUBLANES, stride=pitch), :] = x_ref[t, :, j * LANES:(j + 1) * LANES]
        x = jnp.concatenate(
            [jnp.concatenate([scr[j, b * pitch:b * pitch + ts, :] for b in range(SUBLANES)], axis=0)
             for j in range(scr.shape[0])], axis=1)
    elif in_layout == "bt":
        x = x_ref[...].reshape(-1, D_MODEL)
    else:
        x = x_ref[...]
    if has_pre:
        x = x + _dot(ys_ref[...], wos_ref[...]) + _dot(ym_ref[...], wom_ref[...])
    h = _rms(x, nw_ref[...]).astype(BF16)
    y = None
    for lo, hi in _ff_chunks(x.shape[0]):
        a = (_silu(_dot(h, wg_ref[:, lo:hi])) * _dot(h, wu_ref[:, lo:hi])).astype(BF16)
        part = _dot(a, wd_ref[lo:hi, :])
        y = part if y is None else y + part
    x = x + FFN_RES * y
    if has_final:
        x = _rms(x, fw_ref[...])
    if out_layout == "tb":
        ts = o_ref.shape[0]
        pitch = ts + 1
        for j in range(scr.shape[0]):
            for b in range(SUBLANES):
                scr[j, b * pitch:b * pitch + ts, :] = x[b * ts:(b + 1) * ts, j * LANES:(j + 1) * LANES]
        for t in range(ts):
            for j in range(scr.shape[0]):
                o_ref[t, :, j * LANES:(j + 1) * LANES] = scr[j, pl.ds(t, SUBLANES, stride=pitch), :]
    elif out_layout == "bt":
        o_ref[...] = x.reshape(o_ref.shape)
    else:
        o_ref[...] = x


def _row_spec(layout, tm, n_inner):
    if layout == "bs":
        return pl.BlockSpec((tm, D_MODEL), lambda b, i: (b * n_inner + i, 0))
    ts = tm // SUBLANES
    if layout == "bt":
        return pl.BlockSpec((None, SUBLANES, ts, D_MODEL), lambda b, i: (b, 0, i, 0))
    return pl.BlockSpec((ts, None, SUBLANES, D_MODEL), lambda b, i: (i, b, 0, 0))


def _ffn(x, nw, wg, wu, wd, layer, *, batch, seq, in_layout="bs", out_layout="bs", tm=512, pre=None,
         final_w=None):
    if in_layout == "bs":
        grid = (batch, seq // tm)
        n_inner = seq // tm
    else:
        grid = (batch // SUBLANES, seq * SUBLANES // tm)
        n_inner = None
    args = [x]
    specs = [_row_spec(in_layout, tm, n_inner)]
    if pre is not None:
        ys, ym, wos, wom = pre
        args += [ys, ym, wos, wom]
        specs += [_row_spec("bs", tm, n_inner), _row_spec("bs", tm, n_inner),
                  _const_spec(wos.shape), _const_spec(wom.shape)]
    args += [nw, wg, wu, wd]
    specs += [_const_spec(nw.shape), _layer_spec(wg.shape, layer), _layer_spec(wu.shape, layer),
              _layer_spec(wd.shape, layer)]
    if final_w is not None:
        args.append(final_w)
        specs.append(_const_spec(final_w.shape))
    out_shape = {"bs": (batch * seq, D_MODEL),
                 "bt": (batch // SUBLANES, SUBLANES, seq, D_MODEL),
                 "tb": (seq, batch // SUBLANES, SUBLANES, D_MODEL)}[out_layout]
    scratch = ([pltpu.VMEM((D_MODEL // LANES, tm + SUBLANES, LANES), F32)]
               if "tb" in (in_layout, out_layout) else [])
    return pl.pallas_call(
        functools.partial(_ffn_kernel, has_pre=pre is not None, has_final=final_w is not None,
                          in_layout=in_layout, out_layout=out_layout),
        grid=grid,
        in_specs=specs,
        out_specs=_row_spec(out_layout, tm, n_inner),
        out_shape=jax.ShapeDtypeStruct(out_shape, F32),
        scratch_shapes=scratch,
        compiler_params=_cparams(("parallel", "parallel")),
        name="ffn",
    )(*args)


def _inproj_kernel(x_ref, nw_ref, wm_ref, wdt_ref, wdtT_ref, z_ref, xbc_ref, mx_ref, mz_ref, dt_ref, dtT_ref):
    u = _rms(x_ref[...], nw_ref[...]).astype(BF16)
    o1 = SSD_D_INNER
    o2 = o1 + SSD_CONV_CH
    o3 = o2 + ML_D_INNER
    proj = _dot(u, wm_ref[...])
    z_ref[...] = proj[:, :o1]
    xbc_ref[...] = proj[:, o1:o2]
    mx_ref[...] = proj[:, o2:o3]
    mz_ref[...] = proj[:, o3:]
    dt_ref[...] = _dot(u, wdt_ref[...])
    dtT_ref[...] = _dot_nt(wdtT_ref[...], u)


def _inproj(x, nw, w_main, w_dt, w_dtT, *, tm=512):
    rows = x.shape[0]
    row = lambda w: pl.BlockSpec((tm, w), lambda i: (i, 0))
    consts = [nw, w_main, w_dt, w_dtT]
    widths = [SSD_D_INNER, SSD_CONV_CH, ML_D_INNER, ML_D_INNER, LANES]
    return pl.pallas_call(
        _inproj_kernel,
        grid=(rows // tm,),
        in_specs=[row(D_MODEL)] + [_const_spec(a.shape) for a in consts],
        out_specs=[row(w) for w in widths] + [pl.BlockSpec((SSD_HEADS, tm), lambda i: (0, i))],
        out_shape=[jax.ShapeDtypeStruct((rows, w), F32) for w in widths]
                  + [jax.ShapeDtypeStruct((SSD_HEADS, rows), F32)],
        compiler_params=_cparams(("parallel",)),
        name="inproj",
    )(x, *consts)


def _causal_conv_silu(cbuf, x_ref, w_ref, b_ref):
    L = x_ref.shape[0]
    cbuf[CONV_HALO:CONV_HALO + L, :] = x_ref[...]
    ext = cbuf[...]
    prev = pltpu.roll(ext, 1, axis=0)
    near = ext * w_ref[3:4, :] + prev * w_ref[2:3, :]
    far = ext * w_ref[1:2, :] + prev * w_ref[0:1, :]
    acc = b_ref[...] + near[CONV_HALO:, :] + pltpu.roll(far, 2, axis=0)[CONV_HALO:, :]
    cbuf[0:CONV_HALO, :] = ext[L:L + CONV_HALO, :]
    return _silu(acc)


def _ssd_prepare(xbc_ref, dt_ref, dtT_ref, cw_ref, cb_ref, dtb_ref, dtbT_ref, alog_ref, alogT_ref, cbuf):
    L = CHUNK
    xbc = _causal_conv_silu(cbuf, xbc_ref, cw_ref, cb_ref)
    dt = _softplus(dt_ref[...] + dtb_ref[...])
    adt = dt * (-jnp.exp(alog_ref[...]))
    dtT = _softplus(dtT_ref[...] + dtbT_ref[...])
    adtT = dtT * (-jnp.exp(alogT_ref[...]))
    a_cs = _dot_f32(_tri(L, True).astype(F32), adt)
    a_csT = _dot_f32(adtT, _tri(L, False).astype(F32))
    w_st = dt * jnp.exp(a_cs[L - 1:L, :] - a_cs)
    return dict(xs=xbc[:, :SSD_D_INNER], bm=xbc[:, SSD_D_INNER:SSD_D_INNER + SSD_BC],
                cm=xbc[:, SSD_D_INNER + SSD_BC:], dtT=dtT, a_cs=a_cs, a_csT=a_csT, w_st=w_st)


def _ssd_group(g, pre, z_ref, dexp_ref, nw_ref, o_ref, st_ref):
    L = CHUNK
    xs, bm, cm, dtT, a_cs, a_csT, w_st = (pre[k] for k in ("xs", "bm", "cm", "dtT", "a_cs", "a_csT", "w_st"))
    tril = _tri(L, True)
    left = lax.broadcasted_iota(jnp.int32, (L, LANES), 1) < SSD_HEAD_DIM

    def col(a, h):
        return jnp.broadcast_to(a[:, h:h + 1], (L, LANES))

    heads_per_group = SSD_HEADS // SSD_GROUPS
    gs = slice(g * SSD_GROUP_W, (g + 1) * SSD_GROUP_W)
    bm_g = bm[:, g * SSD_D_STATE:(g + 1) * SSD_D_STATE]
    cm_g = cm[:, g * SSD_D_STATE:(g + 1) * SSD_D_STATE].astype(BF16)
    cb = _dot_nt(cm_g, bm_g.astype(BF16))
    st_prev = st_ref[:, gs]
    y_off = _dot(cm_g, st_prev.astype(BF16))
    diag, acs_pairs, xdec_pairs = [], [], []
    for hp in range(heads_per_group // 2):
        h0 = g * heads_per_group + 2 * hp
        cols = [col(a_cs, h0), col(a_cs, h0 + 1)]
        ms = []
        for h, acs_col in zip((h0, h0 + 1), cols):
            lm = jnp.exp(jnp.where(tril, acs_col - a_csT[h:h + 1, :], -jnp.inf))
            ms.append((cb * lm * dtT[h:h + 1, :]).astype(BF16))
        slab = xs[:, h0 * SSD_HEAD_DIM:(h0 + 2) * SSD_HEAD_DIM]
        rhs = jnp.concatenate([jnp.where(left, slab, 0.0), jnp.where(left, 0.0, slab)], axis=0)
        diag.append(_dot(jnp.concatenate(ms, axis=1), rhs.astype(BF16)))
        acs_pairs.append(jnp.where(left, cols[0], cols[1]))
        xdec_pairs.append(slab * jnp.where(left, col(w_st, h0), col(w_st, h0 + 1)))
    acs_e = jnp.concatenate(acs_pairs, axis=1)
    y = jnp.concatenate(diag, axis=1) + y_off * jnp.exp(acs_e) + dexp_ref[:, gs] * xs[:, gs]
    xd_dec = jnp.concatenate(xdec_pairs, axis=1).astype(BF16)
    st_ref[:, gs] = st_prev * jnp.exp(acs_e[L - 1:L, :]) + _dot(bm_g.T.astype(BF16), xd_dec)
    v = y * _silu(z_ref[:, gs])
    out = v * lax.rsqrt(jnp.mean(v * v, axis=-1, keepdims=True) + EPS) * nw_ref[:, gs]
    o_ref[:, gs] = out.astype(o_ref.dtype)


def _mlstm_project(mx_ref, cbuf, cw_ref, cb_ref, wq_ref, wk_ref, wv_ref, wif_ref, bif_ref):
    L = CHUNK
    mx = mx_ref[...]
    xc = _causal_conv_silu(cbuf, mx_ref, cw_ref, cb_ref)
    xc_b = xc.astype(BF16)
    mx_b = mx.astype(BF16)
    tile = 2 * LANES
    q = jnp.concatenate([_dot(xc_b[:, t * tile:(t + 1) * tile], wq_ref[t]) for t in range(ML_HEADS)], axis=1)
    k = jnp.concatenate([_dot(xc_b[:, t * tile:(t + 1) * tile], wk_ref[t]) for t in range(ML_HEADS)], axis=1)
    v = jnp.concatenate([_dot(mx_b[:, t * tile:(t + 1) * tile], wv_ref[t]) for t in range(ML_HEADS)], axis=1)
    qkv = jnp.concatenate([q, k, v], axis=1).astype(BF16)
    gates = _dot(qkv, wif_ref[...]) + bif_ref[...]
    gatesT = gates.T[0:2 * ML_HEADS, :]
    logf = -_softplus(-gates)
    logfT = -_softplus(-gatesT)
    bcum = _dot_f32(_tri(L, True).astype(F32), logf)
    bcumT = _dot_f32(logfT, _tri(L, False).astype(F32))
    return dict(xc=xc, qkv=qkv, q=q, k=k, gates=gates, gatesT=gatesT, bcum=bcum, bcumT=bcumT)


def _mlstm_head(h, pre, ct_ref, n_ref, m_ref):
    L = CHUNK
    tril = _tri(L, True)
    qkv, gates, gatesT, bcum, bcumT = pre["qkv"], pre["gates"], pre["gatesT"], pre["bcum"], pre["bcumT"]
    hs = slice(h * ML_HEAD_DIM, (h + 1) * ML_HEAD_DIM)
    qh = qkv[:, hs]
    kh = qkv[:, ML_D_INNER + hs.start:ML_D_INNER + hs.stop]
    vh = qkv[:, 2 * ML_D_INNER + hs.start:2 * ML_D_INNER + hs.stop]
    bc = jnp.broadcast_to(bcum[:, ML_HEADS + h:ML_HEADS + h + 1], (L, LANES))
    ig = jnp.broadcast_to(gates[:, h:h + 1], (L, LANES))
    m_prev = m_ref[0:1, :]
    n_prev = n_ref[0:1, :]
    ct_prev = ct_ref[...]

    dlog = jnp.where(tril, bc - bcumT[ML_HEADS + h:ML_HEADS + h + 1, :] + gatesT[h:h + 1, :], -jnp.inf)
    m_inter = bc + m_prev
    m_t = jnp.maximum(jnp.max(dlog, axis=-1, keepdims=True), m_inter)
    scores = _dot_nt(qh, kh) * jnp.exp(dlog - m_t)
    inter_w = jnp.exp(m_inter - m_t)
    num = _dot(scores.astype(BF16), vh) + _rep(inter_w, 2) * _dot(qh, ct_prev.astype(BF16))
    den = (jnp.sum(scores, axis=-1, keepdims=True)
           + inter_w * jnp.sum(pre["q"][:, hs] * n_prev, axis=-1, keepdims=True))
    hout = num / _rep(jnp.maximum(jnp.abs(den), jnp.exp(-m_t)), 2)

    b_last = bc[L - 1:L, :]
    w_state = b_last - bc + ig
    m_new = jnp.maximum(b_last + m_prev, jnp.max(w_state, axis=0, keepdims=True))
    decay = _rep(jnp.exp(b_last + m_prev - m_new), 2)
    kw = pre["k"][:, hs] * _rep(jnp.exp(w_state - m_new), 2)
    ct_ref[...] = decay * ct_prev + _dot(kw.T.astype(BF16), vh)
    n_ref[...] = jnp.broadcast_to(decay * n_prev + jnp.sum(kw, axis=0, keepdims=True), n_ref.shape)
    m_ref[...] = jnp.broadcast_to(m_new, m_ref.shape)

    mu = jnp.mean(hout, axis=-1, keepdims=True)
    d = hout - mu
    var = jnp.mean(d * d, axis=-1, keepdims=True)
    return d * lax.rsqrt(var + EPS)


N_SSD_CONSTS = 8
N_ML_CONSTS = 9


def _mixer_kernel(*refs):
    n = SEQS_PER_STEP
    it = iter(refs)
    take = lambda k: [next(it) for _ in range(k)]
    xbc_ref, dt_ref = take(2)
    dtT_refs = take(n)
    (z_ref,) = take(1)
    ssd_consts = take(N_SSD_CONSTS)
    mx_ref, mz_ref = take(2)
    cw_ref, cb_ref, wq_ref, wk_ref, wv_ref, wif_ref, bif_ref, nw_ref, skip_ref = take(N_ML_CONSTS)
    ys_ref, ym_ref, cbuf_s, st_ref, cbuf_m, ct_ref, n_ref, m_ref = take(8)

    @pl.when(pl.program_id(1) == 0)
    def _():
        st_ref[...] = jnp.zeros(st_ref.shape, F32)
        cbuf_s[:, 0:CONV_HALO, :] = jnp.zeros((n, CONV_HALO, cbuf_s.shape[2]), F32)
        ct_ref[...] = jnp.zeros(ct_ref.shape, F32)
        n_ref[...] = jnp.zeros(n_ref.shape, F32)
        m_ref[...] = jnp.full(m_ref.shape, -1e30, F32)
        cbuf_m[:, 0:CONV_HALO, :] = jnp.zeros((n, CONV_HALO, cbuf_m.shape[2]), F32)

    scw_ref, scb_ref, dtb_ref, dtbT_ref, alog_ref, alogT_ref, dexp_ref, snw_ref = ssd_consts
    pre, ssd_pre = [], []
    for s in range(n):
        pre.append(_mlstm_project(mx_ref.at[s], cbuf_m.at[s], cw_ref, cb_ref, wq_ref, wk_ref, wv_ref,
                                  wif_ref, bif_ref))
        ssd_pre.append(_ssd_prepare(xbc_ref.at[s], dt_ref.at[s], dtT_refs[s], scw_ref, scb_ref,
                                    dtb_ref, dtbT_ref, alog_ref, alogT_ref, cbuf_s.at[s]))
    heads = [[] for _ in range(n)]
    for h in range(ML_HEADS):
        for s in range(n):
            if h < SSD_GROUPS:
                _ssd_group(h, ssd_pre[s], z_ref.at[s], dexp_ref, snw_ref, ys_ref.at[s], st_ref.at[s])
            heads[s].append(_mlstm_head(h, pre[s], ct_ref.at[s, h], n_ref.at[s, h], m_ref.at[s, h]))
    for s in range(n):
        hm = jnp.concatenate(heads[s], axis=1) * nw_ref[...]
        ym_ref[s] = ((hm + skip_ref[...] * pre[s]["xc"]) * _silu(mz_ref[s])).astype(ym_ref.dtype)


def _mixer(xbc, dt, dtT, z, ssd_consts, mx, mz, ml_consts, *, batch, seq):
    L = CHUNK
    nc = seq // L
    n = SEQS_PER_STEP
    seqs = lambda t: t.reshape(batch // n, n, seq, t.shape[-1])
    row = lambda w: pl.BlockSpec((None, n, L, w), lambda b, c: (b, 0, c, 0))
    dtT_spec = lambda s: pl.BlockSpec((SSD_HEADS, L), lambda b, c: (0, (b * n + s) * nc + c))
    assert len(ssd_consts) == N_SSD_CONSTS and len(ml_consts) == N_ML_CONSTS
    y_shape = jax.ShapeDtypeStruct((batch // n, n, seq, D_MODEL), BF16)
    ys, ym = pl.pallas_call(
        _mixer_kernel,
        grid=(batch // n, nc),
        in_specs=[row(SSD_CONV_CH), row(LANES)] + [dtT_spec(s) for s in range(n)] + [row(SSD_D_INNER)]
                 + [_const_spec(a.shape) for a in ssd_consts]
                 + [row(ML_D_INNER)] * 2 + [_const_spec(a.shape) for a in ml_consts],
        out_specs=[row(SSD_D_INNER), row(ML_D_INNER)],
        out_shape=[y_shape, y_shape],
        scratch_shapes=[pltpu.VMEM((n, CONV_HALO + L, SSD_CONV_CH), F32),
                        pltpu.VMEM((n, SSD_D_STATE, SSD_D_INNER), F32),
                        pltpu.VMEM((n, CONV_HALO + L, ML_D_INNER), F32),
                        pltpu.VMEM((n, ML_HEADS, ML_HEAD_DIM, ML_HEAD_DIM), F32),
                        pltpu.VMEM((n, ML_HEADS, SUBLANES, ML_HEAD_DIM), F32),
                        pltpu.VMEM((n, ML_HEADS, SUBLANES, LANES), F32)],
        compiler_params=_cparams(("parallel", "arbitrary")),
        name="mixer",
    )(seqs(xbc), seqs(dt), *([dtT] * n), seqs(z), *ssd_consts, seqs(mx), seqs(mz), *ml_consts)
    return ys.reshape(batch * seq, D_MODEL), ym.reshape(batch * seq, D_MODEL)


def _s5_param_kernel(are_ref, aim_ref, lstep_ref, bre_ref, bim_ref, cre_ref, cim_ref, bs_ref, cs_ref, lam_ref):
    a_re, a_im = are_ref[...], aim_ref[...]
    step = jnp.exp(lstep_ref[...])
    mag = jnp.exp(a_re * step)
    lam_re = mag * jnp.cos(a_im * step)
    lam_im = mag * jnp.sin(a_im * step)
    den = a_re * a_re + a_im * a_im
    coef_re = ((lam_re - 1.0) * a_re + lam_im * a_im) / den
    coef_im = (lam_im * a_re - (lam_re - 1.0) * a_im) / den
    b_re, b_im = bre_ref[...], bim_ref[...]
    bb_re = coef_re * b_re - coef_im * b_im
    bb_im = coef_re * b_im + coef_im * b_re

    rows, wide = S5_SLAB_IN, S5_SLAB_ST
    r = lax.broadcasted_iota(jnp.int32, (rows, wide), 0)
    c = lax.broadcasted_iota(jnp.int32, (rows, wide), 1)
    same_group = r // S5_GROUP == c // S5_STATE
    sel = (lax.broadcasted_iota(jnp.int32, (S5_STATE, wide), 1) % S5_STATE
           == lax.broadcasted_iota(jnp.int32, (S5_STATE, wide), 0))

    def spread(t, exact):
        if exact:
            wide_t = _dot_f32(t, sel.astype(F32))
        else:
            wide_t = _dot(t.astype(BF16), sel.astype(BF16))
        return jnp.where(same_group, wide_t, 0.0)

    bs_ref[...] = jnp.concatenate([spread(bb_re, False), spread(bb_im, False)], axis=1).astype(BF16)
    cs_ref[...] = jnp.concatenate([spread(cre_ref[...], False).T, spread(-cim_ref[...], False).T],
                                  axis=0).astype(BF16)
    first = r % S5_GROUP == 0
    lam = jnp.concatenate([jnp.sum(jnp.where(first, spread(lam_re, True), 0.0), axis=0, keepdims=True),
                           jnp.sum(jnp.where(first, spread(lam_im, True), 0.0), axis=0, keepdims=True)], axis=1)
    lam_ref[...] = jnp.broadcast_to(lam, lam_ref.shape)


def _s5_params(a_re, a_im, log_step, b_re, b_im, c_re, c_im):
    rep = lambda t: jnp.repeat(t, S5_GROUP, axis=0)
    are, aim = rep(a_re), rep(a_im)
    lstep = jnp.broadcast_to(rep(log_step[:, None]), are.shape)
    flat = lambda t: t.reshape(D_MODEL, S5_STATE)
    args = [are, aim, lstep, flat(jnp.swapaxes(b_re, 1, 2)), flat(jnp.swapaxes(b_im, 1, 2)),
            flat(c_re), flat(c_im)]
    bs, cs, lam = pl.pallas_call(
        _s5_param_kernel,
        grid=(S5_SLABS,),
        in_specs=[pl.BlockSpec((S5_SLAB_IN, S5_STATE), lambda k: (k, 0))] * len(args),
        out_specs=[pl.BlockSpec((None, S5_SLAB_IN, 2 * S5_SLAB_ST), lambda k: (k, 0, 0)),
                   pl.BlockSpec((None, 2 * S5_SLAB_ST, S5_SLAB_IN), lambda k: (k, 0, 0)),
                   pl.BlockSpec((None, SUBLANES, 2 * S5_SLAB_ST), lambda k: (k, 0, 0))],
        out_shape=[jax.ShapeDtypeStruct((S5_SLABS, S5_SLAB_IN, 2 * S5_SLAB_ST), BF16),
                   jax.ShapeDtypeStruct((S5_SLABS, 2 * S5_SLAB_ST, S5_SLAB_IN), BF16),
                   jax.ShapeDtypeStruct((S5_SLABS, SUBLANES, 2 * S5_SLAB_ST), F32)],
        compiler_params=_cparams(("parallel",)),
        name="s5_params",
    )(*args)
    return bs, lam, cs


def _s5_kernel(x_ref, nw_ref, bs_ref, lam_ref, cs_ref, dsk_ref, wa_ref, ba_ref, wb_ref, bb_ref, o_ref,
               bu_ref, xs_ref, st_ref, *, batch):
    rows = x_ref.shape[0]
    steps = rows // batch

    @pl.when(pl.program_id(0) == 0)
    def _():
        st_ref[...] = jnp.zeros(st_ref.shape, F32)

    x = x_ref[...]
    u = _rms(x, nw_ref[...])
    u_b = u.astype(BF16)
    def bu_slab(k):
        bu_ref[k % 2] = _dot(u_b[:, k * S5_SLAB_IN:(k + 1) * S5_SLAB_IN], bs_ref[k])

    ys = []
    bu_slab(0)
    for k in range(S5_SLABS):
        if k + 1 < S5_SLABS:
            bu_slab(k + 1)
        for w in range(S5_SLAB_ST // S5_SCAN_W):
            re_l = slice(w * S5_SCAN_W, (w + 1) * S5_SCAN_W)
            im_l = slice(S5_SLAB_ST + w * S5_SCAN_W, S5_SLAB_ST + (w + 1) * S5_SCAN_W)
            lr = jnp.broadcast_to(lam_ref[k, 0:1, re_l], (batch, S5_SCAN_W))
            li = jnp.broadcast_to(lam_ref[k, 0:1, im_l], (batch, S5_SCAN_W))
            xr, xi = st_ref[k, :, re_l], st_ref[k, :, im_l]
            for t in range(steps):
                rows_t = slice(t * batch, (t + 1) * batch)
                xr, xi = (lr * xr - li * xi + bu_ref[k % 2, rows_t, re_l],
                          lr * xi + li * xr + bu_ref[k % 2, rows_t, im_l])
                xs_ref[k % 2, rows_t, re_l] = xr.astype(BF16)
                xs_ref[k % 2, rows_t, im_l] = xi.astype(BF16)
            st_ref[k, :, re_l] = xr
            st_ref[k, :, im_l] = xi
        ys.append(_dot(xs_ref[k % 2], cs_ref[k]))
    y = jnp.concatenate(ys, axis=1) + dsk_ref[...] * u
    g = jax.nn.gelu(y).astype(BF16)
    out = (_dot(g, wa_ref[...]) + ba_ref[...]) * _sigmoid(_dot(g, wb_ref[...]) + bb_ref[...])
    o_ref[...] = x + out


def _s5(x_sb, nw, b_slabs, lam_slabs, c_slabs, dsk, wa, ba, wb, bb, *, batch, seq, ts=32):
    rows = ts * batch
    consts = [nw, b_slabs, lam_slabs, c_slabs, dsk, wa, ba, wb, bb]
    return pl.pallas_call(
        functools.partial(_s5_kernel, batch=batch),
        grid=(seq // ts,),
        in_specs=[pl.BlockSpec((rows, D_MODEL), lambda i: (i, 0))] + [_const_spec(a.shape) for a in consts],
        out_specs=pl.BlockSpec((rows, D_MODEL), lambda i: (i, 0)),
        out_shape=jax.ShapeDtypeStruct((seq * batch, D_MODEL), F32),
        scratch_shapes=[pltpu.VMEM((2, rows, 2 * S5_SLAB_ST), F32),
                        pltpu.VMEM((2, rows, 2 * S5_SLAB_ST), BF16),
                        pltpu.VMEM((S5_SLABS, batch, 2 * S5_SLAB_ST), F32)],
        compiler_params=_cparams(("arbitrary",)),
        name="s5",
    )(x_sb, *consts)


def _blockdiag_tiles(w):
    nb = w.shape[0]
    tile = nb * ML_QKV_BLOCK // ML_HEADS
    rows = jnp.swapaxes(w, 1, 2).reshape(nb * ML_QKV_BLOCK, ML_QKV_BLOCK)
    sel = (jnp.arange(tile)[None, :] % ML_QKV_BLOCK == jnp.arange(ML_QKV_BLOCK)[:, None]).astype(w.dtype)
    wide = jnp.dot(rows, sel, precision=HIGHEST)
    blk_r = (jnp.arange(nb * ML_QKV_BLOCK) % tile) // ML_QKV_BLOCK
    blk_c = jnp.arange(tile) // ML_QKV_BLOCK
    wide = jnp.where(blk_r[:, None] == blk_c[None, :], wide, 0.0)
    return wide.reshape(ML_HEADS, tile, tile)


def _pad_lanes(t, n=LANES):
    return jnp.pad(t, ((0, 0), (0, n - t.shape[1])))


def kernel(x, ffn1_norm, ffn1_w_gate, ffn1_w_up, ffn1_w_down, mix_norm, ffn2_norm, ffn2_w_gate, ffn2_w_up, ffn2_w_down, hy_w_in, ssd_conv_w, ssd_conv_b, ssd_dt_bias, ssd_a_log, ssd_d, ssd_norm_w, ml_conv_w, ml_conv_b, ml_w_q, ml_w_k, ml_w_v, ml_w_if, ml_b_if, ml_norm_w, ml_skip, hy_w_out, s5_a_re, s5_a_im, s5_log_step, s5_b_re, s5_b_im, s5_c_re, s5_c_im, s5_d, s5_w_a, s5_b_a, s5_w_b, s5_b_b, final_norm):
    batch, seq, _ = x.shape
    assert seq % 512 == 0 and batch % 8 == 0
    row = lambda t: t.reshape(1, -1)
    bf = lambda t: t.astype(BF16)
    xf = x.reshape(batch * seq, D_MODEL)

    g1, u1, d1, g2, u2, d2 = _to_bf16([ffn1_w_gate, ffn1_w_up, ffn1_w_down, ffn2_w_gate, ffn2_w_up, ffn2_w_down])
    x1 = _ffn(xf, row(ffn1_norm[0]), g1, u1, d1, 0, batch=batch, seq=seq, tm=FFN_BIG_ROWS)

    w_in = hy_w_in[0]
    o1 = SSD_D_INNER
    o2 = o1 + SSD_CONV_CH
    o3 = o2 + SSD_HEADS
    w_main = bf(jnp.concatenate([w_in[:, :o2], w_in[:, o3:]], axis=1))
    w_dt = w_in[:, o2:o3]
    z_s, xbc, m_x, m_z, dt_raw, dt_rawT = _inproj(x1, row(mix_norm[0]), w_main, bf(_pad_lanes(w_dt)),
                                                       bf(w_dt.T))

    ssd_consts = [ssd_conv_w[0], row(ssd_conv_b[0]),
                  _pad_lanes(row(ssd_dt_bias[0])), ssd_dt_bias[0].reshape(-1, 1),
                  _pad_lanes(row(ssd_a_log[0])), ssd_a_log[0].reshape(-1, 1),
                  row(jnp.repeat(ssd_d[0], SSD_HEAD_DIM)), row(ssd_norm_w[0])]

    k_scale = 1.0 / math.sqrt(ML_HEAD_DIM)
    w_if = ml_w_if[0]
    w_if = jnp.concatenate([w_if[:ML_D_INNER], w_if[ML_D_INNER:2 * ML_D_INNER] / k_scale, w_if[2 * ML_D_INNER:]],
                           axis=0)
    b_if = ml_b_if[0]
    ml_consts = [ml_conv_w[0], row(ml_conv_b[0]),
                 bf(_blockdiag_tiles(ml_w_q[0])), bf(_blockdiag_tiles(ml_w_k[0]) * k_scale),
                 bf(_blockdiag_tiles(ml_w_v[0])), bf(_pad_lanes(w_if)),
                 _pad_lanes(row(b_if)), row(ml_norm_w[0]), row(ml_skip[0])]
    y_ssd, y_ml = _mixer(xbc, dt_raw, dt_rawT, z_s, ssd_consts, m_x, m_z, ml_consts, batch=batch, seq=seq)

    w_out = bf(hy_w_out[0])
    x3 = _ffn(x1, row(ffn2_norm[0]), g2, u2, d2, 0, batch=batch, seq=seq,
              pre=(y_ssd, y_ml, w_out[:SSD_D_INNER], w_out[SSD_D_INNER:]))

    x4 = _ffn(x3.reshape(batch // SUBLANES, SUBLANES, seq, D_MODEL), row(ffn1_norm[1]), g1, u1, d1, 1,
              batch=batch, seq=seq, in_layout="bt", out_layout="tb", tm=FFN_BIG_ROWS)
    b_slabs, lam_slabs, c_slabs = _s5_params(s5_a_re[0], s5_a_im[0], s5_log_step[0], s5_b_re[0], s5_b_im[0],
                                             s5_c_re[0], s5_c_im[0])
    x5 = _s5(x4.reshape(seq * batch, D_MODEL), row(mix_norm[1]), b_slabs, lam_slabs, c_slabs, row(s5_d[0]),
             bf(s5_w_a[0]), row(s5_b_a[0]), bf(s5_w_b[0]), row(s5_b_b[0]), batch=batch, seq=seq)
    out = _ffn(x5.reshape(seq, batch // SUBLANES, SUBLANES, D_MODEL), row(ffn2_norm[1]), g2, u2, d2, 1,
               batch=batch, seq=seq, in_layout="tb", out_layout="bt", final_w=row(final_norm), tm=FFN_BIG_ROWS)
    return out.reshape(batch, seq, D_MODEL)
```

```python
import functools
import math

import jax
import jax.numpy as jnp
from jax import lax
from jax.experimental import pallas as pl
from jax.experimental.pallas import tpu as pltpu

F32 = jnp.float32
BF16 = jnp.bfloat16
HIGHEST = lax.Precision.HIGHEST

D_MODEL = 1024
EPS = 1e-6
D_FF = 2816
FFN_RES = 0.5
CONV_W = 4
CONV_HALO = 8

SSD_HEADS = 16
SSD_HEAD_DIM = 64
SSD_GROUPS = 2
SSD_D_STATE = 128
SSD_D_INNER = SSD_HEADS * SSD_HEAD_DIM
SSD_BC = SSD_GROUPS * SSD_D_STATE
SSD_CONV_CH = SSD_D_INNER + 2 * SSD_BC
SSD_GROUP_W = SSD_D_INNER // SSD_GROUPS

ML_HEADS = 4
ML_HEAD_DIM = 256
ML_D_INNER = ML_HEADS * ML_HEAD_DIM
ML_QKV_BLOCK = 4

CHUNK = 128
LANES = 128
SUBLANES = 8
SEQS_PER_STEP = 4

S5_GROUP = 16
S5_GROUPS = D_MODEL // S5_GROUP
S5_STATE = 64
S5_SLAB_GROUPS = 16
S5_SLABS = S5_GROUPS // S5_SLAB_GROUPS
S5_SLAB_IN = S5_SLAB_GROUPS * S5_GROUP
S5_SLAB_ST = S5_SLAB_GROUPS * S5_STATE
S5_SCAN_W = 512

VMEM_LIMIT_BYTES = 56 * 1024 * 1024


def _cparams(sem):
    return pltpu.CompilerParams(dimension_semantics=sem, vmem_limit_bytes=VMEM_LIMIT_BYTES)


def _dot(a, b):
    return jnp.dot(a, b, preferred_element_type=F32)


def _dot_f32(a, b):
    return jnp.dot(a, b, preferred_element_type=F32, precision=HIGHEST)


def _dot_nt(a, b):
    return lax.dot_general(a, b, (((1,), (1,)), ((), ())), preferred_element_type=F32)


def _rms(x, w):
    return x * lax.rsqrt(jnp.mean(x * x, axis=-1, keepdims=True) + EPS) * w


def _sigmoid(x):
    return 1.0 / (1.0 + jnp.exp(-x))


def _silu(x):
    return x * _sigmoid(x)


def _softplus(x):
    return jnp.maximum(x, 0.0) + jnp.log1p(jnp.exp(-jnp.abs(x)))


def _rep(x, n):
    return jnp.concatenate([x] * n, axis=-1)


def _tri(n, lower):
    r = lax.broadcasted_iota(jnp.int32, (n, n), 0)
    c = lax.broadcasted_iota(jnp.int32, (n, n), 1)
    return r >= c if lower else r <= c


def _const_spec(shape):
    nd = len(shape)
    return pl.BlockSpec(shape, lambda *_: (0,) * nd, pipeline_mode=pl.Buffered(1))


def _layer_spec(shape, layer):
    return pl.BlockSpec((None,) + tuple(shape[1:]), lambda *_: (layer, 0, 0), pipeline_mode=pl.Buffered(1))


def _to_bf16_kernel(*refs):
    n = len(refs) // 2
    for src, dst in zip(refs[:n], refs[n:]):
        dst[...] = src[...].astype(BF16)


def _to_bf16(weights, blocks=8):
    specs = [pl.BlockSpec((1, w.shape[1] // blocks, w.shape[2]), lambda l, i: (l, i, 0)) for w in weights]
    return pl.pallas_call(
        _to_bf16_kernel,
        grid=(weights[0].shape[0], blocks),
        in_specs=specs,
        out_specs=specs,
        out_shape=[jax.ShapeDtypeStruct(w.shape, BF16) for w in weights],
        compiler_params=_cparams(("parallel", "parallel")),
        name="to_bf16",
    )(*weights)


FFN_CHUNK_ROWS = 512
FFN_BIG_ROWS = 1024


def _ffn_kernel(*refs, has_pre, has_final, in_layout, out_layout):
    it = iter(refs)
    x_ref = next(it)
    if has_pre:
        ys_ref, ym_ref, wos_ref, wom_ref = next(it), next(it), next(it), next(it)
    nw_ref, wg_ref, wu_ref, wd_ref = next(it), next(it), next(it), next(it)
    if has_final:
        fw_ref = next(it)
    o_ref = next(it)
    scr = next(it, None)

    if in_layout == "tb":
        ts = x_ref.shape[0]
        pitch = ts + 1
        for t in range(ts):
            for j in range(scr.shape[0]):
                scr[j, pl.ds(t, SUBLANES, stride=pitch), :] = x_ref[t, :, j * LANES:(j + 1) * LANES]
        x = jnp.concatenate(
            [jnp.concatenate([scr[j, b * pitch:b * pitch + ts, :] for b in range(SUBLANES)], axis=0)
             for j in range(scr.shape[0])], axis=1)
    elif in_layout == "bt":
        x = x_ref[...].reshape(-1, D_MODEL)
    else:
        x = x_ref[...]
    if has_pre:
        x = x + _dot(ys_ref[...], wos_ref[...]) + _dot(ym_ref[...], wom_ref[...])
    sub = min(x.shape[0], FFN_CHUNK_ROWS)
    x_parts = [x[r:r + sub] for r in range(0, x.shape[0], sub)]
    h_parts = [_rms(xp, nw_ref[...]).astype(BF16) for xp in x_parts]
    outs = []
    for xp, h in zip(x_parts, h_parts):
        a = (_silu(_dot(h, wg_ref[...])) * _dot(h, wu_ref[...])).astype(BF16)
        xo = xp + FFN_RES * _dot(a, wd_ref[...])
        outs.append(_rms(xo, fw_ref[...]) if has_final else xo)
    x = jnp.concatenate(outs, axis=0) if len(outs) > 1 else outs[0]
    if out_layout == "tb":
        ts = o_ref.shape[0]
        pitch = ts + 1
        for j in range(scr.shape[0]):
            for b in range(SUBLANES):
                scr[j, b * pitch:b * pitch + ts, :] = x[b * ts:(b + 1) * ts, j * LANES:(j + 1) * LANES]
        for t in range(ts):
            for j in range(scr.shape[0]):
                o_ref[t, :, j * LANES:(j + 1) * LANES] = scr[j, pl.ds(t, SUBLANES, stride=pitch), :]
    elif out_layout == "bt":
        o_ref[...] = x.reshape(o_ref.shape)
    else:
        o_ref[...] = x


def _row_spec(layout, tm, n_inner):
    if layout == "bs":
        return pl.BlockSpec((tm, D_MODEL), lambda b, i: (b * n_inner + i, 0))
    ts = tm // SUBLANES
    if layout == "bt":
        return pl.BlockSpec((None, SUBLANES, ts, D_MODEL), lambda b, i: (b, 0, i, 0))
    return pl.BlockSpec((ts, None, SUBLANES, D_MODEL), lambda b, i: (i, b, 0, 0))


def _ffn(x, nw, wg, wu, wd, layer, *, batch, seq, in_layout="bs", out_layout="bs", tm=512, pre=None,
         final_w=None):
    if in_layout == "bs":
        grid = (batch, seq // tm)
        n_inner = seq // tm
    else:
        grid = (batch // SUBLANES, seq * SUBLANES // tm)
        n_inner = None
    args = [x]
    specs = [_row_spec(in_layout, tm, n_inner)]
    if pre is not None:
        ys, ym, wos, wom = pre
        args += [ys, ym, wos, wom]
        specs += [_row_spec("bs", tm, n_inner), _row_spec("bs", tm, n_inner),
                  _const_spec(wos.shape), _const_spec(wom.shape)]
    args += [nw, wg, wu, wd]
    specs += [_const_spec(nw.shape), _layer_spec(wg.shape, layer), _layer_spec(wu.shape, layer),
              _layer_spec(wd.shape, layer)]
    if final_w is not None:
        args.append(final_w)
        specs.append(_const_spec(final_w.shape))
    out_shape = {"bs": (batch * seq, D_MODEL),
                 "bt": (batch // SUBLANES, SUBLANES, seq, D_MODEL),
                 "tb": (seq, batch // SUBLANES, SUBLANES, D_MODEL)}[out_layout]
    scratch = ([pltpu.VMEM((D_MODEL // LANES, tm + SUBLANES, LANES), F32)]
               if "tb" in (in_layout, out_layout) else [])
    return pl.pallas_call(
        functools.partial(_ffn_kernel, has_pre=pre is not None, has_final=final_w is not None,
                          in_layout=in_layout, out_layout=out_layout),
        grid=grid,
        in_specs=specs,
        out_specs=_row_spec(out_layout, tm, n_inner),
        out_shape=jax.ShapeDtypeStruct(out_shape, F32),
        scratch_shapes=scratch,
        compiler_params=_cparams(("parallel", "parallel")),
        name="ffn",
    )(*args)


def _inproj_kernel(x_ref, nw_ref, ws_ref, wm_ref, wdt_ref, wdtT_ref,
                   z_ref, xbc_ref, mx_ref, mz_ref, dt_ref, dtT_ref):
    u = _rms(x_ref[...], nw_ref[...]).astype(BF16)
    ps = _dot(u, ws_ref[...])
    z_ref[...] = ps[:, :SSD_D_INNER]
    xbc_ref[...] = ps[:, SSD_D_INNER:]
    pm = _dot(u, wm_ref[...])
    mx_ref[...] = pm[:, :ML_D_INNER]
    mz_ref[...] = pm[:, ML_D_INNER:]
    dt_ref[...] = _dot(u, wdt_ref[...])
    dtT_ref[...] = _dot_nt(wdtT_ref[...], u)


def _inproj(x, nw, w_ssd, w_ml, w_dt, w_dtT, *, tm=512):
    rows = x.shape[0]
    row = lambda w: pl.BlockSpec((tm, w), lambda i: (i, 0))
    consts = [nw, w_ssd, w_ml, w_dt, w_dtT]
    widths = [SSD_D_INNER, SSD_CONV_CH, ML_D_INNER, ML_D_INNER, LANES]
    return pl.pallas_call(
        _inproj_kernel,
        grid=(rows // tm,),
        in_specs=[row(D_MODEL)] + [_const_spec(a.shape) for a in consts],
        out_specs=[row(w) for w in widths] + [pl.BlockSpec((SSD_HEADS, tm), lambda i: (0, i))],
        out_shape=[jax.ShapeDtypeStruct((rows, w), F32) for w in widths]
                  + [jax.ShapeDtypeStruct((SSD_HEADS, rows), F32)],
        compiler_params=_cparams(("parallel",)),
        name="inproj",
    )(x, *consts)


def _causal_conv_silu(cbuf, x_ref, w_ref, b_ref):
    L = x_ref.shape[0]
    cbuf[CONV_HALO:CONV_HALO + L, :] = x_ref[...]
    ext = cbuf[...]
    prev = pltpu.roll(ext, 1, axis=0)
    near = ext * w_ref[3:4, :] + prev * w_ref[2:3, :]
    far = ext * w_ref[1:2, :] + prev * w_ref[0:1, :]
    acc = b_ref[...] + near[CONV_HALO:, :] + pltpu.roll(far, 2, axis=0)[CONV_HALO:, :]
    cbuf[0:CONV_HALO, :] = ext[L:L + CONV_HALO, :]
    return _silu(acc)


def _ssd_prepare(xbc_ref, dt_ref, dtT_ref, cw_ref, cb_ref, dtb_ref, dtbT_ref, alog_ref, alogT_ref, cbuf):
    L = CHUNK
    xbc = _causal_conv_silu(cbuf, xbc_ref, cw_ref, cb_ref)
    dt = _softplus(dt_ref[...] + dtb_ref[...])
    adt = dt * (-jnp.exp(alog_ref[...]))
    dtT = _softplus(dtT_ref[...] + dtbT_ref[...])
    adtT = dtT * (-jnp.exp(alogT_ref[...]))
    a_cs = _dot_f32(_tri(L, True).astype(F32), adt)
    a_csT = _dot_f32(adtT, _tri(L, False).astype(F32))
    w_st = dt * jnp.exp(a_cs[L - 1:L, :] - a_cs)
    return dict(xs=xbc[:, :SSD_D_INNER], bm=xbc[:, SSD_D_INNER:SSD_D_INNER + SSD_BC],
                cm=xbc[:, SSD_D_INNER + SSD_BC:], dtT=dtT, a_cs=a_cs, a_csT=a_csT, w_st=w_st)


def _ssd_group(g, pre, z_ref, dexp_ref, nw_ref, o_ref, st_ref):
    L = CHUNK
    xs, bm, cm, dtT, a_cs, a_csT, w_st = (pre[k] for k in ("xs", "bm", "cm", "dtT", "a_cs", "a_csT", "w_st"))
    tril = _tri(L, True)
    left = lax.broadcasted_iota(jnp.int32, (L, LANES), 1) < SSD_HEAD_DIM

    def col(a, h):
        return jnp.broadcast_to(a[:, h:h + 1], (L, LANES))

    heads_per_group = SSD_HEADS // SSD_GROUPS
    gs = slice(g * SSD_GROUP_W, (g + 1) * SSD_GROUP_W)
    bm_g = bm[:, g * SSD_D_STATE:(g + 1) * SSD_D_STATE]
    cm_g = cm[:, g * SSD_D_STATE:(g + 1) * SSD_D_STATE].astype(BF16)
    cb = _dot_nt(cm_g, bm_g.astype(BF16))
    st_prev = st_ref[:, gs]
    y_off = _dot(cm_g, st_prev.astype(BF16))
    diag, acs_pairs, xdec_pairs = [], [], []
    for hp in range(heads_per_group // 2):
        h0 = g * heads_per_group + 2 * hp
        cols = [col(a_cs, h0), col(a_cs, h0 + 1)]
        ms = []
        for h, acs_col in zip((h0, h0 + 1), cols):
            lm = jnp.exp(jnp.where(tril, acs_col - a_csT[h:h + 1, :], -jnp.inf))
            ms.append((cb * lm * dtT[h:h + 1, :]).astype(BF16))
        slab = xs[:, h0 * SSD_HEAD_DIM:(h0 + 2) * SSD_HEAD_DIM]
        rhs = jnp.concatenate([jnp.where(left, slab, 0.0), jnp.where(left, 0.0, slab)], axis=0)
        diag.append(_dot(jnp.concatenate(ms, axis=1), rhs.astype(BF16)))
        acs_pairs.append(jnp.where(left, cols[0], cols[1]))
        xdec_pairs.append(slab * jnp.where(left, col(w_st, h0), col(w_st, h0 + 1)))
    acs_e = jnp.concatenate(acs_pairs, axis=1)
    y = jnp.concatenate(diag, axis=1) + y_off * jnp.exp(acs_e) + dexp_ref[:, gs] * xs[:, gs]
    xd_dec = jnp.concatenate(xdec_pairs, axis=1).astype(BF16)
    st_ref[:, gs] = st_prev * jnp.exp(acs_e[L - 1:L, :]) + _dot(bm_g.T.astype(BF16), xd_dec)
    v = y * _silu(z_ref[:, gs])
    out = v * lax.rsqrt(jnp.mean(v * v, axis=-1, keepdims=True) + EPS) * nw_ref[:, gs]
    o_ref[:, gs] = out.astype(o_ref.dtype)


def _mlstm_project(mx_ref, cbuf, cw_ref, cb_ref, wq_ref, wk_ref, wv_ref, wif_ref, bif_ref):
    L = CHUNK
    mx = mx_ref[...]
    xc = _causal_conv_silu(cbuf, mx_ref, cw_ref, cb_ref)
    xc_b = xc.astype(BF16)
    mx_b = mx.astype(BF16)
    tile = 2 * LANES
    q = jnp.concatenate([_dot(xc_b[:, t * tile:(t + 1) * tile], wq_ref[t]) for t in range(ML_HEADS)], axis=1)
    k = jnp.concatenate([_dot(xc_b[:, t * tile:(t + 1) * tile], wk_ref[t]) for t in range(ML_HEADS)], axis=1)
    v = jnp.concatenate([_dot(mx_b[:, t * tile:(t + 1) * tile], wv_ref[t]) for t in range(ML_HEADS)], axis=1)
    qkv = jnp.concatenate([q, k, v], axis=1).astype(BF16)
    gates = _dot(qkv, wif_ref[...]) + bif_ref[...]
    gatesT = gates.T[0:2 * ML_HEADS, :]
    logf = -_softplus(-gates)
    logfT = -_softplus(-gatesT)
    bcum = _dot_f32(_tri(L, True).astype(F32), logf)
    bcumT = _dot_f32(logfT, _tri(L, False).astype(F32))
    return dict(xc=xc, qkv=qkv, q=q, k=k, gates=gates, gatesT=gatesT, bcum=bcum, bcumT=bcumT)


def _mlstm_head(h, pre, ct_ref, n_ref, m_ref):
    L = CHUNK
    tril = _tri(L, True)
    qkv, gates, gatesT, bcum, bcumT = pre["qkv"], pre["gates"], pre["gatesT"], pre["bcum"], pre["bcumT"]
    hs = slice(h * ML_HEAD_DIM, (h + 1) * ML_HEAD_DIM)
    qh = qkv[:, hs]
    kh = qkv[:, ML_D_INNER + hs.start:ML_D_INNER + hs.stop]
    vh = qkv[:, 2 * ML_D_INNER + hs.start:2 * ML_D_INNER + hs.stop]
    bc = jnp.broadcast_to(bcum[:, ML_HEADS + h:ML_HEADS + h + 1], (L, LANES))
    ig = jnp.broadcast_to(gates[:, h:h + 1], (L, LANES))
    m_prev = m_ref[0:1, :]
    n_prev = n_ref[0:1, :]
    ct_prev = ct_ref[...]

    dlog = jnp.where(tril, bc - bcumT[ML_HEADS + h:ML_HEADS + h + 1, :] + gatesT[h:h + 1, :], -jnp.inf)
    m_inter = bc + m_prev
    m_t = jnp.maximum(jnp.max(dlog, axis=-1, keepdims=True), m_inter)
    scores = _dot_nt(qh, kh) * jnp.exp(dlog - m_t)
    inter_w = jnp.exp(m_inter - m_t)
    num = _dot(scores.astype(BF16), vh) + _rep(inter_w, 2) * _dot(qh, ct_prev.astype(BF16))
    den = (jnp.sum(scores, axis=-1, keepdims=True)
           + inter_w * jnp.sum(pre["q"][:, hs] * n_prev, axis=-1, keepdims=True))
    hout = num / _rep(jnp.maximum(jnp.abs(den), jnp.exp(-m_t)), 2)

    b_last = bc[L - 1:L, :]
    w_state = b_last - bc + ig
    m_new = jnp.maximum(b_last + m_prev, jnp.max(w_state, axis=0, keepdims=True))
    decay = _rep(jnp.exp(b_last + m_prev - m_new), 2)
    kw = pre["k"][:, hs] * _rep(jnp.exp(w_state - m_new), 2)
    ct_ref[...] = decay * ct_prev + _dot(kw.T.astype(BF16), vh)
    n_ref[...] = jnp.broadcast_to(decay * n_prev + jnp.sum(kw, axis=0, keepdims=True), n_ref.shape)
    m_ref[...] = jnp.broadcast_to(m_new, m_ref.shape)

    mu = jnp.mean(hout, axis=-1, keepdims=True)
    d = hout - mu
    var = jnp.mean(d * d, axis=-1, keepdims=True)
    return d * lax.rsqrt(var + EPS)


N_SSD_CONSTS = 8
N_ML_CONSTS = 9


def _mixer_kernel(*refs):
    n = SEQS_PER_STEP
    it = iter(refs)
    take = lambda k: [next(it) for _ in range(k)]
    xbc_ref, dt_ref = take(2)
    dtT_refs = take(n)
    (z_ref,) = take(1)
    ssd_consts = take(N_SSD_CONSTS)
    mx_ref, mz_ref = take(2)
    cw_ref, cb_ref, wq_ref, wk_ref, wv_ref, wif_ref, bif_ref, nw_ref, skip_ref = take(N_ML_CONSTS)
    ys_ref, ym_ref, cbuf_s, st_ref, cbuf_m, ct_ref, n_ref, m_ref = take(8)

    @pl.when(pl.program_id(1) == 0)
    def _():
        st_ref[...] = jnp.zeros(st_ref.shape, F32)
        cbuf_s[:, 0:CONV_HALO, :] = jnp.zeros((n, CONV_HALO, cbuf_s.shape[2]), F32)
        ct_ref[...] = jnp.zeros(ct_ref.shape, F32)
        n_ref[...] = jnp.zeros(n_ref.shape, F32)
        m_ref[...] = jnp.full(m_ref.shape, -1e30, F32)
        cbuf_m[:, 0:CONV_HALO, :] = jnp.zeros((n, CONV_HALO, cbuf_m.shape[2]), F32)

    scw_ref, scb_ref, dtb_ref, dtbT_ref, alog_ref, alogT_ref, dexp_ref, snw_ref = ssd_consts
    pre, ssd_pre = [], []
    for s in range(n):
        pre.append(_mlstm_project(mx_ref.at[s], cbuf_m.at[s], cw_ref, cb_ref, wq_ref, wk_ref, wv_ref,
                                  wif_ref, bif_ref))
        ssd_pre.append(_ssd_prepare(xbc_ref.at[s], dt_ref.at[s], dtT_refs[s], scw_ref, scb_ref,
                                    dtb_ref, dtbT_ref, alog_ref, alogT_ref, cbuf_s.at[s]))
    heads = [[] for _ in range(n)]
    for h in range(ML_HEADS):
        for s in range(n):
            if h < SSD_GROUPS:
                _ssd_group(h, ssd_pre[s], z_ref.at[s], dexp_ref, snw_ref, ys_ref.at[s], st_ref.at[s])
            heads[s].append(_mlstm_head(h, pre[s], ct_ref.at[s, h], n_ref.at[s, h], m_ref.at[s, h]))
    for s in range(n):
        hm = jnp.concatenate(heads[s], axis=1) * nw_ref[...]
        ym_ref[s] = ((hm + skip_ref[...] * pre[s]["xc"]) * _silu(mz_ref[s])).astype(ym_ref.dtype)


def _mixer(xbc, dt, dtT, z, ssd_consts, mx, mz, ml_consts, *, batch, seq):
    L = CHUNK
    nc = seq // L
    n = SEQS_PER_STEP
    seqs = lambda t: t.reshape(batch // n, n, seq, t.shape[-1])
    row = lambda w: pl.BlockSpec((None, n, L, w), lambda b, c: (b, 0, c, 0))
    dtT_spec = lambda s: pl.BlockSpec((SSD_HEADS, L), lambda b, c: (0, (b * n + s) * nc + c))
    assert len(ssd_consts) == N_SSD_CONSTS and len(ml_consts) == N_ML_CONSTS
    y_shape = jax.ShapeDtypeStruct((batch // n, n, seq, D_MODEL), BF16)
    ys, ym = pl.pallas_call(
        _mixer_kernel,
        grid=(batch // n, nc),
        in_specs=[row(SSD_CONV_CH), row(LANES)] + [dtT_spec(s) for s in range(n)] + [row(SSD_D_INNER)]
                 + [_const_spec(a.shape) for a in ssd_consts]
                 + [row(ML_D_INNER)] * 2 + [_const_spec(a.shape) for a in ml_consts],
        out_specs=[row(SSD_D_INNER), row(ML_D_INNER)],
        out_shape=[y_shape, y_shape],
        scratch_shapes=[pltpu.VMEM((n, CONV_HALO + L, SSD_CONV_CH), F32),
                        pltpu.VMEM((n, SSD_D_STATE, SSD_D_INNER), F32),
                        pltpu.VMEM((n, CONV_HALO + L, ML_D_INNER), F32),
                        pltpu.VMEM((n, ML_HEADS, ML_HEAD_DIM, ML_HEAD_DIM), F32),
                        pltpu.VMEM((n, ML_HEADS, SUBLANES, ML_HEAD_DIM), F32),
                        pltpu.VMEM((n, ML_HEADS, SUBLANES, LANES), F32)],
        compiler_params=_cparams(("parallel", "arbitrary")),
        name="mixer",
    )(seqs(xbc), seqs(dt), *([dtT] * n), seqs(z), *ssd_consts, seqs(mx), seqs(mz), *ml_consts)
    return ys.reshape(batch * seq, D_MODEL), ym.reshape(batch * seq, D_MODEL)


def _s5_param_kernel(are_ref, aim_ref, lstep_ref, bre_ref, bim_ref, cre_ref, cim_ref, bs_ref, cs_ref, lam_ref):
    a_re, a_im = are_ref[...], aim_ref[...]
    step = jnp.exp(lstep_ref[...])
    mag = jnp.exp(a_re * step)
    lam_re = mag * jnp.cos(a_im * step)
    lam_im = mag * jnp.sin(a_im * step)
    den = a_re * a_re + a_im * a_im
    coef_re = ((lam_re - 1.0) * a_re + lam_im * a_im) / den
    coef_im = (lam_im * a_re - (lam_re - 1.0) * a_im) / den
    b_re, b_im = bre_ref[...], bim_ref[...]
    bb_re = coef_re * b_re - coef_im * b_im
    bb_im = coef_re * b_im + coef_im * b_re

    rows, wide = S5_SLAB_IN, S5_SLAB_ST
    r = lax.broadcasted_iota(jnp.int32, (rows, wide), 0)
    c = lax.broadcasted_iota(jnp.int32, (rows, wide), 1)
    same_group = r // S5_GROUP == c // S5_STATE
    sel = (lax.broadcasted_iota(jnp.int32, (S5_STATE, wide), 1) % S5_STATE
           == lax.broadcasted_iota(jnp.int32, (S5_STATE, wide), 0))

    def spread(t, exact):
        if exact:
            wide_t = _dot_f32(t, sel.astype(F32))
        else:
            wide_t = _dot(t.astype(BF16), sel.astype(BF16))
        return jnp.where(same_group, wide_t, 0.0)

    bs_ref[...] = jnp.concatenate([spread(bb_re, False), spread(bb_im, False)], axis=1).astype(BF16)
    cs_ref[...] = jnp.concatenate([spread(cre_ref[...], False).T, spread(-cim_ref[...], False).T],
                                  axis=0).astype(BF16)
    first = r % S5_GROUP == 0
    lam = jnp.concatenate([jnp.sum(jnp.where(first, spread(lam_re, True), 0.0), axis=0, keepdims=True),
                           jnp.sum(jnp.where(first, spread(lam_im, True), 0.0), axis=0, keepdims=True)], axis=1)
    lam_ref[...] = jnp.broadcast_to(lam, lam_ref.shape)


def _s5_params(a_re, a_im, log_step, b_re, b_im, c_re, c_im):
    rep = lambda t: jnp.repeat(t, S5_GROUP, axis=0)
    are, aim = rep(a_re), rep(a_im)
    lstep = jnp.broadcast_to(rep(log_step[:, None]), are.shape)
    flat = lambda t: t.reshape(D_MODEL, S5_STATE)
    args = [are, aim, lstep, flat(jnp.swapaxes(b_re, 1, 2)), flat(jnp.swapaxes(b_im, 1, 2)),
            flat(c_re), flat(c_im)]
    bs, cs, lam = pl.pallas_call(
        _s5_param_kernel,
        grid=(S5_SLABS,),
        in_specs=[pl.BlockSpec((S5_SLAB_IN, S5_STATE), lambda k: (k, 0))] * len(args),
        out_specs=[pl.BlockSpec((None, S5_SLAB_IN, 2 * S5_SLAB_ST), lambda k: (k, 0, 0)),
                   pl.BlockSpec((None, 2 * S5_SLAB_ST, S5_SLAB_IN), lambda k: (k, 0, 0)),
                   pl.BlockSpec((None, SUBLANES, 2 * S5_SLAB_ST), lambda k: (k, 0, 0))],
        out_shape=[jax.ShapeDtypeStruct((S5_SLABS, S5_SLAB_IN, 2 * S5_SLAB_ST), BF16),
                   jax.ShapeDtypeStruct((S5_SLABS, 2 * S5_SLAB_ST, S5_SLAB_IN), BF16),
                   jax.ShapeDtypeStruct((S5_SLABS, SUBLANES, 2 * S5_SLAB_ST), F32)],
        compiler_params=_cparams(("parallel",)),
        name="s5_params",
    )(*args)
    return bs, lam, cs


def _s5_kernel(x_ref, nw_ref, bs_ref, lam_ref, cs_ref, dsk_ref, wa_ref, ba_ref, wb_ref, bb_ref, o_ref,
               bu_ref, xs_ref, st_ref, *, batch):
    rows = x_ref.shape[0]
    steps = rows // batch

    @pl.when(pl.program_id(0) == 0)
    def _():
        st_ref[...] = jnp.zeros(st_ref.shape, F32)

    x = x_ref[...]
    u = _rms(x, nw_ref[...])
    u_b = u.astype(BF16)
    def bu_slab(k):
        bu_ref[k % 2] = _dot(u_b[:, k * S5_SLAB_IN:(k + 1) * S5_SLAB_IN], bs_ref[k])

    ys = []
    bu_slab(0)
    for k in range(S5_SLABS):
        if k + 1 < S5_SLABS:
            bu_slab(k + 1)
        for w in range(S5_SLAB_ST // S5_SCAN_W):
            re_l = slice(w * S5_SCAN_W, (w + 1) * S5_SCAN_W)
            im_l = slice(S5_SLAB_ST + w * S5_SCAN_W, S5_SLAB_ST + (w + 1) * S5_SCAN_W)
            lr = jnp.broadcast_to(lam_ref[k, 0:1, re_l], (batch, S5_SCAN_W))
            li = jnp.broadcast_to(lam_ref[k, 0:1, im_l], (batch, S5_SCAN_W))
            xr, xi = st_ref[k, :, re_l], st_ref[k, :, im_l]
            for t in range(steps):
                rows_t = slice(t * batch, (t + 1) * batch)
                xr, xi = (lr * xr - li * xi + bu_ref[k % 2, rows_t, re_l],
                          lr * xi + li * xr + bu_ref[k % 2, rows_t, im_l])
                xs_ref[k % 2, rows_t, re_l] = xr.astype(BF16)
                xs_ref[k % 2, rows_t, im_l] = xi.astype(BF16)
            st_ref[k, :, re_l] = xr
            st_ref[k, :, im_l] = xi
        ys.append(_dot(xs_ref[k % 2], cs_ref[k]))
    y = jnp.concatenate(ys, axis=1) + dsk_ref[...] * u
    g = jax.nn.gelu(y).astype(BF16)
    out = (_dot(g, wa_ref[...]) + ba_ref[...]) * _sigmoid(_dot(g, wb_ref[...]) + bb_ref[...])
    o_ref[...] = x + out


def _s5(x_sb, nw, b_slabs, lam_slabs, c_slabs, dsk, wa, ba, wb, bb, *, batch, seq, ts=32):
    rows = ts * batch
    consts = [nw, b_slabs, lam_slabs, c_slabs, dsk, wa, ba, wb, bb]
    return pl.pallas_call(
        functools.partial(_s5_kernel, batch=batch),
        grid=(seq // ts,),
        in_specs=[pl.BlockSpec((rows, D_MODEL), lambda i: (i, 0))] + [_const_spec(a.shape) for a in consts],
        out_specs=pl.BlockSpec((rows, D_MODEL), lambda i: (i, 0)),
        out_shape=jax.ShapeDtypeStruct((seq * batch, D_MODEL), F32),
        scratch_shapes=[pltpu.VMEM((2, rows, 2 * S5_SLAB_ST), F32),
                        pltpu.VMEM((2, rows, 2 * S5_SLAB_ST), BF16),
                        pltpu.VMEM((S5_SLABS, batch, 2 * S5_SLAB_ST), F32)],
        compiler_params=_cparams(("arbitrary",)),
        name="s5",
    )(x_sb, *consts)


def _blockdiag_tiles(w):
    nb = w.shape[0]
    tile = nb * ML_QKV_BLOCK // ML_HEADS
    rows = jnp.swapaxes(w, 1, 2).reshape(nb * ML_QKV_BLOCK, ML_QKV_BLOCK)
    sel = (jnp.arange(tile)[None, :] % ML_QKV_BLOCK == jnp.arange(ML_QKV_BLOCK)[:, None]).astype(w.dtype)
    wide = jnp.dot(rows, sel, precision=HIGHEST)
    blk_r = (jnp.arange(nb * ML_QKV_BLOCK) % tile) // ML_QKV_BLOCK
    blk_c = jnp.arange(tile) // ML_QKV_BLOCK
    wide = jnp.where(blk_r[:, None] == blk_c[None, :], wide, 0.0)
    return wide.reshape(ML_HEADS, tile, tile)


def _pad_lanes(t, n=LANES):
    return jnp.pad(t, ((0, 0), (0, n - t.shape[1])))


def kernel(x, ffn1_norm, ffn1_w_gate, ffn1_w_up, ffn1_w_down, mix_norm, ffn2_norm, ffn2_w_gate, ffn2_w_up, ffn2_w_down, hy_w_in, ssd_conv_w, ssd_conv_b, ssd_dt_bias, ssd_a_log, ssd_d, ssd_norm_w, ml_conv_w, ml_conv_b, ml_w_q, ml_w_k, ml_w_v, ml_w_if, ml_b_if, ml_norm_w, ml_skip, hy_w_out, s5_a_re, s5_a_im, s5_log_step, s5_b_re, s5_b_im, s5_c_re, s5_c_im, s5_d, s5_w_a, s5_b_a, s5_w_b, s5_b_b, final_norm):
    batch, seq, _ = x.shape
    assert seq % 512 == 0 and batch % 8 == 0
    row = lambda t: t.reshape(1, -1)
    bf = lambda t: t.astype(BF16)
    xf = x.reshape(batch * seq, D_MODEL)

    g1, u1, d1, g2, u2, d2 = _to_bf16([ffn1_w_gate, ffn1_w_up, ffn1_w_down, ffn2_w_gate, ffn2_w_up, ffn2_w_down])
    x1 = _ffn(xf, row(ffn1_norm[0]), g1, u1, d1, 0, batch=batch, seq=seq, tm=FFN_BIG_ROWS)

    w_in = hy_w_in[0]
    o1 = SSD_D_INNER
    o2 = o1 + SSD_CONV_CH
    o3 = o2 + SSD_HEADS
    w_dt = w_in[:, o2:o3]
    z_s, xbc, m_x, m_z, dt_raw, dt_rawT = _inproj(x1, row(mix_norm[0]), bf(w_in[:, :o2]), bf(w_in[:, o3:]),
                                                  bf(_pad_lanes(w_dt)), bf(w_dt.T))

    ssd_consts = [ssd_conv_w[0], row(ssd_conv_b[0]),
                  _pad_lanes(row(ssd_dt_bias[0])), ssd_dt_bias[0].reshape(-1, 1),
                  _pad_lanes(row(ssd_a_log[0])), ssd_a_log[0].reshape(-1, 1),
                  row(jnp.repeat(ssd_d[0], SSD_HEAD_DIM)), row(ssd_norm_w[0])]

    k_scale = 1.0 / math.sqrt(ML_HEAD_DIM)
    w_if = ml_w_if[0]
    w_if = jnp.concatenate([w_if[:ML_D_INNER], w_if[ML_D_INNER:2 * ML_D_INNER] / k_scale, w_if[2 * ML_D_INNER:]],
                           axis=0)
    b_if = ml_b_if[0]
    ml_consts = [ml_conv_w[0], row(ml_conv_b[0]),
                 bf(_blockdiag_tiles(ml_w_q[0])), bf(_blockdiag_tiles(ml_w_k[0]) * k_scale),
                 bf(_blockdiag_tiles(ml_w_v[0])), bf(_pad_lanes(w_if)),
                 _pad_lanes(row(b_if)), row(ml_norm_w[0]), row(ml_skip[0])]
    y_ssd, y_ml = _mixer(xbc, dt_raw, dt_rawT, z_s, ssd_consts, m_x, m_z, ml_consts, batch=batch, seq=seq)

    w_out = bf(hy_w_out[0])
    x3 = _ffn(x1, row(ffn2_norm[0]), g2, u2, d2, 0, batch=batch, seq=seq,
              pre=(y_ssd, y_ml, w_out[:SSD_D_INNER], w_out[SSD_D_INNER:]))

    x4 = _ffn(x3.reshape(batch // SUBLANES, SUBLANES, seq, D_MODEL), row(ffn1_norm[1]), g1, u1, d1, 1,
              batch=batch, seq=seq, in_layout="bt", out_layout="tb", tm=FFN_BIG_ROWS)
    b_slabs, lam_slabs, c_slabs = _s5_params(s5_a_re[0], s5_a_im[0], s5_log_step[0], s5_b_re[0], s5_b_im[0],
                                             s5_c_re[0], s5_c_im[0])
    x5 = _s5(x4.reshape(seq * batch, D_MODEL), row(mix_norm[1]), b_slabs, lam_slabs, c_slabs, row(s5_d[0]),
             bf(s5_w_a[0]), row(s5_b_a[0]), bf(s5_w_b[0]), row(s5_b_b[0]), batch=batch, seq=seq)
    out = _ffn(x5.reshape(seq, batch // SUBLANES, SUBLANES, D_MODEL), row(ffn2_norm[1]), g2, u2, d2, 1,
               batch=batch, seq=seq, in_layout="tb", out_layout="bt", final_w=row(final_norm), tm=FFN_BIG_ROWS)
    return out.reshape(batch, seq, D_MODEL)
```

```python
import functools
import math

import jax
import jax.numpy as jnp
from jax import lax
from jax.experimental import pallas as pl
from jax.experimental.pallas import tpu as pltpu

F32 = jnp.float32
BF16 = jnp.bfloat16
HIGHEST = lax.Precision.HIGHEST

D_MODEL = 1024
EPS = 1e-6
D_FF = 2816
FFN_RES = 0.5
CONV_W = 4
CONV_HALO = 8

SSD_HEADS = 16
SSD_HEAD_DIM = 64
SSD_GROUPS = 2
SSD_D_STATE = 128
SSD_D_INNER = SSD_HEADS * SSD_HEAD_DIM
SSD_BC = SSD_GROUPS * SSD_D_STATE
SSD_CONV_CH = SSD_D_INNER + 2 * SSD_BC
SSD_GROUP_W = SSD_D_INNER // SSD_GROUPS

ML_HEADS = 4
ML_HEAD_DIM = 256
ML_D_INNER = ML_HEADS * ML_HEAD_DIM
ML_QKV_BLOCK = 4

CHUNK = 128
LANES = 128
SUBLANES = 8
SEQS_PER_STEP = 4

S5_GROUP = 16
S5_GROUPS = D_MODEL // S5_GROUP
S5_STATE = 64
S5_SLAB_GROUPS = 16
S5_SLABS = S5_GROUPS // S5_SLAB_GROUPS
S5_SLAB_IN = S5_SLAB_GROUPS * S5_GROUP
S5_SLAB_ST = S5_SLAB_GROUPS * S5_STATE
S5_SCAN_W = 512

VMEM_LIMIT_BYTES = 56 * 1024 * 1024


def _cparams(sem):
    return pltpu.CompilerParams(dimension_semantics=sem, vmem_limit_bytes=VMEM_LIMIT_BYTES)


def _dot(a, b):
    return jnp.dot(a, b, preferred_element_type=F32)


def _dot_f32(a, b):
    return jnp.dot(a, b, preferred_element_type=F32, precision=HIGHEST)


def _dot_nt(a, b):
    return lax.dot_general(a, b, (((1,), (1,)), ((), ())), preferred_element_type=F32)


def _rms(x, w):
    return x * lax.rsqrt(jnp.mean(x * x, axis=-1, keepdims=True) + EPS) * w


def _sigmoid(x):
    return 1.0 / (1.0 + jnp.exp(-x))


def _silu(x):
    return x * _sigmoid(x)


def _softplus(x):
    return jnp.maximum(x, 0.0) + jnp.log1p(jnp.exp(-jnp.abs(x)))


def _rep(x, n):
    return jnp.concatenate([x] * n, axis=-1)


def _tri(n, lower):
    r = lax.broadcasted_iota(jnp.int32, (n, n), 0)
    c = lax.broadcasted_iota(jnp.int32, (n, n), 1)
    return r >= c if lower else r <= c


def _const_spec(shape):
    nd = len(shape)
    return pl.BlockSpec(shape, lambda *_: (0,) * nd, pipeline_mode=pl.Buffered(1))


def _layer_spec(shape, layer):
    return pl.BlockSpec((None,) + tuple(shape[1:]), lambda *_: (layer, 0, 0), pipeline_mode=pl.Buffered(1))


def _to_bf16_kernel(*refs):
    n = len(refs) // 2
    for src, dst in zip(refs[:n], refs[n:]):
        dst[...] = src[...].astype(BF16)


def _to_bf16(weights, blocks=8):
    specs = [pl.BlockSpec((1, w.shape[1] // blocks, w.shape[2]), lambda l, i: (l, i, 0)) for w in weights]
    return pl.pallas_call(
        _to_bf16_kernel,
        grid=(weights[0].shape[0], blocks),
        in_specs=specs,
        out_specs=specs,
        out_shape=[jax.ShapeDtypeStruct(w.shape, BF16) for w in weights],
        compiler_params=_cparams(("parallel", "parallel")),
        name="to_bf16",
    )(*weights)


FFN_CHUNK_ROWS = 512
FFN_BIG_ROWS = 1024


def _ffn_kernel(*refs, has_final, in_layout, out_layout):
    it = iter(refs)
    x_ref = next(it)
    nw_ref, wg_ref, wu_ref, wd_ref = next(it), next(it), next(it), next(it)
    if has_final:
        fw_ref = next(it)
    o_ref = next(it)
    scr = next(it, None)

    if in_layout == "tb":
        ts = x_ref.shape[0]
        pitch = ts + 1
        for t in range(ts):
            for j in range(scr.shape[0]):
                scr[j, pl.ds(t, SUBLANES, stride=pitch), :] = x_ref[t, :, j * LANES:(j + 1) * LANES]
        x = jnp.concatenate(
            [jnp.concatenate([scr[j, b * pitch:b * pitch + ts, :] for b in range(SUBLANES)], axis=0)
             for j in range(scr.shape[0])], axis=1)
    elif in_layout == "bt":
        x = x_ref[...].reshape(-1, D_MODEL)
    else:
        x = x_ref[...]
    sub = min(x.shape[0], FFN_CHUNK_ROWS)
    x_parts = [x[r:r + sub] for r in range(0, x.shape[0], sub)]
    h_parts = [_rms(xp, nw_ref[...]).astype(BF16) for xp in x_parts]
    outs = []
    for xp, h in zip(x_parts, h_parts):
        a = (_silu(_dot(h, wg_ref[...])) * _dot(h, wu_ref[...])).astype(BF16)
        xo = xp + FFN_RES * _dot(a, wd_ref[...])
        outs.append(_rms(xo, fw_ref[...]) if has_final else xo)
    x = jnp.concatenate(outs, axis=0) if len(outs) > 1 else outs[0]
    if out_layout == "tb":
        ts = o_ref.shape[0]
        pitch = ts + 1
        for j in range(scr.shape[0]):
            for b in range(SUBLANES):
                scr[j, b * pitch:b * pitch + ts, :] = x[b * ts:(b + 1) * ts, j * LANES:(j + 1) * LANES]
        for t in range(ts):
            for j in range(scr.shape[0]):
                o_ref[t, :, j * LANES:(j + 1) * LANES] = scr[j, pl.ds(t, SUBLANES, stride=pitch), :]
    elif out_layout == "bt":
        o_ref[...] = x.reshape(o_ref.shape)
    else:
        o_ref[...] = x


def _row_spec(layout, tm, n_inner):
    if layout == "bs":
        return pl.BlockSpec((tm, D_MODEL), lambda b, i: (b * n_inner + i, 0))
    ts = tm // SUBLANES
    if layout == "bt":
        return pl.BlockSpec((None, SUBLANES, ts, D_MODEL), lambda b, i: (b, 0, i, 0))
    return pl.BlockSpec((ts, None, SUBLANES, D_MODEL), lambda b, i: (i, b, 0, 0))


def _ffn(x, nw, wg, wu, wd, layer, *, batch, seq, in_layout="bs", out_layout="bs", tm=FFN_BIG_ROWS, final_w=None):
    if in_layout == "bs":
        grid = (batch, seq // tm)
        n_inner = seq // tm
    else:
        grid = (batch // SUBLANES, seq * SUBLANES // tm)
        n_inner = None
    args = [x]
    specs = [_row_spec(in_layout, tm, n_inner)]
    args += [nw, wg, wu, wd]
    specs += [_const_spec(nw.shape), _layer_spec(wg.shape, layer), _layer_spec(wu.shape, layer),
              _layer_spec(wd.shape, layer)]
    if final_w is not None:
        args.append(final_w)
        specs.append(_const_spec(final_w.shape))
    out_shape = {"bs": (batch * seq, D_MODEL),
                 "bt": (batch // SUBLANES, SUBLANES, seq, D_MODEL),
                 "tb": (seq, batch // SUBLANES, SUBLANES, D_MODEL)}[out_layout]
    scratch = ([pltpu.VMEM((D_MODEL // LANES, tm + SUBLANES, LANES), F32)]
               if "tb" in (in_layout, out_layout) else [])
    return pl.pallas_call(
        functools.partial(_ffn_kernel, has_final=final_w is not None,
                          in_layout=in_layout, out_layout=out_layout),
        grid=grid,
        in_specs=specs,
        out_specs=_row_spec(out_layout, tm, n_inner),
        out_shape=jax.ShapeDtypeStruct(out_shape, F32),
        scratch_shapes=scratch,
        compiler_params=_cparams(("parallel", "parallel")),
        name="ffn",
    )(*args)


def _inproj_kernel(x_ref, nw_ref, ws_ref, wm_ref, wdt_ref, wdtT_ref,
                   z_ref, xbc_ref, mx_ref, mz_ref, dt_ref, dtT_ref):
    u = _rms(x_ref[...], nw_ref[...]).astype(BF16)
    ps = _dot(u, ws_ref[...])
    z_ref[...] = ps[:, :SSD_D_INNER]
    xbc_ref[...] = ps[:, SSD_D_INNER:]
    pm = _dot(u, wm_ref[...])
    mx_ref[...] = pm[:, :ML_D_INNER]
    mz_ref[...] = pm[:, ML_D_INNER:]
    dt_ref[...] = _dot(u, wdt_ref[...])
    dtT_ref[...] = _dot_nt(wdtT_ref[...], u)


def _inproj(x, nw, w_ssd, w_ml, w_dt, w_dtT, *, tm=512):
    rows = x.shape[0]
    row = lambda w: pl.BlockSpec((tm, w), lambda i: (i, 0))
    consts = [nw, w_ssd, w_ml, w_dt, w_dtT]
    widths = [SSD_D_INNER, SSD_CONV_CH, ML_D_INNER, ML_D_INNER, LANES]
    return pl.pallas_call(
        _inproj_kernel,
        grid=(rows // tm,),
        in_specs=[row(D_MODEL)] + [_const_spec(a.shape) for a in consts],
        out_specs=[row(w) for w in widths] + [pl.BlockSpec((SSD_HEADS, tm), lambda i: (0, i))],
        out_shape=[jax.ShapeDtypeStruct((rows, w), F32) for w in widths]
                  + [jax.ShapeDtypeStruct((SSD_HEADS, rows), F32)],
        compiler_params=_cparams(("parallel",)),
        name="inproj",
    )(x, *consts)


def _causal_conv_silu(cbuf, x_ref, w_ref, b_ref):
    L = x_ref.shape[0]
    cbuf[CONV_HALO:CONV_HALO + L, :] = x_ref[...]
    ext = cbuf[...]
    prev = pltpu.roll(ext, 1, axis=0)
    near = ext * w_ref[3:4, :] + prev * w_ref[2:3, :]
    far = ext * w_ref[1:2, :] + prev * w_ref[0:1, :]
    acc = b_ref[...] + near[CONV_HALO:, :] + pltpu.roll(far, 2, axis=0)[CONV_HALO:, :]
    cbuf[0:CONV_HALO, :] = ext[L:L + CONV_HALO, :]
    return _silu(acc)


def _ssd_prepare(xbc_ref, dt_ref, dtT_ref, cw_ref, cb_ref, dtb_ref, dtbT_ref, alog_ref, alogT_ref, cbuf):
    L = CHUNK
    xbc = _causal_conv_silu(cbuf, xbc_ref, cw_ref, cb_ref)
    dt = _softplus(dt_ref[...] + dtb_ref[...])
    adt = dt * (-jnp.exp(alog_ref[...]))
    dtT = _softplus(dtT_ref[...] + dtbT_ref[...])
    adtT = dtT * (-jnp.exp(alogT_ref[...]))
    a_cs = _dot_f32(_tri(L, True).astype(F32), adt)
    a_csT = _dot_f32(adtT, _tri(L, False).astype(F32))
    w_st = dt * jnp.exp(a_cs[L - 1:L, :] - a_cs)
    return dict(xs=xbc[:, :SSD_D_INNER], bm=xbc[:, SSD_D_INNER:SSD_D_INNER + SSD_BC],
                cm=xbc[:, SSD_D_INNER + SSD_BC:], dtT=dtT, a_cs=a_cs, a_csT=a_csT, w_st=w_st)


def _ssd_group(g, pre, z_ref, dexp_ref, nw_ref, o_ref, st_ref):
    L = CHUNK
    xs, bm, cm, dtT, a_cs, a_csT, w_st = (pre[k] for k in ("xs", "bm", "cm", "dtT", "a_cs", "a_csT", "w_st"))
    tril = _tri(L, True)
    left = lax.broadcasted_iota(jnp.int32, (L, LANES), 1) < SSD_HEAD_DIM

    def col(a, h):
        return jnp.broadcast_to(a[:, h:h + 1], (L, LANES))

    heads_per_group = SSD_HEADS // SSD_GROUPS
    gs = slice(g * SSD_GROUP_W, (g + 1) * SSD_GROUP_W)
    bm_g = bm[:, g * SSD_D_STATE:(g + 1) * SSD_D_STATE]
    cm_g = cm[:, g * SSD_D_STATE:(g + 1) * SSD_D_STATE].astype(BF16)
    cb = _dot_nt(cm_g, bm_g.astype(BF16))
    st_prev = st_ref[:, gs]
    y_off = _dot(cm_g, st_prev.astype(BF16))
    diag, acs_pairs, xdec_pairs = [], [], []
    for hp in range(heads_per_group // 2):
        h0 = g * heads_per_group + 2 * hp
        cols = [col(a_cs, h0), col(a_cs, h0 + 1)]
        ms = []
        for h, acs_col in zip((h0, h0 + 1), cols):
            lm = jnp.exp(jnp.where(tril, acs_col - a_csT[h:h + 1, :], -jnp.inf))
            ms.append((cb * lm * dtT[h:h + 1, :]).astype(BF16))
        slab = xs[:, h0 * SSD_HEAD_DIM:(h0 + 2) * SSD_HEAD_DIM]
        rhs = jnp.concatenate([jnp.where(left, slab, 0.0), jnp.where(left, 0.0, slab)], axis=0)
        diag.append(_dot(jnp.concatenate(ms, axis=1), rhs.astype(BF16)))
        acs_pairs.append(jnp.where(left, cols[0], cols[1]))
        xdec_pairs.append(slab * jnp.where(left, col(w_st, h0), col(w_st, h0 + 1)))
    acs_e = jnp.concatenate(acs_pairs, axis=1)
    y = jnp.concatenate(diag, axis=1) + y_off * jnp.exp(acs_e) + dexp_ref[:, gs] * xs[:, gs]
    xd_dec = jnp.concatenate(xdec_pairs, axis=1).astype(BF16)
    st_ref[:, gs] = st_prev * jnp.exp(acs_e[L - 1:L, :]) + _dot(bm_g.T.astype(BF16), xd_dec)
    v = y * _silu(z_ref[:, gs])
    out = v * lax.rsqrt(jnp.mean(v * v, axis=-1, keepdims=True) + EPS) * nw_ref[:, gs]
    o_ref[:, gs] = out.astype(o_ref.dtype)


def _mlstm_project(mx_ref, cbuf, cw_ref, cb_ref, wq_ref, wk_ref, wv_ref, wif_ref, bif_ref):
    L = CHUNK
    mx = mx_ref[...]
    xc = _causal_conv_silu(cbuf, mx_ref, cw_ref, cb_ref)
    xc_b = xc.astype(BF16)
    mx_b = mx.astype(BF16)
    tile = 2 * LANES
    q = jnp.concatenate([_dot(xc_b[:, t * tile:(t + 1) * tile], wq_ref[t]) for t in range(ML_HEADS)], axis=1)
    k = jnp.concatenate([_dot(xc_b[:, t * tile:(t + 1) * tile], wk_ref[t]) for t in range(ML_HEADS)], axis=1)
    v = jnp.concatenate([_dot(mx_b[:, t * tile:(t + 1) * tile], wv_ref[t]) for t in range(ML_HEADS)], axis=1)
    qkv = jnp.concatenate([q, k, v], axis=1).astype(BF16)
    gates = _dot(qkv, wif_ref[...]) + bif_ref[...]
    gatesT = gates.T[0:2 * ML_HEADS, :]
    logf = -_softplus(-gates)
    logfT = -_softplus(-gatesT)
    bcum = _dot_f32(_tri(L, True).astype(F32), logf)
    bcumT = _dot_f32(logfT, _tri(L, False).astype(F32))
    return dict(xc=xc, qkv=qkv, q=q, k=k, gates=gates, gatesT=gatesT, bcum=bcum, bcumT=bcumT)


def _mlstm_head(h, pre, ct_ref, n_ref, m_ref):
    L = CHUNK
    tril = _tri(L, True)
    qkv, gates, gatesT, bcum, bcumT = pre["qkv"], pre["gates"], pre["gatesT"], pre["bcum"], pre["bcumT"]
    hs = slice(h * ML_HEAD_DIM, (h + 1) * ML_HEAD_DIM)
    qh = qkv[:, hs]
    kh = qkv[:, ML_D_INNER + hs.start:ML_D_INNER + hs.stop]
    vh = qkv[:, 2 * ML_D_INNER + hs.start:2 * ML_D_INNER + hs.stop]
    bc = jnp.broadcast_to(bcum[:, ML_HEADS + h:ML_HEADS + h + 1], (L, LANES))
    ig = jnp.broadcast_to(gates[:, h:h + 1], (L, LANES))
    m_prev = m_ref[0:1, :]
    n_prev = n_ref[0:1, :]
    ct_prev = ct_ref[...]

    dlog = jnp.where(tril, bc - bcumT[ML_HEADS + h:ML_HEADS + h + 1, :] + gatesT[h:h + 1, :], -jnp.inf)
    m_inter = bc + m_prev
    m_t = jnp.maximum(jnp.max(dlog, axis=-1, keepdims=True), m_inter)
    scores = _dot_nt(qh, kh) * jnp.exp(dlog - m_t)
    inter_w = jnp.exp(m_inter - m_t)
    num = _dot(scores.astype(BF16), vh) + _rep(inter_w, 2) * _dot(qh, ct_prev.astype(BF16))
    den = (jnp.sum(scores, axis=-1, keepdims=True)
           + inter_w * jnp.sum(pre["q"][:, hs] * n_prev, axis=-1, keepdims=True))
    hout = num / _rep(jnp.maximum(jnp.abs(den), jnp.exp(-m_t)), 2)

    b_last = bc[L - 1:L, :]
    w_state = b_last - bc + ig
    m_new = jnp.maximum(b_last + m_prev, jnp.max(w_state, axis=0, keepdims=True))
    decay = _rep(jnp.exp(b_last + m_prev - m_new), 2)
    kw = pre["k"][:, hs] * _rep(jnp.exp(w_state - m_new), 2)
    ct_ref[...] = decay * ct_prev + _dot(kw.T.astype(BF16), vh)
    n_ref[...] = jnp.broadcast_to(decay * n_prev + jnp.sum(kw, axis=0, keepdims=True), n_ref.shape)
    m_ref[...] = jnp.broadcast_to(m_new, m_ref.shape)

    mu = jnp.mean(hout, axis=-1, keepdims=True)
    d = hout - mu
    var = jnp.mean(d * d, axis=-1, keepdims=True)
    return d * lax.rsqrt(var + EPS)


N_SSD_CONSTS = 8
N_ML_CONSTS = 9


def _mixer_kernel(*refs):
    n = SEQS_PER_STEP
    it = iter(refs)
    take = lambda k: [next(it) for _ in range(k)]
    xbc_ref, dt_ref = take(2)
    dtT_refs = take(n)
    (z_ref,) = take(1)
    ssd_consts = take(N_SSD_CONSTS)
    mx_ref, mz_ref = take(2)
    cw_ref, cb_ref, wq_ref, wk_ref, wv_ref, wif_ref, bif_ref, nw_ref, skip_ref = take(N_ML_CONSTS)
    x_ref, wos_ref, wom_ref = take(3)
    o_ref, ys_ref, cbuf_s, st_ref, cbuf_m, ct_ref, n_ref, m_ref = take(8)

    @pl.when(pl.program_id(1) == 0)
    def _():
        st_ref[...] = jnp.zeros(st_ref.shape, F32)
        cbuf_s[:, 0:CONV_HALO, :] = jnp.zeros((n, CONV_HALO, cbuf_s.shape[2]), F32)
        ct_ref[...] = jnp.zeros(ct_ref.shape, F32)
        n_ref[...] = jnp.zeros(n_ref.shape, F32)
        m_ref[...] = jnp.full(m_ref.shape, -1e30, F32)
        cbuf_m[:, 0:CONV_HALO, :] = jnp.zeros((n, CONV_HALO, cbuf_m.shape[2]), F32)

    scw_ref, scb_ref, dtb_ref, dtbT_ref, alog_ref, alogT_ref, dexp_ref, snw_ref = ssd_consts

    def prepare(seqs):
        return {s: (_mlstm_project(mx_ref.at[s], cbuf_m.at[s], cw_ref, cb_ref, wq_ref, wk_ref, wv_ref,
                                   wif_ref, bif_ref),
                    _ssd_prepare(xbc_ref.at[s], dt_ref.at[s], dtT_refs[s], scw_ref, scb_ref,
                                 dtb_ref, dtbT_ref, alog_ref, alogT_ref, cbuf_s.at[s])) for s in seqs}

    def mix(seqs, pre):
        heads = {s: [] for s in seqs}
        for h in range(ML_HEADS):
            for s in seqs:
                if h < SSD_GROUPS:
                    _ssd_group(h, pre[s][1], z_ref.at[s], dexp_ref, snw_ref, ys_ref.at[s], st_ref.at[s])
                heads[s].append(_mlstm_head(h, pre[s][0], ct_ref.at[s, h], n_ref.at[s, h], m_ref.at[s, h]))
        ym = {}
        for s in seqs:
            hm = jnp.concatenate(heads[s], axis=1) * nw_ref[...]
            ym[s] = ((hm + skip_ref[...] * pre[s][0]["xc"]) * _silu(mz_ref[s])).astype(BF16)
        return ym

    def project_out(seqs, ym):
        ys = jnp.concatenate([ys_ref[s] for s in seqs], axis=0)
        yml = jnp.concatenate([ym[s] for s in seqs], axis=0)
        out = _dot(ys, wos_ref[...]) + _dot(yml, wom_ref[...])
        for i, s in enumerate(seqs):
            o_ref[s] = x_ref[s] + out[i * CHUNK:(i + 1) * CHUNK]

    first, second = list(range(n // 2)), list(range(n // 2, n))
    pre_a = prepare(first)
    ym_a = mix(first, pre_a)
    pre_b = prepare(second)
    project_out(first, ym_a)
    ym_b = mix(second, pre_b)
    project_out(second, ym_b)


def _mixer(x, xbc, dt, dtT, z, ssd_consts, mx, mz, ml_consts, w_out_ssd, w_out_ml, *, batch, seq):
    L = CHUNK
    nc = seq // L
    n = SEQS_PER_STEP
    seqs = lambda t: t.reshape(batch // n, n, seq, t.shape[-1])
    row = lambda w: pl.BlockSpec((None, n, L, w), lambda b, c: (b, 0, c, 0))
    dtT_spec = lambda s: pl.BlockSpec((SSD_HEADS, L), lambda b, c: (0, (b * n + s) * nc + c))
    assert len(ssd_consts) == N_SSD_CONSTS and len(ml_consts) == N_ML_CONSTS
    out = pl.pallas_call(
        _mixer_kernel,
        grid=(batch // n, nc),
        in_specs=[row(SSD_CONV_CH), row(LANES)] + [dtT_spec(s) for s in range(n)] + [row(SSD_D_INNER)]
                 + [_const_spec(a.shape) for a in ssd_consts]
                 + [row(ML_D_INNER)] * 2 + [_const_spec(a.shape) for a in ml_consts]
                 + [row(D_MODEL), _const_spec(w_out_ssd.shape), _const_spec(w_out_ml.shape)],
        out_specs=row(D_MODEL),
        out_shape=jax.ShapeDtypeStruct((batch // n, n, seq, D_MODEL), F32),
        scratch_shapes=[pltpu.VMEM((n, L, SSD_D_INNER), BF16),
                        pltpu.VMEM((n, CONV_HALO + L, SSD_CONV_CH), F32),
                        pltpu.VMEM((n, SSD_D_STATE, SSD_D_INNER), F32),
                        pltpu.VMEM((n, CONV_HALO + L, ML_D_INNER), F32),
                        pltpu.VMEM((n, ML_HEADS, ML_HEAD_DIM, ML_HEAD_DIM), F32),
                        pltpu.VMEM((n, ML_HEADS, SUBLANES, ML_HEAD_DIM), F32),
                        pltpu.VMEM((n, ML_HEADS, SUBLANES, LANES), F32)],
        compiler_params=_cparams(("parallel", "arbitrary")),
        name="mixer",
    )(seqs(xbc), seqs(dt), *([dtT] * n), seqs(z), *ssd_consts, seqs(mx), seqs(mz), *ml_consts,
      seqs(x), w_out_ssd, w_out_ml)
    return out.reshape(batch * seq, D_MODEL)


def _s5_param_kernel(are_ref, aim_ref, lstep_ref, bre_ref, bim_ref, cre_ref, cim_ref, bs_ref, cs_ref, lam_ref):
    a_re, a_im = are_ref[...], aim_ref[...]
    step = jnp.exp(lstep_ref[...])
    mag = jnp.exp(a_re * step)
    lam_re = mag * jnp.cos(a_im * step)
    lam_im = mag * jnp.sin(a_im * step)
    den = a_re * a_re + a_im * a_im
    coef_re = ((lam_re - 1.0) * a_re + lam_im * a_im) / den
    coef_im = (lam_im * a_re - (lam_re - 1.0) * a_im) / den
    b_re, b_im = bre_ref[...], bim_ref[...]
    bb_re = coef_re * b_re - coef_im * b_im
    bb_im = coef_re * b_im + coef_im * b_re

    rows, wide = S5_SLAB_IN, S5_SLAB_ST
    r = lax.broadcasted_iota(jnp.int32, (rows, wide), 0)
    c = lax.broadcasted_iota(jnp.int32, (rows, wide), 1)
    same_group = r // S5_GROUP == c // S5_STATE
    sel = (lax.broadcasted_iota(jnp.int32, (S5_STATE, wide), 1) % S5_STATE
           == lax.broadcasted_iota(jnp.int32, (S5_STATE, wide), 0))

    def spread(t, exact):
        if exact:
            wide_t = _dot_f32(t, sel.astype(F32))
        else:
            wide_t = _dot(t.astype(BF16), sel.astype(BF16))
        return jnp.where(same_group, wide_t, 0.0)

    bs_ref[...] = jnp.concatenate([spread(bb_re, False), spread(bb_im, False)], axis=1).astype(BF16)
    cs_ref[...] = jnp.concatenate([spread(cre_ref[...], False).T, spread(-cim_ref[...], False).T],
                                  axis=0).astype(BF16)
    first = r % S5_GROUP == 0
    lam = jnp.concatenate([jnp.sum(jnp.where(first, spread(lam_re, True), 0.0), axis=0, keepdims=True),
                           jnp.sum(jnp.where(first, spread(lam_im, True), 0.0), axis=0, keepdims=True)], axis=1)
    lam_ref[...] = jnp.broadcast_to(lam, lam_ref.shape)


def _s5_params(a_re, a_im, log_step, b_re, b_im, c_re, c_im):
    rep = lambda t: jnp.repeat(t, S5_GROUP, axis=0)
    are, aim = rep(a_re), rep(a_im)
    lstep = jnp.broadcast_to(rep(log_step[:, None]), are.shape)
    flat = lambda t: t.reshape(D_MODEL, S5_STATE)
    args = [are, aim, lstep, flat(jnp.swapaxes(b_re, 1, 2)), flat(jnp.swapaxes(b_im, 1, 2)),
            flat(c_re), flat(c_im)]
    bs, cs, lam = pl.pallas_call(
        _s5_param_kernel,
        grid=(S5_SLABS,),
        in_specs=[pl.BlockSpec((S5_SLAB_IN, S5_STATE), lambda k: (k, 0))] * len(args),
        out_specs=[pl.BlockSpec((None, S5_SLAB_IN, 2 * S5_SLAB_ST), lambda k: (k, 0, 0)),
                   pl.BlockSpec((None, 2 * S5_SLAB_ST, S5_SLAB_IN), lambda k: (k, 0, 0)),
                   pl.BlockSpec((None, SUBLANES, 2 * S5_SLAB_ST), lambda k: (k, 0, 0))],
        out_shape=[jax.ShapeDtypeStruct((S5_SLABS, S5_SLAB_IN, 2 * S5_SLAB_ST), BF16),
                   jax.ShapeDtypeStruct((S5_SLABS, 2 * S5_SLAB_ST, S5_SLAB_IN), BF16),
                   jax.ShapeDtypeStruct((S5_SLABS, SUBLANES, 2 * S5_SLAB_ST), F32)],
        compiler_params=_cparams(("parallel",)),
        name="s5_params",
    )(*args)
    return bs, lam, cs


def _s5_kernel(x_ref, nw_ref, bs_ref, lam_ref, cs_ref, dsk_ref, wa_ref, ba_ref, wb_ref, bb_ref, o_ref,
               bu_ref, xs_ref, st_ref, *, batch):
    rows = x_ref.shape[0]
    steps = rows // batch

    @pl.when(pl.program_id(0) == 0)
    def _():
        st_ref[...] = jnp.zeros(st_ref.shape, F32)

    x = x_ref[...]
    u = _rms(x, nw_ref[...])
    u_b = u.astype(BF16)
    def bu_slab(k):
        bu_ref[k % 2] = _dot(u_b[:, k * S5_SLAB_IN:(k + 1) * S5_SLAB_IN], bs_ref[k])

    ys = []
    bu_slab(0)
    for k in range(S5_SLABS):
        if k + 1 < S5_SLABS:
            bu_slab(k + 1)
        for w in range(S5_SLAB_ST // S5_SCAN_W):
            re_l = slice(w * S5_SCAN_W, (w + 1) * S5_SCAN_W)
            im_l = slice(S5_SLAB_ST + w * S5_SCAN_W, S5_SLAB_ST + (w + 1) * S5_SCAN_W)
            lr = jnp.broadcast_to(lam_ref[k, 0:1, re_l], (batch, S5_SCAN_W))
            li = jnp.broadcast_to(lam_ref[k, 0:1, im_l], (batch, S5_SCAN_W))
            xr, xi = st_ref[k, :, re_l], st_ref[k, :, im_l]
            for t in range(steps):
                rows_t = slice(t * batch, (t + 1) * batch)
                xr, xi = (lr * xr - li * xi + bu_ref[k % 2, rows_t, re_l],
                          lr * xi + li * xr + bu_ref[k % 2, rows_t, im_l])
                xs_ref[k % 2, rows_t, re_l] = xr.astype(BF16)
                xs_ref[k % 2, rows_t, im_l] = xi.astype(BF16)
            st_ref[k, :, re_l] = xr
            st_ref[k, :, im_l] = xi
        ys.append(_dot(xs_ref[k % 2], cs_ref[k]))
    y = jnp.concatenate(ys, axis=1) + dsk_ref[...] * u
    g = jax.nn.gelu(y).astype(BF16)
    out = (_dot(g, wa_ref[...]) + ba_ref[...]) * _sigmoid(_dot(g, wb_ref[...]) + bb_ref[...])
    o_ref[...] = x + out


def _s5(x_sb, nw, b_slabs, lam_slabs, c_slabs, dsk, wa, ba, wb, bb, *, batch, seq, ts=32):
    rows = ts * batch
    consts = [nw, b_slabs, lam_slabs, c_slabs, dsk, wa, ba, wb, bb]
    return pl.pallas_call(
        functools.partial(_s5_kernel, batch=batch),
        grid=(seq // ts,),
        in_specs=[pl.BlockSpec((rows, D_MODEL), lambda i: (i, 0))] + [_const_spec(a.shape) for a in consts],
        out_specs=pl.BlockSpec((rows, D_MODEL), lambda i: (i, 0)),
        out_shape=jax.ShapeDtypeStruct((seq * batch, D_MODEL), F32),
        scratch_shapes=[pltpu.VMEM((2, rows, 2 * S5_SLAB_ST), F32),
                        pltpu.VMEM((2, rows, 2 * S5_SLAB_ST), BF16),
                        pltpu.VMEM((S5_SLABS, batch, 2 * S5_SLAB_ST), F32)],
        compiler_params=_cparams(("arbitrary",)),
        name="s5",
    )(x_sb, *consts)


def _blockdiag_tiles(w):
    nb = w.shape[0]
    tile = nb * ML_QKV_BLOCK // ML_HEADS
    rows = jnp.swapaxes(w, 1, 2).reshape(nb * ML_QKV_BLOCK, ML_QKV_BLOCK)
    sel = (jnp.arange(tile)[None, :] % ML_QKV_BLOCK == jnp.arange(ML_QKV_BLOCK)[:, None]).astype(w.dtype)
    wide = jnp.dot(rows, sel, precision=HIGHEST)
    blk_r = (jnp.arange(nb * ML_QKV_BLOCK) % tile) // ML_QKV_BLOCK
    blk_c = jnp.arange(tile) // ML_QKV_BLOCK
    wide = jnp.where(blk_r[:, None] == blk_c[None, :], wide, 0.0)
    return wide.reshape(ML_HEADS, tile, tile)


def _pad_lanes(t, n=LANES):
    return jnp.pad(t, ((0, 0), (0, n - t.shape[1])))


def kernel(x, ffn1_norm, ffn1_w_gate, ffn1_w_up, ffn1_w_down, mix_norm, ffn2_norm, ffn2_w_gate, ffn2_w_up, ffn2_w_down, hy_w_in, ssd_conv_w, ssd_conv_b, ssd_dt_bias, ssd_a_log, ssd_d, ssd_norm_w, ml_conv_w, ml_conv_b, ml_w_q, ml_w_k, ml_w_v, ml_w_if, ml_b_if, ml_norm_w, ml_skip, hy_w_out, s5_a_re, s5_a_im, s5_log_step, s5_b_re, s5_b_im, s5_c_re, s5_c_im, s5_d, s5_w_a, s5_b_a, s5_w_b, s5_b_b, final_norm):
    batch, seq, _ = x.shape
    assert seq % 512 == 0 and batch % 8 == 0
    row = lambda t: t.reshape(1, -1)
    bf = lambda t: t.astype(BF16)
    xf = x.reshape(batch * seq, D_MODEL)

    g1, u1, d1, g2, u2, d2 = _to_bf16([ffn1_w_gate, ffn1_w_up, ffn1_w_down, ffn2_w_gate, ffn2_w_up, ffn2_w_down])
    x1 = _ffn(xf, row(ffn1_norm[0]), g1, u1, d1, 0, batch=batch, seq=seq)

    w_in = hy_w_in[0]
    o1 = SSD_D_INNER
    o2 = o1 + SSD_CONV_CH
    o3 = o2 + SSD_HEADS
    w_dt = w_in[:, o2:o3]
    z_s, xbc, m_x, m_z, dt_raw, dt_rawT = _inproj(x1, row(mix_norm[0]), bf(w_in[:, :o2]), bf(w_in[:, o3:]),
                                                  bf(_pad_lanes(w_dt)), bf(w_dt.T))

    ssd_consts = [ssd_conv_w[0], row(ssd_conv_b[0]),
                  _pad_lanes(row(ssd_dt_bias[0])), ssd_dt_bias[0].reshape(-1, 1),
                  _pad_lanes(row(ssd_a_log[0])), ssd_a_log[0].reshape(-1, 1),
                  row(jnp.repeat(ssd_d[0], SSD_HEAD_DIM)), row(ssd_norm_w[0])]

    k_scale = 1.0 / math.sqrt(ML_HEAD_DIM)
    w_if = ml_w_if[0]
    w_if = jnp.concatenate([w_if[:ML_D_INNER], w_if[ML_D_INNER:2 * ML_D_INNER] / k_scale, w_if[2 * ML_D_INNER:]],
                           axis=0)
    b_if = ml_b_if[0]
    ml_consts = [ml_conv_w[0], row(ml_conv_b[0]),
                 bf(_blockdiag_tiles(ml_w_q[0])), bf(_blockdiag_tiles(ml_w_k[0]) * k_scale),
                 bf(_blockdiag_tiles(ml_w_v[0])), bf(_pad_lanes(w_if)),
                 _pad_lanes(row(b_if)), row(ml_norm_w[0]), row(ml_skip[0])]
    w_out = bf(hy_w_out[0])
    x2 = _mixer(x1, xbc, dt_raw, dt_rawT, z_s, ssd_consts, m_x, m_z, ml_consts,
                w_out[:SSD_D_INNER], w_out[SSD_D_INNER:], batch=batch, seq=seq)
    x3 = _ffn(x2, row(ffn2_norm[0]), g2, u2, d2, 0, batch=batch, seq=seq)

    x4 = _ffn(x3.reshape(batch // SUBLANES, SUBLANES, seq, D_MODEL), row(ffn1_norm[1]), g1, u1, d1, 1,
              batch=batch, seq=seq, in_layout="bt", out_layout="tb")
    b_slabs, lam_slabs, c_slabs = _s5_params(s5_a_re[0], s5_a_im[0], s5_log_step[0], s5_b_re[0], s5_b_im[0],
                                             s5_c_re[0], s5_c_im[0])
    x5 = _s5(x4.reshape(seq * batch, D_MODEL), row(mix_norm[1]), b_slabs, lam_slabs, c_slabs, row(s5_d[0]),
             bf(s5_w_a[0]), row(s5_b_a[0]), bf(s5_w_b[0]), row(s5_b_b[0]), batch=batch, seq=seq)
    out = _ffn(x5.reshape(seq, batch // SUBLANES, SUBLANES, D_MODEL), row(ffn2_norm[1]), g2, u2, d2, 1,
               batch=batch, seq=seq, in_layout="tb", out_layout="bt", final_w=row(final_norm))
    return out.reshape(batch, seq, D_MODEL)
```

```python
import functools
import math

import jax
import jax.numpy as jnp
from jax import lax
from jax.experimental import pallas as pl
from jax.experimental.pallas import tpu as pltpu

F32 = jnp.float32
BF16 = jnp.bfloat16
HIGHEST = lax.Precision.HIGHEST

D_MODEL = 1024
EPS = 1e-6
D_FF = 2816
FFN_RES = 0.5
CONV_W = 4
CONV_HALO = 8

SSD_HEADS = 16
SSD_HEAD_DIM = 64
SSD_GROUPS = 2
SSD_D_STATE = 128
SSD_D_INNER = SSD_HEADS * SSD_HEAD_DIM
SSD_BC = SSD_GROUPS * SSD_D_STATE
SSD_CONV_CH = SSD_D_INNER + 2 * SSD_BC
SSD_GROUP_W = SSD_D_INNER // SSD_GROUPS

ML_HEADS = 4
ML_HEAD_DIM = 256
ML_D_INNER = ML_HEADS * ML_HEAD_DIM
ML_QKV_BLOCK = 4

CHUNK = 128
LANES = 128
SUBLANES = 8
SEQS_PER_STEP = 4

S5_GROUP = 16
S5_GROUPS = D_MODEL // S5_GROUP
S5_STATE = 64
S5_SLAB_GROUPS = 16
S5_SLABS = S5_GROUPS // S5_SLAB_GROUPS
S5_SLAB_IN = S5_SLAB_GROUPS * S5_GROUP
S5_SLAB_ST = S5_SLAB_GROUPS * S5_STATE
S5_SCAN_W = 512

VMEM_LIMIT_BYTES = 56 * 1024 * 1024


def _cparams(sem):
    return pltpu.CompilerParams(dimension_semantics=sem, vmem_limit_bytes=VMEM_LIMIT_BYTES)


def _dot(a, b):
    return jnp.dot(a, b, preferred_element_type=F32)


def _dot_f32(a, b):
    return jnp.dot(a, b, preferred_element_type=F32, precision=HIGHEST)


def _dot_nt(a, b):
    return lax.dot_general(a, b, (((1,), (1,)), ((), ())), preferred_element_type=F32)


def _rms(x, w):
    return x * lax.rsqrt(jnp.mean(x * x, axis=-1, keepdims=True) + EPS) * w


def _sigmoid(x):
    return 1.0 / (1.0 + jnp.exp(-x))


def _silu(x):
    return x * _sigmoid(x)


def _softplus(x):
    return jnp.maximum(x, 0.0) + jnp.log1p(jnp.exp(-jnp.abs(x)))


def _rep(x, n):
    return jnp.concatenate([x] * n, axis=-1)


def _tri(n, lower):
    r = lax.broadcasted_iota(jnp.int32, (n, n), 0)
    c = lax.broadcasted_iota(jnp.int32, (n, n), 1)
    return r >= c if lower else r <= c


def _const_spec(shape):
    nd = len(shape)
    return pl.BlockSpec(shape, lambda *_: (0,) * nd, pipeline_mode=pl.Buffered(1))


def _layer_spec(shape, layer):
    return pl.BlockSpec((None,) + tuple(shape[1:]), lambda *_: (layer, 0, 0), pipeline_mode=pl.Buffered(1))


def _to_bf16_kernel(*refs):
    n = len(refs) // 2
    for src, dst in zip(refs[:n], refs[n:]):
        dst[...] = src[...].astype(BF16)


def _to_bf16(weights, blocks=8):
    specs = [pl.BlockSpec((1, w.shape[1] // blocks, w.shape[2]), lambda l, i: (l, i, 0)) for w in weights]
    return pl.pallas_call(
        _to_bf16_kernel,
        grid=(weights[0].shape[0], blocks),
        in_specs=specs,
        out_specs=specs,
        out_shape=[jax.ShapeDtypeStruct(w.shape, BF16) for w in weights],
        compiler_params=_cparams(("parallel", "parallel")),
        name="to_bf16",
    )(*weights)


FFN_CHUNK_ROWS = 512
FFN_BIG_ROWS = 1024


def _ffn_kernel(*refs, has_final, in_layout, out_layout):
    it = iter(refs)
    x_ref = next(it)
    nw_ref, wg_ref, wu_ref, wd_ref = next(it), next(it), next(it), next(it)
    if has_final:
        fw_ref = next(it)
    o_ref = next(it)
    scr = next(it, None)

    if in_layout == "tb":
        ts = x_ref.shape[0]
        pitch = ts + 1
        for t in range(ts):
            for j in range(scr.shape[0]):
                scr[j, pl.ds(t, SUBLANES, stride=pitch), :] = x_ref[t, :, j * LANES:(j + 1) * LANES]
        x = jnp.concatenate(
            [jnp.concatenate([scr[j, b * pitch:b * pitch + ts, :] for b in range(SUBLANES)], axis=0)
             for j in range(scr.shape[0])], axis=1)
    elif in_layout == "bt":
        x = x_ref[...].reshape(-1, D_MODEL)
    else:
        x = x_ref[...]
    sub = min(x.shape[0], FFN_CHUNK_ROWS)
    x_parts = [x[r:r + sub] for r in range(0, x.shape[0], sub)]
    h_parts = [_rms(xp, nw_ref[...]).astype(BF16) for xp in x_parts]
    outs = []
    for xp, h in zip(x_parts, h_parts):
        a = (_silu(_dot(h, wg_ref[...])) * _dot(h, wu_ref[...])).astype(BF16)
        xo = xp + FFN_RES * _dot(a, wd_ref[...])
        outs.append(_rms(xo, fw_ref[...]) if has_final else xo)
    x = jnp.concatenate(outs, axis=0) if len(outs) > 1 else outs[0]
    if out_layout == "tb":
        ts = o_ref.shape[0]
        pitch = ts + 1
        for j in range(scr.shape[0]):
            for b in range(SUBLANES):
                scr[j, b * pitch:b * pitch + ts, :] = x[b * ts:(b + 1) * ts, j * LANES:(j + 1) * LANES]
        for t in range(ts):
            for j in range(scr.shape[0]):
                o_ref[t, :, j * LANES:(j + 1) * LANES] = scr[j, pl.ds(t, SUBLANES, stride=pitch), :]
    elif out_layout == "bt":
        o_ref[...] = x.reshape(o_ref.shape)
    else:
        o_ref[...] = x


def _row_spec(layout, tm, n_inner):
    if layout == "bs":
        return pl.BlockSpec((tm, D_MODEL), lambda b, i: (b * n_inner + i, 0))
    ts = tm // SUBLANES
    if layout == "bt":
        return pl.BlockSpec((None, SUBLANES, ts, D_MODEL), lambda b, i: (b, 0, i, 0))
    return pl.BlockSpec((ts, None, SUBLANES, D_MODEL), lambda b, i: (i, b, 0, 0))


def _ffn(x, nw, wg, wu, wd, layer, *, batch, seq, in_layout="bs", out_layout="bs", tm=FFN_BIG_ROWS, final_w=None):
    if in_layout == "bs":
        grid = (batch, seq // tm)
        n_inner = seq // tm
    else:
        grid = (batch // SUBLANES, seq * SUBLANES // tm)
        n_inner = None
    args = [x]
    specs = [_row_spec(in_layout, tm, n_inner)]
    args += [nw, wg, wu, wd]
    specs += [_const_spec(nw.shape), _layer_spec(wg.shape, layer), _layer_spec(wu.shape, layer),
              _layer_spec(wd.shape, layer)]
    if final_w is not None:
        args.append(final_w)
        specs.append(_const_spec(final_w.shape))
    out_shape = {"bs": (batch * seq, D_MODEL),
                 "bt": (batch // SUBLANES, SUBLANES, seq, D_MODEL),
                 "tb": (seq, batch // SUBLANES, SUBLANES, D_MODEL)}[out_layout]
    scratch = ([pltpu.VMEM((D_MODEL // LANES, tm + SUBLANES, LANES), F32)]
               if "tb" in (in_layout, out_layout) else [])
    return pl.pallas_call(
        functools.partial(_ffn_kernel, has_final=final_w is not None,
                          in_layout=in_layout, out_layout=out_layout),
        grid=grid,
        in_specs=specs,
        out_specs=_row_spec(out_layout, tm, n_inner),
        out_shape=jax.ShapeDtypeStruct(out_shape, F32),
        scratch_shapes=scratch,
        compiler_params=_cparams(("parallel", "parallel")),
        name="ffn",
    )(*args)


def _inproj_kernel(x_ref, nw_ref, ws_ref, wm_ref, wdt_ref, wdtT_ref,
                   z_ref, xbc_ref, mx_ref, mz_ref, dt_ref, dtT_ref):
    u = _rms(x_ref[...], nw_ref[...]).astype(BF16)
    ps = _dot(u, ws_ref[...])
    z_ref[...] = ps[:, :SSD_D_INNER]
    xbc_ref[...] = ps[:, SSD_D_INNER:]
    pm = _dot(u, wm_ref[...])
    mx_ref[...] = pm[:, :ML_D_INNER]
    mz_ref[...] = pm[:, ML_D_INNER:]
    dt_ref[...] = _dot(u, wdt_ref[...])
    dtT_ref[...] = _dot_nt(wdtT_ref[...], u)


def _inproj(x, nw, w_ssd, w_ml, w_dt, w_dtT, *, tm=512):
    rows = x.shape[0]
    row = lambda w: pl.BlockSpec((tm, w), lambda i: (i, 0))
    consts = [nw, w_ssd, w_ml, w_dt, w_dtT]
    widths = [SSD_D_INNER, SSD_CONV_CH, ML_D_INNER, ML_D_INNER, LANES]
    return pl.pallas_call(
        _inproj_kernel,
        grid=(rows // tm,),
        in_specs=[row(D_MODEL)] + [_const_spec(a.shape) for a in consts],
        out_specs=[row(w) for w in widths] + [pl.BlockSpec((SSD_HEADS, tm), lambda i: (0, i))],
        out_shape=[jax.ShapeDtypeStruct((rows, w), F32) for w in widths]
                  + [jax.ShapeDtypeStruct((SSD_HEADS, rows), F32)],
        compiler_params=_cparams(("parallel",)),
        name="inproj",
    )(x, *consts)


def _causal_conv_silu(cbuf, x_ref, w_ref, b_ref):
    L = x_ref.shape[0]
    cbuf[CONV_HALO:CONV_HALO + L, :] = x_ref[...]
    ext = cbuf[...]
    prev = pltpu.roll(ext, 1, axis=0)
    near = ext * w_ref[3:4, :] + prev * w_ref[2:3, :]
    far = ext * w_ref[1:2, :] + prev * w_ref[0:1, :]
    acc = b_ref[...] + near[CONV_HALO:, :] + pltpu.roll(far, 2, axis=0)[CONV_HALO:, :]
    cbuf[0:CONV_HALO, :] = ext[L:L + CONV_HALO, :]
    return _silu(acc)


def _ssd_prepare(xbc_ref, dt_ref, dtT_ref, cw_ref, cb_ref, dtb_ref, dtbT_ref, alog_ref, alogT_ref, cbuf):
    L = CHUNK
    xbc = _causal_conv_silu(cbuf, xbc_ref, cw_ref, cb_ref)
    dt = _softplus(dt_ref[...] + dtb_ref[...])
    adt = dt * (-jnp.exp(alog_ref[...]))
    dtT = _softplus(dtT_ref[...] + dtbT_ref[...])
    adtT = dtT * (-jnp.exp(alogT_ref[...]))
    a_cs = _dot_f32(_tri(L, True).astype(F32), adt)
    a_csT = _dot_f32(adtT, _tri(L, False).astype(F32))
    w_st = dt * jnp.exp(a_cs[L - 1:L, :] - a_cs)
    return dict(xs=xbc[:, :SSD_D_INNER], bm=xbc[:, SSD_D_INNER:SSD_D_INNER + SSD_BC],
                cm=xbc[:, SSD_D_INNER + SSD_BC:], dtT=dtT, a_cs=a_cs, a_csT=a_csT, w_st=w_st)


def _ssd_group(g, pre, z_ref, dexp_ref, nw_ref, o_ref, st_ref):
    L = CHUNK
    xs, bm, cm, dtT, a_cs, a_csT, w_st = (pre[k] for k in ("xs", "bm", "cm", "dtT", "a_cs", "a_csT", "w_st"))
    tril = _tri(L, True)
    left = lax.broadcasted_iota(jnp.int32, (L, LANES), 1) < SSD_HEAD_DIM

    def col(a, h):
        return jnp.broadcast_to(a[:, h:h + 1], (L, LANES))

    heads_per_group = SSD_HEADS // SSD_GROUPS
    gs = slice(g * SSD_GROUP_W, (g + 1) * SSD_GROUP_W)
    bm_g = bm[:, g * SSD_D_STATE:(g + 1) * SSD_D_STATE]
    cm_g = cm[:, g * SSD_D_STATE:(g + 1) * SSD_D_STATE].astype(BF16)
    cb = _dot_nt(cm_g, bm_g.astype(BF16))
    st_prev = st_ref[:, gs]
    y_off = _dot(cm_g, st_prev.astype(BF16))
    diag, acs_pairs, xdec_pairs = [], [], []
    for hp in range(heads_per_group // 2):
        h0 = g * heads_per_group + 2 * hp
        cols = [col(a_cs, h0), col(a_cs, h0 + 1)]
        ms = []
        for h, acs_col in zip((h0, h0 + 1), cols):
            lm = jnp.exp(jnp.where(tril, acs_col - a_csT[h:h + 1, :], -jnp.inf))
            ms.append((cb * lm * dtT[h:h + 1, :]).astype(BF16))
        slab = xs[:, h0 * SSD_HEAD_DIM:(h0 + 2) * SSD_HEAD_DIM]
        rhs = jnp.concatenate([jnp.where(left, slab, 0.0), jnp.where(left, 0.0, slab)], axis=0)
        diag.append(_dot(jnp.concatenate(ms, axis=1), rhs.astype(BF16)))
        acs_pairs.append(jnp.where(left, cols[0], cols[1]))
        xdec_pairs.append(slab * jnp.where(left, col(w_st, h0), col(w_st, h0 + 1)))
    acs_e = jnp.concatenate(acs_pairs, axis=1)
    y = jnp.concatenate(diag, axis=1) + y_off * jnp.exp(acs_e) + dexp_ref[:, gs] * xs[:, gs]
    xd_dec = jnp.concatenate(xdec_pairs, axis=1).astype(BF16)
    st_ref[:, gs] = st_prev * jnp.exp(acs_e[L - 1:L, :]) + _dot(bm_g.T.astype(BF16), xd_dec)
    v = y * _silu(z_ref[:, gs])
    out = v * lax.rsqrt(jnp.mean(v * v, axis=-1, keepdims=True) + EPS) * nw_ref[:, gs]
    o_ref[:, gs] = out.astype(o_ref.dtype)


def _mlstm_project(mx_ref, cbuf, cw_ref, cb_ref, wq_ref, wk_ref, wv_ref, wif_ref, bif_ref):
    L = CHUNK
    mx = mx_ref[...]
    xc = _causal_conv_silu(cbuf, mx_ref, cw_ref, cb_ref)
    xc_b = xc.astype(BF16)
    mx_b = mx.astype(BF16)
    tile = 2 * LANES
    q = jnp.concatenate([_dot(xc_b[:, t * tile:(t + 1) * tile], wq_ref[t]) for t in range(ML_HEADS)], axis=1)
    k = jnp.concatenate([_dot(xc_b[:, t * tile:(t + 1) * tile], wk_ref[t]) for t in range(ML_HEADS)], axis=1)
    v = jnp.concatenate([_dot(mx_b[:, t * tile:(t + 1) * tile], wv_ref[t]) for t in range(ML_HEADS)], axis=1)
    qkv = jnp.concatenate([q, k, v], axis=1).astype(BF16)
    gates = _dot(qkv, wif_ref[...]) + bif_ref[...]
    gatesT = gates.T[0:2 * ML_HEADS, :]
    logf = -_softplus(-gates)
    logfT = -_softplus(-gatesT)
    bcum = _dot_f32(_tri(L, True).astype(F32), logf)
    bcumT = _dot_f32(logfT, _tri(L, False).astype(F32))
    return dict(xc=xc, qkv=qkv, q=q, k=k, gates=gates, gatesT=gatesT, bcum=bcum, bcumT=bcumT)


def _mlstm_head(h, pre, ct_ref, n_ref, m_ref):
    L = CHUNK
    tril = _tri(L, True)
    qkv, gates, gatesT, bcum, bcumT = pre["qkv"], pre["gates"], pre["gatesT"], pre["bcum"], pre["bcumT"]
    hs = slice(h * ML_HEAD_DIM, (h + 1) * ML_HEAD_DIM)
    qh = qkv[:, hs]
    kh = qkv[:, ML_D_INNER + hs.start:ML_D_INNER + hs.stop]
    vh = qkv[:, 2 * ML_D_INNER + hs.start:2 * ML_D_INNER + hs.stop]
    bc = jnp.broadcast_to(bcum[:, ML_HEADS + h:ML_HEADS + h + 1], (L, LANES))
    ig = jnp.broadcast_to(gates[:, h:h + 1], (L, LANES))
    m_prev = m_ref[0:1, :]
    n_prev = n_ref[0:1, :]
    ct_prev = ct_ref[...]

    dlog = jnp.where(tril, bc - bcumT[ML_HEADS + h:ML_HEADS + h + 1, :] + gatesT[h:h + 1, :], -jnp.inf)
    m_inter = bc + m_prev
    m_t = jnp.maximum(jnp.max(dlog, axis=-1, keepdims=True), m_inter)
    scores = _dot_nt(qh, kh) * jnp.exp(dlog - m_t)
    inter_w = jnp.exp(m_inter - m_t)
    num = _dot(scores.astype(BF16), vh) + _rep(inter_w, 2) * _dot(qh, ct_prev.astype(BF16))
    den = (jnp.sum(scores, axis=-1, keepdims=True)
           + inter_w * jnp.sum(pre["q"][:, hs] * n_prev, axis=-1, keepdims=True))
    hout = num / _rep(jnp.maximum(jnp.abs(den), jnp.exp(-m_t)), 2)

    b_last = bc[L - 1:L, :]
    w_state = b_last - bc + ig
    m_new = jnp.maximum(b_last + m_prev, jnp.max(w_state, axis=0, keepdims=True))
    decay = _rep(jnp.exp(b_last + m_prev - m_new), 2)
    kw = pre["k"][:, hs] * _rep(jnp.exp(w_state - m_new), 2)
    ct_ref[...] = decay * ct_prev + _dot(kw.T.astype(BF16), vh)
    n_ref[...] = jnp.broadcast_to(decay * n_prev + jnp.sum(kw, axis=0, keepdims=True), n_ref.shape)
    m_ref[...] = jnp.broadcast_to(m_new, m_ref.shape)

    mu = jnp.mean(hout, axis=-1, keepdims=True)
    d = hout - mu
    var = jnp.mean(d * d, axis=-1, keepdims=True)
    return d * lax.rsqrt(var + EPS)


N_SSD_CONSTS = 8
N_ML_CONSTS = 9


def _mixer_kernel(*refs):
    n = SEQS_PER_STEP
    it = iter(refs)
    take = lambda k: [next(it) for _ in range(k)]
    xbc_ref, dt_ref = take(2)
    dtT_refs = take(n)
    (z_ref,) = take(1)
    ssd_consts = take(N_SSD_CONSTS)
    mx_ref, mz_ref = take(2)
    cw_ref, cb_ref, wq_ref, wk_ref, wv_ref, wif_ref, bif_ref, nw_ref, skip_ref = take(N_ML_CONSTS)
    x_ref, wos_ref, wom_ref = take(3)
    o_ref, ys_ref, cbuf_s, st_ref, cbuf_m, ct_ref, n_ref, m_ref = take(8)

    @pl.when(pl.program_id(1) == 0)
    def _():
        st_ref[...] = jnp.zeros(st_ref.shape, F32)
        cbuf_s[:, 0:CONV_HALO, :] = jnp.zeros((n, CONV_HALO, cbuf_s.shape[2]), F32)
        ct_ref[...] = jnp.zeros(ct_ref.shape, F32)
        n_ref[...] = jnp.zeros(n_ref.shape, F32)
        m_ref[...] = jnp.full(m_ref.shape, -1e30, F32)
        cbuf_m[:, 0:CONV_HALO, :] = jnp.zeros((n, CONV_HALO, cbuf_m.shape[2]), F32)

    scw_ref, scb_ref, dtb_ref, dtbT_ref, alog_ref, alogT_ref, dexp_ref, snw_ref = ssd_consts

    def prepare(seqs):
        return {s: (_mlstm_project(mx_ref.at[s], cbuf_m.at[s], cw_ref, cb_ref, wq_ref, wk_ref, wv_ref,
                                   wif_ref, bif_ref),
                    _ssd_prepare(xbc_ref.at[s], dt_ref.at[s], dtT_refs[s], scw_ref, scb_ref,
                                 dtb_ref, dtbT_ref, alog_ref, alogT_ref, cbuf_s.at[s])) for s in seqs}

    def mix(seqs, pre):
        heads = {s: [] for s in seqs}
        for h in range(ML_HEADS):
            for s in seqs:
                if h < SSD_GROUPS:
                    _ssd_group(h, pre[s][1], z_ref.at[s], dexp_ref, snw_ref, ys_ref.at[s], st_ref.at[s])
                heads[s].append(_mlstm_head(h, pre[s][0], ct_ref.at[s, h], n_ref.at[s, h], m_ref.at[s, h]))
        ym = {}
        for s in seqs:
            hm = jnp.concatenate(heads[s], axis=1) * nw_ref[...]
            ym[s] = ((hm + skip_ref[...] * pre[s][0]["xc"]) * _silu(mz_ref[s])).astype(BF16)
        return ym

    def project_out(seqs, ym):
        ys = jnp.concatenate([ys_ref[s] for s in seqs], axis=0)
        yml = jnp.concatenate([ym[s] for s in seqs], axis=0)
        out = _dot(ys, wos_ref[...]) + _dot(yml, wom_ref[...])
        for i, s in enumerate(seqs):
            o_ref[s] = x_ref[s] + out[i * CHUNK:(i + 1) * CHUNK]

    waves = [list(range(n // 2))] + [[s] for s in range(n // 2, n)]
    pending = None
    for wave in waves:
        pre_w = prepare(wave)
        if pending is not None:
            project_out(*pending)
        pending = (wave, mix(wave, pre_w))
    project_out(*pending)


def _mixer(x, xbc, dt, dtT, z, ssd_consts, mx, mz, ml_consts, w_out_ssd, w_out_ml, *, batch, seq):
    L = CHUNK
    nc = seq // L
    n = SEQS_PER_STEP
    seqs = lambda t: t.reshape(batch // n, n, seq, t.shape[-1])
    row = lambda w: pl.BlockSpec((None, n, L, w), lambda b, c: (b, 0, c, 0))
    dtT_spec = lambda s: pl.BlockSpec((SSD_HEADS, L), lambda b, c: (0, (b * n + s) * nc + c))
    assert len(ssd_consts) == N_SSD_CONSTS and len(ml_consts) == N_ML_CONSTS
    out = pl.pallas_call(
        _mixer_kernel,
        grid=(batch // n, nc),
        in_specs=[row(SSD_CONV_CH), row(LANES)] + [dtT_spec(s) for s in range(n)] + [row(SSD_D_INNER)]
                 + [_const_spec(a.shape) for a in ssd_consts]
                 + [row(ML_D_INNER)] * 2 + [_const_spec(a.shape) for a in ml_consts]
                 + [row(D_MODEL), _const_spec(w_out_ssd.shape), _const_spec(w_out_ml.shape)],
        out_specs=row(D_MODEL),
        out_shape=jax.ShapeDtypeStruct((batch // n, n, seq, D_MODEL), F32),
        scratch_shapes=[pltpu.VMEM((n, L, SSD_D_INNER), BF16),
                        pltpu.VMEM((n, CONV_HALO + L, SSD_CONV_CH), F32),
                        pltpu.VMEM((n, SSD_D_STATE, SSD_D_INNER), F32),
                        pltpu.VMEM((n, CONV_HALO + L, ML_D_INNER), F32),
                        pltpu.VMEM((n, ML_HEADS, ML_HEAD_DIM, ML_HEAD_DIM), F32),
                        pltpu.VMEM((n, ML_HEADS, SUBLANES, ML_HEAD_DIM), F32),
                        pltpu.VMEM((n, ML_HEADS, SUBLANES, LANES), F32)],
        compiler_params=_cparams(("parallel", "arbitrary")),
        name="mixer",
    )(seqs(xbc), seqs(dt), *([dtT] * n), seqs(z), *ssd_consts, seqs(mx), seqs(mz), *ml_consts,
      seqs(x), w_out_ssd, w_out_ml)
    return out.reshape(batch * seq, D_MODEL)


def _s5_param_kernel(are_ref, aim_ref, lstep_ref, bre_ref, bim_ref, cre_ref, cim_ref, bs_ref, cs_ref, lam_ref):
    a_re, a_im = are_ref[...], aim_ref[...]
    step = jnp.exp(lstep_ref[...])
    mag = jnp.exp(a_re * step)
    lam_re = mag * jnp.cos(a_im * step)
    lam_im = mag * jnp.sin(a_im * step)
    den = a_re * a_re + a_im * a_im
    coef_re = ((lam_re - 1.0) * a_re + lam_im * a_im) / den
    coef_im = (lam_im * a_re - (lam_re - 1.0) * a_im) / den
    b_re, b_im = bre_ref[...], bim_ref[...]
    bb_re = coef_re * b_re - coef_im * b_im
    bb_im = coef_re * b_im + coef_im * b_re

    rows, wide = S5_SLAB_IN, S5_SLAB_ST
    r = lax.broadcasted_iota(jnp.int32, (rows, wide), 0)
    c = lax.broadcasted_iota(jnp.int32, (rows, wide), 1)
    same_group = r // S5_GROUP == c // S5_STATE
    sel = (lax.broadcasted_iota(jnp.int32, (S5_STATE, wide), 1) % S5_STATE
           == lax.broadcasted_iota(jnp.int32, (S5_STATE, wide), 0))

    def spread(t, exact):
        if exact:
            wide_t = _dot_f32(t, sel.astype(F32))
        else:
            wide_t = _dot(t.astype(BF16), sel.astype(BF16))
        return jnp.where(same_group, wide_t, 0.0)

    bs_ref[...] = jnp.concatenate([spread(bb_re, False), spread(bb_im, False)], axis=1).astype(BF16)
    cs_ref[...] = jnp.concatenate([spread(cre_ref[...], False).T, spread(-cim_ref[...], False).T],
                                  axis=0).astype(BF16)
    first = r % S5_GROUP == 0
    lam = jnp.concatenate([jnp.sum(jnp.where(first, spread(lam_re, True), 0.0), axis=0, keepdims=True),
                           jnp.sum(jnp.where(first, spread(lam_im, True), 0.0), axis=0, keepdims=True)], axis=1)
    lam_ref[...] = jnp.broadcast_to(lam, lam_ref.shape)


def _s5_params(a_re, a_im, log_step, b_re, b_im, c_re, c_im):
    rep = lambda t: jnp.repeat(t, S5_GROUP, axis=0)
    are, aim = rep(a_re), rep(a_im)
    lstep = jnp.broadcast_to(rep(log_step[:, None]), are.shape)
    flat = lambda t: t.reshape(D_MODEL, S5_STATE)
    args = [are, aim, lstep, flat(jnp.swapaxes(b_re, 1, 2)), flat(jnp.swapaxes(b_im, 1, 2)),
            flat(c_re), flat(c_im)]
    bs, cs, lam = pl.pallas_call(
        _s5_param_kernel,
        grid=(S5_SLABS,),
        in_specs=[pl.BlockSpec((S5_SLAB_IN, S5_STATE), lambda k: (k, 0))] * len(args),
        out_specs=[pl.BlockSpec((None, S5_SLAB_IN, 2 * S5_SLAB_ST), lambda k: (k, 0, 0)),
                   pl.BlockSpec((None, 2 * S5_SLAB_ST, S5_SLAB_IN), lambda k: (k, 0, 0)),
                   pl.BlockSpec((None, SUBLANES, 2 * S5_SLAB_ST), lambda k: (k, 0, 0))],
        out_shape=[jax.ShapeDtypeStruct((S5_SLABS, S5_SLAB_IN, 2 * S5_SLAB_ST), BF16),
                   jax.ShapeDtypeStruct((S5_SLABS, 2 * S5_SLAB_ST, S5_SLAB_IN), BF16),
                   jax.ShapeDtypeStruct((S5_SLABS, SUBLANES, 2 * S5_SLAB_ST), F32)],
        compiler_params=_cparams(("parallel",)),
        name="s5_params",
    )(*args)
    return bs, lam, cs


def _s5_kernel(x_ref, nw_ref, bs_ref, lam_ref, cs_ref, dsk_ref, wa_ref, ba_ref, wb_ref, bb_ref, o_ref,
               bu_ref, xs_ref, st_ref, *, batch):
    rows = x_ref.shape[0]
    steps = rows // batch

    @pl.when(pl.program_id(0) == 0)
    def _():
        st_ref[...] = jnp.zeros(st_ref.shape, F32)

    x = x_ref[...]
    u = _rms(x, nw_ref[...])
    u_b = u.astype(BF16)
    def bu_slab(k):
        bu_ref[k % 2] = _dot(u_b[:, k * S5_SLAB_IN:(k + 1) * S5_SLAB_IN], bs_ref[k])

    ys = []
    bu_slab(0)
    for k in range(S5_SLABS):
        if k + 1 < S5_SLABS:
            bu_slab(k + 1)
        for w in range(S5_SLAB_ST // S5_SCAN_W):
            re_l = slice(w * S5_SCAN_W, (w + 1) * S5_SCAN_W)
            im_l = slice(S5_SLAB_ST + w * S5_SCAN_W, S5_SLAB_ST + (w + 1) * S5_SCAN_W)
            lr = jnp.broadcast_to(lam_ref[k, 0:1, re_l], (batch, S5_SCAN_W))
            li = jnp.broadcast_to(lam_ref[k, 0:1, im_l], (batch, S5_SCAN_W))
            xr, xi = st_ref[k, :, re_l], st_ref[k, :, im_l]
            for t in range(steps):
                rows_t = slice(t * batch, (t + 1) * batch)
                xr, xi = (lr * xr - li * xi + bu_ref[k % 2, rows_t, re_l],
                          lr * xi + li * xr + bu_ref[k % 2, rows_t, im_l])
                xs_ref[k % 2, rows_t, re_l] = xr.astype(BF16)
                xs_ref[k % 2, rows_t, im_l] = xi.astype(BF16)
            st_ref[k, :, re_l] = xr
            st_ref[k, :, im_l] = xi
        ys.append(_dot(xs_ref[k % 2], cs_ref[k]))
    y = jnp.concatenate(ys, axis=1) + dsk_ref[...] * u
    g = jax.nn.gelu(y).astype(BF16)
    out = (_dot(g, wa_ref[...]) + ba_ref[...]) * _sigmoid(_dot(g, wb_ref[...]) + bb_ref[...])
    o_ref[...] = x + out


def _s5(x_sb, nw, b_slabs, lam_slabs, c_slabs, dsk, wa, ba, wb, bb, *, batch, seq, ts=32):
    rows = ts * batch
    consts = [nw, b_slabs, lam_slabs, c_slabs, dsk, wa, ba, wb, bb]
    return pl.pallas_call(
        functools.partial(_s5_kernel, batch=batch),
        grid=(seq // ts,),
        in_specs=[pl.BlockSpec((rows, D_MODEL), lambda i: (i, 0))] + [_const_spec(a.shape) for a in consts],
        out_specs=pl.BlockSpec((rows, D_MODEL), lambda i: (i, 0)),
        out_shape=jax.ShapeDtypeStruct((seq * batch, D_MODEL), F32),
        scratch_shapes=[pltpu.VMEM((2, rows, 2 * S5_SLAB_ST), F32),
                        pltpu.VMEM((2, rows, 2 * S5_SLAB_ST), BF16),
                        pltpu.VMEM((S5_SLABS, batch, 2 * S5_SLAB_ST), F32)],
        compiler_params=_cparams(("arbitrary",)),
        name="s5",
    )(x_sb, *consts)


def _blockdiag_tiles(w):
    nb = w.shape[0]
    tile = nb * ML_QKV_BLOCK // ML_HEADS
    rows = jnp.swapaxes(w, 1, 2).reshape(nb * ML_QKV_BLOCK, ML_QKV_BLOCK)
    sel = (jnp.arange(tile)[None, :] % ML_QKV_BLOCK == jnp.arange(ML_QKV_BLOCK)[:, None]).astype(w.dtype)
    wide = jnp.dot(rows, sel, precision=HIGHEST)
    blk_r = (jnp.arange(nb * ML_QKV_BLOCK) % tile) // ML_QKV_BLOCK
    blk_c = jnp.arange(tile) // ML_QKV_BLOCK
    wide = jnp.where(blk_r[:, None] == blk_c[None, :], wide, 0.0)
    return wide.reshape(ML_HEADS, tile, tile)


def _pad_lanes(t, n=LANES):
    return jnp.pad(t, ((0, 0), (0, n - t.shape[1])))


def kernel(x, ffn1_norm, ffn1_w_gate, ffn1_w_up, ffn1_w_down, mix_norm, ffn2_norm, ffn2_w_gate, ffn2_w_up, ffn2_w_down, hy_w_in, ssd_conv_w, ssd_conv_b, ssd_dt_bias, ssd_a_log, ssd_d, ssd_norm_w, ml_conv_w, ml_conv_b, ml_w_q, ml_w_k, ml_w_v, ml_w_if, ml_b_if, ml_norm_w, ml_skip, hy_w_out, s5_a_re, s5_a_im, s5_log_step, s5_b_re, s5_b_im, s5_c_re, s5_c_im, s5_d, s5_w_a, s5_b_a, s5_w_b, s5_b_b, final_norm):
    batch, seq, _ = x.shape
    assert seq % 512 == 0 and batch % 8 == 0
    row = lambda t: t.reshape(1, -1)
    bf = lambda t: t.astype(BF16)
    xf = x.reshape(batch * seq, D_MODEL)

    g1, u1, d1, g2, u2, d2 = _to_bf16([ffn1_w_gate, ffn1_w_up, ffn1_w_down, ffn2_w_gate, ffn2_w_up, ffn2_w_down])
    x1 = _ffn(xf, row(ffn1_norm[0]), g1, u1, d1, 0, batch=batch, seq=seq)

    w_in = hy_w_in[0]
    o1 = SSD_D_INNER
    o2 = o1 + SSD_CONV_CH
    o3 = o2 + SSD_HEADS
    w_dt = w_in[:, o2:o3]
    z_s, xbc, m_x, m_z, dt_raw, dt_rawT = _inproj(x1, row(mix_norm[0]), bf(w_in[:, :o2]), bf(w_in[:, o3:]),
                                                  bf(_pad_lanes(w_dt)), bf(w_dt.T))

    ssd_consts = [ssd_conv_w[0], row(ssd_conv_b[0]),
                  _pad_lanes(row(ssd_dt_bias[0])), ssd_dt_bias[0].reshape(-1, 1),
                  _pad_lanes(row(ssd_a_log[0])), ssd_a_log[0].reshape(-1, 1),
                  row(jnp.repeat(ssd_d[0], SSD_HEAD_DIM)), row(ssd_norm_w[0])]

    k_scale = 1.0 / math.sqrt(ML_HEAD_DIM)
    w_if = ml_w_if[0]
    w_if = jnp.concatenate([w_if[:ML_D_INNER], w_if[ML_D_INNER:2 * ML_D_INNER] / k_scale, w_if[2 * ML_D_INNER:]],
                           axis=0)
    b_if = ml_b_if[0]
    ml_consts = [ml_conv_w[0], row(ml_conv_b[0]),
                 bf(_blockdiag_tiles(ml_w_q[0])), bf(_blockdiag_tiles(ml_w_k[0]) * k_scale),
                 bf(_blockdiag_tiles(ml_w_v[0])), bf(_pad_lanes(w_if)),
                 _pad_lanes(row(b_if)), row(ml_norm_w[0]), row(ml_skip[0])]
    w_out = bf(hy_w_out[0])
    x2 = _mixer(x1, xbc, dt_raw, dt_rawT, z_s, ssd_consts, m_x, m_z, ml_consts,
                w_out[:SSD_D_INNER], w_out[SSD_D_INNER:], batch=batch, seq=seq)
    x3 = _ffn(x2, row(ffn2_norm[0]), g2, u2, d2, 0, batch=batch, seq=seq)

    x4 = _ffn(x3.reshape(batch // SUBLANES, SUBLANES, seq, D_MODEL), row(ffn1_norm[1]), g1, u1, d1, 1,
              batch=batch, seq=seq, in_layout="bt", out_layout="tb")
    b_slabs, lam_slabs, c_slabs = _s5_params(s5_a_re[0], s5_a_im[0], s5_log_step[0], s5_b_re[0], s5_b_im[0],
                                             s5_c_re[0], s5_c_im[0])
    x5 = _s5(x4.reshape(seq * batch, D_MODEL), row(mix_norm[1]), b_slabs, lam_slabs, c_slabs, row(s5_d[0]),
             bf(s5_w_a[0]), row(s5_b_a[0]), bf(s5_w_b[0]), row(s5_b_b[0]), batch=batch, seq=seq)
    out = _ffn(x5.reshape(seq, batch // SUBLANES, SUBLANES, D_MODEL), row(ffn2_norm[1]), g2, u2, d2, 1,
               batch=batch, seq=seq, in_layout="tb", out_layout="bt", final_w=row(final_norm))
    return out.reshape(batch, seq, D_MODEL)
```
